```python
import math
import jax, jax.numpy as jnp
from jax import lax
import numpy as np

D_MODEL = 1024
BATCH = 2
SEQ = 8192
DEPTH = 2

CHUNK = 64
Q_BLOCK = 128
PLE_DIM = 256
MLA_HEADS = 4
MLA_NOPE = 64
MLA_ROPE = 32
MLA_V = 64
MLA_Q_RANK = 192
MLA_KV_RANK = 128
ROPE_THETA = 10000.0
DIFF_HEADS = 4
DIFF_QK = 64
DIFF_V = 2 * DIFF_QK
SB_HEADS = 4
SB_D = 64
REL_BUCKETS = 32
REL_MAX_DIST = 128
N_EXPERTS = 16
N_GROUPS = 4
EXPERTS_PER_GROUP = N_EXPERTS // N_GROUPS
TOP_K = 2
D_EXPERT = 512
MLA_IN = MLA_Q_RANK + MLA_KV_RANK + MLA_ROPE
DIFF_IN = 2 * DIFF_HEADS * 2 * DIFF_QK + DIFF_HEADS * DIFF_V
SB_IN = 3 * SB_HEADS * SB_D
D_IN = MLA_IN + DIFF_IN + SB_IN
D_MIX = MLA_HEADS * MLA_V + DIFF_HEADS * DIFF_V + SB_HEADS * SB_D
DEEPNORM_ALPHA = (2 * DEPTH) ** 0.25
DEEPNORM_BETA = (8 * DEPTH) ** -0.25
EPS = 1e-5
NEG_INF = -1e30

kernel_name = "hybrid_chunk_causal_mla_diff_stickbreak_moe"


def layer_norm(x, g, b):
    xf = x.astype(jnp.float32)
    mu = jnp.mean(xf, -1, keepdims=True)
    var = jnp.mean(jnp.square(xf - mu), -1, keepdims=True)
    y = (xf - mu) * lax.rsqrt(var + EPS) * g.astype(jnp.float32) + b.astype(jnp.float32)
    return y.astype(x.dtype)


def rms_norm(x, g):
    xf = x.astype(jnp.float32)
    y = xf * lax.rsqrt(jnp.mean(xf * xf, -1, keepdims=True) + EPS) * g.astype(jnp.float32)
    return y.astype(x.dtype)


def rope(x, pos):
    half = x.shape[-1] // 2
    inv = ROPE_THETA ** (-jnp.arange(half, dtype=jnp.float32) / half)
    ang = pos.astype(jnp.float32)[..., None] * inv
    cos, sin = jnp.cos(ang), jnp.sin(ang)
    xf = x.astype(jnp.float32)
    x1, x2 = xf[..., :half], xf[..., half:]
    return jnp.concatenate([x1 * cos - x2 * sin, x1 * sin + x2 * cos], -1).astype(x.dtype)


def t5_bucket(rel):
    nb = REL_BUCKETS // 2
    max_exact = nb // 2
    side = jnp.where(rel > 0, nb, 0)
    n = jnp.abs(rel)
    nf = jnp.maximum(n, 1).astype(jnp.float32)
    large = max_exact + (jnp.log(nf / max_exact) / math.log(REL_MAX_DIST / max_exact)
                         * (nb - max_exact)).astype(jnp.int32)
    large = jnp.minimum(large, nb - 1)
    return side + jnp.where(n < max_exact, n, large)


def to_heads(a, n_heads):
    B, S, _ = a.shape
    return a.reshape(B, S, n_heads, -1).transpose(0, 2, 1, 3)


def merge_heads(a):
    B, H, S, d = a.shape
    return a.transpose(0, 2, 1, 3).reshape(B, S, H * d)


def chunk_mask(n, S):
    q_idx = n * Q_BLOCK + jnp.arange(Q_BLOCK)
    k_idx = jnp.arange(S)
    return (k_idx[None, :] // CHUNK) <= (q_idx[:, None] // CHUNK)


def sweep_query_blocks(block_fn, q_parts):
    B, H, S, _ = q_parts[0].shape
    nb = S // Q_BLOCK

    def to_blocks(a):
        return jnp.moveaxis(a.reshape(B, H, nb, Q_BLOCK, a.shape[-1]), 2, 0)

    out = lax.map(lambda args: block_fn(args[0], *args[1:]),
                  (jnp.arange(nb, dtype=jnp.int32),) + tuple(to_blocks(a) for a in q_parts))
    return jnp.moveaxis(out, 0, 2).reshape(B, H, S, out.shape[-1])


def mla_mixer(u, positions, q_norm, w_uq, kv_norm, w_ukv):
    S = u.shape[1]
    c_q, c_kv, k_r = jnp.split(u, [MLA_Q_RANK, MLA_Q_RANK + MLA_KV_RANK], axis=-1)
    q = to_heads(rms_norm(c_q, q_norm) @ w_uq, MLA_HEADS)
    q_nope = q[..., :MLA_NOPE]
    q_rope = rope(q[..., MLA_NOPE:], positions[:, None, :])
    kv = to_heads(rms_norm(c_kv, kv_norm) @ w_ukv, MLA_HEADS)
    k_nope, v = kv[..., :MLA_NOPE], kv[..., MLA_NOPE:]
    k_rope = rope(k_r, positions)
    scale = (MLA_NOPE + MLA_ROPE) ** -0.5

    def block(n, qn, qr):
        s = (jnp.einsum('bhqd,bhkd->bhqk', qn, k_nope)
             + jnp.einsum('bhqd,bkd->bhqk', qr, k_rope)).astype(jnp.float32) * scale
        a = jax.nn.softmax(jnp.where(chunk_mask(n, S), s, NEG_INF), axis=-1)
        return jnp.einsum('bhqk,bhkd->bhqd', a.astype(v.dtype), v)

    return merge_heads(sweep_query_blocks(block, (q_nope, q_rope)))


def diff_mixer(u, positions, rel_bias, lq1, lk1, lq2, lk2, subln, lam_init):
    B, S, _ = u.shape
    qk_w = DIFF_HEADS * 2 * DIFF_QK
    q, k, v = jnp.split(u, [qk_w, 2 * qk_w], axis=-1)
    q = q.reshape(B, S, DIFF_HEADS, 2, DIFF_QK).transpose(3, 0, 2, 1, 4)
    k = k.reshape(B, S, DIFF_HEADS, 2, DIFF_QK).transpose(3, 0, 2, 1, 4)
    v = to_heads(v, DIFF_HEADS)
    f = lambda a: a.astype(jnp.float32)
    lam = jnp.exp(jnp.sum(f(lq1) * f(lk1))) - jnp.exp(jnp.sum(f(lq2) * f(lk2))) + lam_init
    scale = DIFF_QK ** -0.5

    def block(n, q1, q2):
        mask = chunk_mask(n, S)
        pos_blk = lax.dynamic_slice_in_dim(positions, n * Q_BLOCK, Q_BLOCK, axis=1)
        rel = positions[:, None, :] - pos_blk[:, :, None]
        bias = jnp.moveaxis(rel_bias[t5_bucket(rel)], -1, 1).astype(jnp.float32)

        def attn_map(qq, kk):
            s = jnp.einsum('bhqd,bhkd->bhqk', qq, kk).astype(jnp.float32) * scale + bias
            return jax.nn.softmax(jnp.where(mask, s, NEG_INF), axis=-1)

        a = attn_map(q1, k[0]) - lam * attn_map(q2, k[1])
        return jnp.einsum('bhqk,bhkd->bhqd', a.astype(v.dtype), v)

    o = sweep_query_blocks(block, (q[0], q[1]))
    o = rms_norm(o, subln) * (1.0 - lam_init)
    return merge_heads(o)


def sb_mixer(u):
    S = u.shape[1]
    q, k, v = [to_heads(a, SB_HEADS) for a in jnp.split(u, 3, axis=-1)]
    scale = SB_D ** -0.5

    def block(n, qb):
        q_idx = n * Q_BLOCK + jnp.arange(Q_BLOCK)
        strict = jnp.arange(S)[None, :] < q_idx[:, None]
        z = jnp.einsum('bhqd,bhkd->bhqk', qb, k).astype(jnp.float32) * scale
        log_fail = jnp.where(strict, jax.nn.log_sigmoid(-z), 0.0)
        later = lax.cumsum(log_fail, axis=3, reverse=True) - log_fail
        w = jnp.where(strict, jnp.exp(jax.nn.log_sigmoid(z) + later), 0.0)
        return jnp.einsum('bhqk,bhkd->bhqd', w.astype(v.dtype), v)

    return merge_heads(sweep_query_blocks(block, (q,)))


def grouped_moe(h, router_w, router_b, w_gate, w_up, w_down):
    scores = jax.nn.sigmoid((h @ router_w).astype(jnp.float32))
    sel = scores + router_b.astype(jnp.float32)
    g = sel.reshape(sel.shape[:-1] + (N_GROUPS, EXPERTS_PER_GROUP))
    group_score = jnp.sum(lax.top_k(g, 2)[0], axis=-1)
    best_group = jnp.argmax(group_score, axis=-1)
    in_group = jnp.arange(N_GROUPS) == best_group[..., None]
    masked = jnp.where(in_group[..., None], g, -jnp.inf).reshape(sel.shape)
    _, idx = lax.top_k(masked, TOP_K)
    wts = jnp.take_along_axis(scores, idx, axis=-1)
    wts = wts / jnp.sum(wts, -1, keepdims=True)
    gate = jnp.sum(jax.nn.one_hot(idx, N_EXPERTS, dtype=jnp.float32) * wts[..., None], axis=-2)
    gate = gate.astype(h.dtype)
    out = jnp.zeros_like(h)
    for e in range(N_EXPERTS):
        hid = jax.nn.silu(h @ w_gate[e]) * (h @ w_up[e])
        out = out + gate[..., e:e + 1] * (hid @ w_down[e])
    return out


def setup_inputs(seed: int = 0) -> dict:
    key = jax.random.key(seed)
    ks = jax.random.split(key, 32)
    f32 = jnp.float32

    def nrm(k, shape, scale):
        return jax.random.normal(k, shape, f32) * scale

    x = nrm(ks[0], (BATCH, SEQ, D_MODEL), 1.0)
    p = nrm(ks[1], (DEPTH, BATCH, SEQ, PLE_DIM), 1.0)
    offset = jax.random.randint(ks[2], (BATCH, 1), 0, 64, dtype=jnp.int32) * CHUNK
    positions = (jnp.arange(SEQ, dtype=jnp.int32)[None, :] + offset).astype(jnp.int32)
    return {
        'x': x,
        'p': p,
        'positions': positions,
        'w_in': nrm(ks[3], (DEPTH, D_MODEL, D_IN), D_MODEL ** -0.5),
        'mla_q_norm': 1.0 + nrm(ks[4], (DEPTH, MLA_Q_RANK), 0.05),
        'mla_w_uq': nrm(ks[5], (DEPTH, MLA_Q_RANK, MLA_HEADS * (MLA_NOPE + MLA_ROPE)), MLA_Q_RANK ** -0.5),
        'mla_kv_norm': 1.0 + nrm(ks[6], (DEPTH, MLA_KV_RANK), 0.05),
        'mla_w_ukv': nrm(ks[7], (DEPTH, MLA_KV_RANK, MLA_HEADS * (MLA_NOPE + MLA_V)), MLA_KV_RANK ** -0.5),
        'diff_lambda_q1': nrm(ks[8], (DEPTH, DIFF_QK), 0.1),
        'diff_lambda_k1': nrm(ks[9], (DEPTH, DIFF_QK), 0.1),
        'diff_lambda_q2': nrm(ks[10], (DEPTH, DIFF_QK), 0.1),
        'diff_lambda_k2': nrm(ks[11], (DEPTH, DIFF_QK), 0.1),
        'diff_subln': 1.0 + nrm(ks[12], (DEPTH, DIFF_V), 0.05),
        'rel_bias': nrm(ks[13], (REL_BUCKETS, DIFF_HEADS), 0.5),
        'w_o': nrm(ks[14], (DEPTH, D_MIX, D_MODEL), D_MIX ** -0.5 * DEEPNORM_BETA),
        'ln1_g': 1.0 + nrm(ks[15], (DEPTH, D_MODEL), 0.05),
        'ln1_b': nrm(ks[16], (DEPTH, D_MODEL), 0.02),
        'router_w': nrm(ks[17], (D_MODEL, N_EXPERTS), D_MODEL ** -0.5),
        'router_b': nrm(ks[18], (N_EXPERTS,), 0.01),
        'w_gate': nrm(ks[19], (DEPTH, N_EXPERTS, D_MODEL, D_EXPERT), D_MODEL ** -0.5),
        'w_up': nrm(ks[20], (DEPTH, N_EXPERTS, D_MODEL, D_EXPERT), D_MODEL ** -0.5),
        'w_down': nrm(ks[21], (DEPTH, N_EXPERTS, D_EXPERT, D_MODEL), D_EXPERT ** -0.5 * DEEPNORM_BETA),
        'ple_proj': nrm(ks[22], (DEPTH, PLE_DIM, D_MODEL), PLE_DIM ** -0.5 * DEEPNORM_BETA),
        'ple_gate': nrm(ks[23], (DEPTH, D_MODEL, D_MODEL), D_MODEL ** -0.5),
        'ln2_g': 1.0 + nrm(ks[24], (DEPTH, D_MODEL), 0.05),
        'ln2_b': nrm(ks[25], (DEPTH, D_MODEL), 0.02),
    }


def reference(x, p, positions, w_in, mla_q_norm, mla_w_uq, mla_kv_norm, mla_w_ukv,
              diff_lambda_q1, diff_lambda_k1, diff_lambda_q2, diff_lambda_k2, diff_subln,
              rel_bias, w_o, ln1_g, ln1_b, router_w, router_b, w_gate, w_up, w_down,
              ple_proj, ple_gate, ln2_g, ln2_b):
    h = x
    for i in range(DEPTH):
        u = h @ w_in[i]
        u_mla, u_diff, u_sb = jnp.split(u, [MLA_IN, MLA_IN + DIFF_IN], axis=-1)
        y_mla = mla_mixer(u_mla, positions, mla_q_norm[i], mla_w_uq[i], mla_kv_norm[i], mla_w_ukv[i])
        lam_init = 0.8 - 0.6 * math.exp(-0.3 * i)
        y_diff = diff_mixer(u_diff, positions, rel_bias, diff_lambda_q1[i], diff_lambda_k1[i],
                            diff_lambda_q2[i], diff_lambda_k2[i], diff_subln[i], lam_init)
        y_sb = sb_mixer(u_sb)
        mix = jnp.concatenate([y_mla, y_diff, y_sb], axis=-1) @ w_o[i]
        h = layer_norm(DEEPNORM_ALPHA * h + mix, ln1_g[i], ln1_b[i])
        ffn = grouped_moe(h, router_w, router_b, w_gate[i], w_up[i], w_down[i])
        ple = jax.nn.sigmoid(h @ ple_gate[i]) * (p[i] @ ple_proj[i])
        h = layer_norm(DEEPNORM_ALPHA * h + ffn + ple, ln2_g[i], ln2_b[i])
    return h
```

```python
import functools
import math

import jax
import jax.numpy as jnp
from jax import lax
from jax.experimental import pallas as pl
from jax.experimental.pallas import tpu as pltpu

D_MODEL = 1024
CHUNK = 64
PLE_DIM = 256
MLA_HEADS = 4
MLA_NOPE = 64
MLA_ROPE = 32
MLA_V = 64
MLA_Q_RANK = 192
MLA_KV_RANK = 128
ROPE_THETA = 10000.0
DIFF_HEADS = 4
DIFF_QK = 64
DIFF_V = 2 * DIFF_QK
SB_HEADS = 4
SB_D = 64
REL_BUCKETS = 32
REL_MAX_DIST = 128
N_EXPERTS = 16
N_GROUPS = 4
EXPERTS_PER_GROUP = N_EXPERTS // N_GROUPS
D_EXPERT = 512
MLA_IN = MLA_Q_RANK + MLA_KV_RANK + MLA_ROPE
DIFF_IN = 2 * DIFF_HEADS * 2 * DIFF_QK + DIFF_HEADS * DIFF_V
SB_IN = 3 * SB_HEADS * SB_D
EPS = 1e-5
NEG_INF = -1e30

LANES = 128
VMEM_LIMIT = 56 * 1024 * 1024

UA_CKV = 0
UA_KR = 128
UA_KRS = 256
UA_CQ = 384
UA_W = 640
UB_W = DIFF_IN + SB_IN

T5_FAR = 91
SB_LOG_ZERO = -88.0

_NT = (((1,), (1,)), ((), ()))


def _cparams(sem):
    return pltpu.CompilerParams(dimension_semantics=sem, vmem_limit_bytes=VMEM_LIMIT)


def _bf16(a):
    return a.astype(jnp.bfloat16)


def _dot(a, b):
    return jnp.dot(a, b, preferred_element_type=jnp.float32)


def _in_proj_kernel(x_ref, w_ref, ua_ref, ub_ref):
    x = _bf16(x_ref[...])
    step = 512
    for c in range(0, UA_W, step):
        e = min(c + step, UA_W)
        ua_ref[:, c:e] = _dot(x, w_ref[:, c:e])
    for c in range(0, UB_W, step):
        e = min(c + step, UB_W)
        ub_ref[:, c:e] = _bf16(_dot(x, w_ref[:, UA_W + c:UA_W + e]))


def _in_proj(h2, w_all, tm=512):
    T = h2.shape[0]
    return pl.pallas_call(
        _in_proj_kernel,
        grid=(T // tm,),
        in_specs=[pl.BlockSpec((tm, D_MODEL), lambda i: (i, 0)),
                  pl.BlockSpec((D_MODEL, UA_W + UB_W), lambda i: (0, 0))],
        out_specs=[pl.BlockSpec((tm, UA_W), lambda i: (i, 0)),
                   pl.BlockSpec((tm, UB_W), lambda i: (i, 0))],
        out_shape=[jax.ShapeDtypeStruct((T, UA_W), jnp.float32),
                   jax.ShapeDtypeStruct((T, UB_W), jnp.bfloat16)],
        compiler_params=_cparams(("parallel",)),
        name="in_proj",
    )(h2, w_all)


def _mla_prep_kernel(ua_ref, pos_ref, inv_ref, gq_ref, gkv_ref, wq_ref, wkv_ref,
                     q_ref, k_ref, v_ref):
    ckv = ua_ref[:, UA_CKV:UA_CKV + 128]
    kr = ua_ref[:, UA_KR:UA_KR + 128]
    krs = ua_ref[:, UA_KRS:UA_KRS + 128]
    cq = ua_ref[:, UA_CQ:UA_CQ + 256]
    cqn = cq * lax.rsqrt(jnp.sum(cq * cq, -1, keepdims=True) * (1.0 / MLA_Q_RANK) + EPS) * gq_ref[...]
    ckvn = ckv * lax.rsqrt(jnp.sum(ckv * ckv, -1, keepdims=True) * (1.0 / MLA_KV_RANK) + EPS) * gkv_ref[...]
    ang = pos_ref[...].astype(jnp.float32) * inv_ref[...]
    lane = lax.broadcasted_iota(jnp.int32, ang.shape, 1)
    rope_lane = (lane >= MLA_NOPE) & (lane < MLA_NOPE + MLA_ROPE)
    cosf = jnp.where(lane < MLA_NOPE, 1.0, jnp.where(rope_lane, jnp.cos(ang), 0.0))
    sinf = jnp.where(rope_lane, jnp.sin(ang), 0.0)
    scale = (MLA_NOPE + MLA_ROPE) ** -0.5
    cqb = _bf16(cqn)
    ckvb = _bf16(ckvn)
    k_rope = kr * cosf + krs * sinf
    for h in range(MLA_HEADS):
        a = _dot(cqb, wq_ref[:, h * 128:(h + 1) * 128])
        b = _dot(cqb, wq_ref[:, 512 + h * 128:512 + (h + 1) * 128])
        q_ref[:, h * 128:(h + 1) * 128] = _bf16((a * cosf + b * sinf) * scale)
        kn = _dot(ckvb, wkv_ref[:, h * 128:(h + 1) * 128])
        k_ref[:, h * 128:(h + 1) * 128] = _bf16(kn + k_rope)
    v_ref[...] = _bf16(_dot(ckvb, wkv_ref[:, 512:768]))


def _mla_prep(ua, pos_col, inv_lane, gq, gkv, wq, wkv, tm=512):
    T = ua.shape[0]
    full = lambda a: pl.BlockSpec(a.shape, lambda i: (0, 0))
    return pl.pallas_call(
        _mla_prep_kernel,
        grid=(T // tm,),
        in_specs=[pl.BlockSpec((tm, UA_W), lambda i: (i, 0)),
                  pl.BlockSpec((tm, 1), lambda i: (i, 0)),
                  full(inv_lane), full(gq), full(gkv), full(wq), full(wkv)],
        out_specs=[pl.BlockSpec((tm, 512), lambda i: (i, 0)),
                   pl.BlockSpec((tm, 512), lambda i: (i, 0)),
                   pl.BlockSpec((tm, 256), lambda i: (i, 0))],
        out_shape=[jax.ShapeDtypeStruct((T, 512), jnp.bfloat16),
                   jax.ShapeDtypeStruct((T, 512), jnp.bfloat16),
                   jax.ShapeDtypeStruct((T, 256), jnp.bfloat16)],
        compiler_params=_cparams(("parallel",)),
        name="mla_prep",
    )(ua, pos_col, inv_lane, gq, gkv, wq, wkv)


def _chunk_mask(tq, tk):
    qc = lax.broadcasted_iota(jnp.int32, (tq, tk), 0) // CHUNK
    kc = lax.broadcasted_iota(jnp.int32, (tq, tk), 1) // CHUNK
    return kc <= qc


def _softmax_update(s, v, m_ref, l_ref, acc_ref, idx):
    m_old = m_ref[idx]
    m_new = jnp.maximum(m_old, jnp.max(s, axis=1, keepdims=True))
    alpha = jnp.exp(m_old - m_new)
    p = jnp.exp(s - m_new)
    l_ref[idx] = alpha * l_ref[idx] + jnp.sum(p, axis=1, keepdims=True)
    acc_ref[idx] = alpha * acc_ref[idx] + _dot(_bf16(p), v)
    m_ref[idx] = m_new


def _mla_attn_kernel(q_ref, k_ref, v_ref, o_ref, m_ref, l_ref, acc_ref, *, t):
    qi = pl.program_id(2)
    m_ref[...] = jnp.full(m_ref.shape, NEG_INF, jnp.float32)
    l_ref[...] = jnp.zeros(l_ref.shape, jnp.float32)
    acc_ref[...] = jnp.zeros(acc_ref.shape, jnp.float32)

    def tile(j, masked):
        ks = pl.multiple_of(j * t, t)
        v = v_ref[pl.ds(ks, t), :]
        for hh in range(2):
            q = q_ref[:, hh * 128:(hh + 1) * 128]
            k = k_ref[pl.ds(ks, t), hh * 128:(hh + 1) * 128]
            s = lax.dot_general(q, k, _NT, preferred_element_type=jnp.float32)
            if masked:
                s = jnp.where(_chunk_mask(t, t), s, NEG_INF)
            _softmax_update(s, v, m_ref, l_ref, acc_ref, hh)

    def body(j, carry):
        tile(j, False)
        return carry

    lax.fori_loop(0, qi, body, 0)
    tile(qi, True)
    lane = lax.broadcasted_iota(jnp.int32, (t, 128), 1)
    o_ref[...] = _bf16(jnp.where(lane < MLA_V, acc_ref[0] / l_ref[0], acc_ref[1] / l_ref[1]))


def _mla_attn(q, k, v, B, S, t=256):
    nq = S // t
    return pl.pallas_call(
        functools.partial(_mla_attn_kernel, t=t),
        grid=(B, 2, nq),
        in_specs=[pl.BlockSpec((t, 256), lambda b, hp, i: (b * nq + i, hp)),
                  pl.BlockSpec((S, 256), lambda b, hp, i: (b, hp)),
                  pl.BlockSpec((S, 128), lambda b, hp, i: (b, hp))],
        out_specs=pl.BlockSpec((t, 128), lambda b, hp, i: (b * nq + i, hp)),
        out_shape=jax.ShapeDtypeStruct((B * S, MLA_HEADS * MLA_V), jnp.bfloat16),
        scratch_shapes=[pltpu.VMEM((2, t, 1), jnp.float32),
                        pltpu.VMEM((2, t, 1), jnp.float32),
                        pltpu.VMEM((2, t, 128), jnp.float32)],
        compiler_params=_cparams(("parallel", "parallel", "arbitrary")),
        name="mla_attn",
    )(q, k, v)


def _t5_bias(rel, rb_ref, h):
    nb = REL_BUCKETS // 2
    max_exact = nb // 2
    n = jnp.abs(rel)
    nf = jnp.maximum(n, 1).astype(jnp.float32)
    large = max_exact + (jnp.log(nf / max_exact) / math.log(REL_MAX_DIST / max_exact)
                         * (nb - max_exact)).astype(jnp.int32)
    large = jnp.minimum(large, nb - 1)
    low = jnp.where(n < max_exact, n, large)
    neg = jnp.zeros(rel.shape, jnp.float32)
    pos = jnp.zeros(rel.shape, jnp.float32)
    for j in range(nb):
        eq = low == j
        neg = jnp.where(eq, rb_ref[h * REL_BUCKETS + j], neg)
        pos = jnp.where(eq, rb_ref[h * REL_BUCKETS + nb + j], pos)
    return jnp.where(rel > 0, pos, neg)


def _diff_attn_kernel(qmin_ref, kmax_ref,
                      rb_ref, q_ref, k_ref, v_ref, posq_ref, posk_ref,
                      lq1_ref, lk1_ref, lq2_ref, lk2_ref, sub_ref,
                      o_ref, m_ref, l_ref, acc_ref, *, t, nq, lam_init):
    b = pl.program_id(0)
    h = pl.program_id(1)
    qi = pl.program_id(2)
    m_ref[...] = jnp.full(m_ref.shape, NEG_INF, jnp.float32)
    l_ref[...] = jnp.zeros(l_ref.shape, jnp.float32)
    acc_ref[...] = jnp.zeros(acc_ref.shape, jnp.float32)

    scale = DIFF_QK ** -0.5
    q = q_ref[...] * scale
    lane = lax.broadcasted_iota(jnp.int32, q.shape, 1)
    zero = jnp.zeros_like(q)
    q1 = jnp.where(lane < DIFF_QK, q, zero)
    q2 = jnp.where(lane < DIFF_QK, zero, q)
    qmin = qmin_ref[b * nq + qi]

    def tile(j, masked, bias_fn):
        ks = pl.multiple_of(j * t, t)
        k = k_ref[pl.ds(ks, t), :]
        v = v_ref[pl.ds(ks, t), :]
        bias = bias_fn(j)
        for idx, qq in enumerate((q1, q2)):
            s = lax.dot_general(qq, k, _NT, preferred_element_type=jnp.float32) + bias
            if masked:
                s = jnp.where(_chunk_mask(t, t), s, NEG_INF)
            _softmax_update(s, v, m_ref, l_ref, acc_ref, idx)

    def general_bias(j):
        rel = posk_ref[j] - posq_ref[...]
        return _t5_bias(rel, rb_ref, h)

    def far_bias(j):
        return rb_ref[h * REL_BUCKETS + REL_BUCKETS // 2 - 1]

    def body(j, carry):
        far = kmax_ref[b * nq + j] - qmin <= -T5_FAR

        @pl.when(far)
        def _():
            tile(j, False, far_bias)

        @pl.when(jnp.logical_not(far))
        def _():
            tile(j, False, general_bias)

        return carry

    lax.fori_loop(0, qi, body, 0)
    tile(qi, True, general_bias)

    f32 = jnp.float32
    lam = (jnp.exp(jnp.sum(lq1_ref[...].astype(f32) * lk1_ref[...].astype(f32), keepdims=True))
           - jnp.exp(jnp.sum(lq2_ref[...].astype(f32) * lk2_ref[...].astype(f32), keepdims=True))
           + lam_init)
    o = acc_ref[0] / l_ref[0] - lam * (acc_ref[1] / l_ref[1])
    o = o * lax.rsqrt(jnp.mean(o * o, -1, keepdims=True) + EPS) * sub_ref[...] * (1.0 - lam_init)
    o_ref[...] = _bf16(o)


def _diff_attn(ub, pos_col, pos_tiles, qmin, kmax, rb_flat, lq1, lk1, lq2, lk2, subln,
               B, S, lam_init, t=256):
    nq = S // t
    H = DIFF_HEADS
    small = lambda a: pl.BlockSpec(a.shape, lambda b, h, i, *_: (0, 0))
    grid_spec = pltpu.PrefetchScalarGridSpec(
        num_scalar_prefetch=2,
        grid=(B, H, nq),
        in_specs=[pl.BlockSpec(memory_space=pltpu.SMEM),
                  pl.BlockSpec((t, 128), lambda b, h, i, *_: (b * nq + i, h)),
                  pl.BlockSpec((S, 128), lambda b, h, i, *_: (b, H + h)),
                  pl.BlockSpec((S, 128), lambda b, h, i, *_: (b, 2 * H + h)),
                  pl.BlockSpec((t, 1), lambda b, h, i, *_: (b * nq + i, 0)),
                  pl.BlockSpec((nq, 1, t), lambda b, h, i, *_: (b, 0, 0)),
                  small(lq1), small(lk1), small(lq2), small(lk2), small(subln)],
        out_specs=pl.BlockSpec((t, 128), lambda b, h, i, *_: (b * nq + i, h)),
        scratch_shapes=[pltpu.VMEM((2, t, 1), jnp.float32),
                        pltpu.VMEM((2, t, 1), jnp.float32),
                        pltpu.VMEM((2, t, 128), jnp.float32)],
    )
    return pl.pallas_call(
        functools.partial(_diff_attn_kernel, t=t, nq=nq, lam_init=lam_init),
        grid_spec=grid_spec,
        out_shape=jax.ShapeDtypeStruct((B * S, H * DIFF_V), jnp.bfloat16),
        compiler_params=_cparams(("parallel", "parallel", "arbitrary")),
        name="diff_attn",
    )(qmin, kmax, rb_flat, ub, ub, ub, pos_col, pos_tiles, lq1, lk1, lq2, lk2, subln)


def _sb_attn_kernel(q_ref, k_ref, v_ref, o_ref, c_ref, acc_ref, *, t):
    qi = pl.program_id(2)
    row = lax.broadcasted_iota(jnp.int32, (t, t), 0)
    col = lax.broadcasted_iota(jnp.int32, (t, t), 1)
    tri = jnp.where(row > col, 1.0, 0.0).astype(jnp.bfloat16)
    strict = col < row
    q = q_ref[...] * (SB_D ** -0.5)
    lane = lax.broadcasted_iota(jnp.int32, q.shape, 1)
    out = []
    for hh in range(2):
        qm = jnp.where((lane < SB_D) == (hh == 0), q, jnp.zeros_like(q))
        c_ref[...] = jnp.zeros(c_ref.shape, jnp.float32)
        acc_ref[...] = jnp.zeros(acc_ref.shape, jnp.float32)

        def tile(j, diag, qm=qm):
            ks = pl.multiple_of(j * t, t)
            k = k_ref[pl.ds(ks, t), :]
            v = v_ref[pl.ds(ks, t), :]
            z = lax.dot_general(qm, k, _NT, preferred_element_type=jnp.float32)
            lf = -(jnp.maximum(z, 0.0) + jnp.log(1.0 + jnp.exp(-jnp.abs(z))))
            if diag:
                lf = jnp.where(strict, lf, 0.0)
            hi = _bf16(lf)
            lo = _bf16(lf - hi.astype(jnp.float32))
            later = _dot(hi, tri) + _dot(lo, tri) + c_ref[...]
            w = jnp.exp(lf + z + later)
            if diag:
                w = jnp.where(strict, w, 0.0)
            acc_ref[...] += _dot(_bf16(w), v)
            c_ref[...] += jnp.sum(lf, axis=1, keepdims=True)

        tile(qi, True)

        def cond(carry):
            j, cmax = carry
            return jnp.logical_and(j >= 0, cmax > SB_LOG_ZERO)

        def body(carry):
            j, _ = carry
            tile(j, False)
            return j - 1, jnp.max(c_ref[...])

        lax.while_loop(cond, body, (qi - 1, jnp.max(c_ref[...])))
        out.append(acc_ref[...])
    o_ref[...] = _bf16(jnp.where(lane < SB_D, out[0], out[1]))


def _sb_attn(ub, B, S, t=256):
    nq = S // t
    c0 = DIFF_IN // 128
    return pl.pallas_call(
        functools.partial(_sb_attn_kernel, t=t),
        grid=(B, 2, nq),
        in_specs=[pl.BlockSpec((t, 128), lambda b, hp, i: (b * nq + i, c0 + hp)),
                  pl.BlockSpec((S, 128), lambda b, hp, i: (b, c0 + 2 + hp)),
                  pl.BlockSpec((S, 128), lambda b, hp, i: (b, c0 + 4 + hp))],
        out_specs=pl.BlockSpec((t, 128), lambda b, hp, i: (b * nq + i, hp)),
        out_shape=jax.ShapeDtypeStruct((B * S, SB_HEADS * SB_D), jnp.bfloat16),
        scratch_shapes=[pltpu.VMEM((t, 1), jnp.float32),
                        pltpu.VMEM((t, 128), jnp.float32)],
        compiler_params=_cparams(("parallel", "parallel", "arbitrary")),
        name="sb_attn",
    )(ub, ub, ub)


def _layer_norm(x, g, b):
    mu = jnp.mean(x, -1, keepdims=True)
    xc = x - mu
    var = jnp.mean(xc * xc, -1, keepdims=True)
    return xc * lax.rsqrt(var + EPS) * g + b


def _out_proj_kernel(h_ref, ym_ref, yd_ref, ys_ref, wm_ref, wd_ref, ws_ref, g_ref, b_ref, o_ref, *, alpha):
    mix = _dot(ym_ref[...], wm_ref[...]) + _dot(yd_ref[...], wd_ref[...]) + _dot(ys_ref[...], ws_ref[...])
    o_ref[...] = _layer_norm(alpha * h_ref[...] + mix, g_ref[...], b_ref[...])


def _out_proj(h2, ym, yd, ys, wm, wd, ws, g, b, alpha, tm=512):
    T = h2.shape[0]
    row = lambda a: pl.BlockSpec((tm, a.shape[1]), lambda i: (i, 0))
    full = lambda a: pl.BlockSpec(a.shape, lambda i: (0, 0))
    return pl.pallas_call(
        functools.partial(_out_proj_kernel, alpha=alpha),
        grid=(T // tm,),
        in_specs=[row(h2), row(ym), row(yd), row(ys), full(wm), full(wd), full(ws), full(g), full(b)],
        out_specs=pl.BlockSpec((tm, D_MODEL), lambda i: (i, 0)),
        out_shape=jax.ShapeDtypeStruct((T, D_MODEL), jnp.float32),
        compiler_params=_cparams(("parallel",)),
        name="out_proj_ln",
    )(h2, ym, yd, ys, wm, wd, ws, g, b)


def _first_max(vals):
    m = vals[0]
    for v in vals[1:]:
        m = jnp.maximum(m, v)
    taken = jnp.zeros(m.shape, jnp.bool_)
    hot = []
    for v in vals:
        is_first = jnp.logical_and(v == m, jnp.logical_not(taken))
        hot.append(is_first)
        taken = jnp.logical_or(taken, is_first)
    return m, hot


def _router_kernel(h_ref, rwt_ref, rb_ref, gate_ref):
    logits = lax.dot_general(rwt_ref[...], h_ref[...], _NT, preferred_element_type=jnp.float32,
                             precision=lax.Precision.HIGHEST)
    scores = jax.nn.sigmoid(logits)
    sel = scores + rb_ref[...]
    ninf = -jnp.inf
    group_score, first, second = [], [], []
    for g in range(N_GROUPS):
        vals = [sel[g * EXPERTS_PER_GROUP + k:g * EXPERTS_PER_GROUP + k + 1, :] for k in range(EXPERTS_PER_GROUP)]
        m1, hot1 = _first_max(vals)
        m2, hot2 = _first_max([jnp.where(hh, ninf, v) for hh, v in zip(hot1, vals)])
        group_score.append(m1 + m2)
        first.append(hot1)
        second.append(hot2)
    _, best = _first_max(group_score)
    w1 = jnp.zeros_like(group_score[0])
    w2 = jnp.zeros_like(group_score[0])
    for g in range(N_GROUPS):
        for k in range(EXPERTS_PER_GROUP):
            e = g * EXPERTS_PER_GROUP + k
            sc = scores[e:e + 1, :]
            w1 = jnp.where(jnp.logical_and(best[g], first[g][k]), sc, w1)
            w2 = jnp.where(jnp.logical_and(best[g], second[g][k]), sc, w2)
    tot = w1 + w2
    for g in range(N_GROUPS):
        for k in range(EXPERTS_PER_GROUP):
            e = g * EXPERTS_PER_GROUP + k
            ge = (jnp.where(jnp.logical_and(best[g], first[g][k]), w1 / tot, 0.0)
                  + jnp.where(jnp.logical_and(best[g], second[g][k]), w2 / tot, 0.0))
            gate_ref[e:e + 1, :] = ge


def _router(h2, rwt, rb_col, tm=512):
    T = h2.shape[0]
    return pl.pallas_call(
        _router_kernel,
        grid=(T // tm,),
        in_specs=[pl.BlockSpec((tm, D_MODEL), lambda i: (i, 0)),
                  pl.BlockSpec(rwt.shape, lambda i: (0, 0)),
                  pl.BlockSpec(rb_col.shape, lambda i: (0, 0))],
        out_specs=pl.BlockSpec((N_EXPERTS, tm), lambda i: (0, i)),
        out_shape=jax.ShapeDtypeStruct((N_EXPERTS, T), jnp.float32),
        compiler_params=_cparams(("parallel",)),
        name="router",
    )(h2, rwt, rb_col)


def _moe_kernel(h_ref, gate_ref, p_ref, wg_ref, wu_ref, wd_ref, pg_ref, pp_ref, g_ref, b_ref,
                o_ref, acc_ref, xb_ref, *, alpha):
    e = pl.program_id(1)

    @pl.when(e == 0)
    def _():
        h = h_ref[...]
        xb = _bf16(h)
        xb_ref[...] = xb
        ple = jax.nn.sigmoid(_dot(xb, pg_ref[...])) * _dot(_bf16(p_ref[...]), pp_ref[...])
        acc_ref[...] = alpha * h + ple

    xb = xb_ref[...]
    hid = jax.nn.silu(_dot(xb, wg_ref[0])) * _dot(xb, wu_ref[0])
    lane = lax.broadcasted_iota(jnp.int32, gate_ref.shape, 1)
    ge = jnp.sum(jnp.where(lane == e, gate_ref[...], 0.0), axis=1, keepdims=True)
    acc_ref[...] += ge * _dot(_bf16(hid), wd_ref[0])

    @pl.when(e == N_EXPERTS - 1)
    def _():
        o_ref[...] = _layer_norm(acc_ref[...], g_ref[...], b_ref[...])


def _moe(h2, gate, p2, wg, wu, wd, pg, pp, g, b, alpha, tm=1024):
    T = h2.shape[0]
    full = lambda a: pl.BlockSpec(a.shape, lambda i, e: (0, 0))
    return pl.pallas_call(
        functools.partial(_moe_kernel, alpha=alpha),
        grid=(T // tm, N_EXPERTS),
        in_specs=[pl.BlockSpec((tm, D_MODEL), lambda i, e: (i, 0)),
                  pl.BlockSpec((tm, N_EXPERTS), lambda i, e: (i, 0)),
                  pl.BlockSpec((tm, PLE_DIM), lambda i, e: (i, 0)),
                  pl.BlockSpec((1, D_MODEL, D_EXPERT), lambda i, e: (e, 0, 0)),
                  pl.BlockSpec((1, D_MODEL, D_EXPERT), lambda i, e: (e, 0, 0)),
                  pl.BlockSpec((1, D_EXPERT, D_MODEL), lambda i, e: (e, 0, 0)),
                  full(pg), full(pp), full(g), full(b)],
        out_specs=pl.BlockSpec((tm, D_MODEL), lambda i, e: (i, 0)),
        out_shape=jax.ShapeDtypeStruct((T, D_MODEL), jnp.float32),
        scratch_shapes=[pltpu.VMEM((tm, D_MODEL), jnp.float32),
                        pltpu.VMEM((tm, D_MODEL), jnp.bfloat16)],
        compiler_params=_cparams(("parallel", "arbitrary")),
        name="moe_ple_ln",
    )(h2, gate, p2, wg, wu, wd, pg, pp, g, b)


def _rotate_half_cols(w):
    half = w.shape[1] // 2
    return jnp.concatenate([-w[:, half:], w[:, :half]], axis=1)


def _in_proj_weights(w_in):
    z = lambda n: jnp.zeros((D_MODEL, n), w_in.dtype)
    cq = w_in[:, :MLA_Q_RANK]
    ckv = w_in[:, MLA_Q_RANK:MLA_Q_RANK + MLA_KV_RANK]
    kr = w_in[:, MLA_Q_RANK + MLA_KV_RANK:MLA_IN]
    cols = [ckv,
            z(MLA_NOPE), kr, z(128 - MLA_NOPE - MLA_ROPE),
            z(MLA_NOPE), _rotate_half_cols(kr), z(128 - MLA_NOPE - MLA_ROPE),
            cq, z(256 - MLA_Q_RANK),
            w_in[:, MLA_IN:]]
    return _bf16(jnp.concatenate(cols, axis=1))


def _mla_up_weights(w_uq, w_ukv):
    dq = MLA_NOPE + MLA_ROPE
    zq = lambda n: jnp.zeros((MLA_Q_RANK, n), w_uq.dtype)
    plain, rot = [], []
    for h in range(MLA_HEADS):
        wh = w_uq[:, h * dq:(h + 1) * dq]
        plain += [wh, zq(128 - dq)]
        rot += [zq(MLA_NOPE), _rotate_half_cols(wh[:, MLA_NOPE:]), zq(128 - dq)]
    wq = jnp.concatenate(plain + rot, axis=1)
    wq = jnp.concatenate([wq, jnp.zeros((256 - MLA_Q_RANK, wq.shape[1]), wq.dtype)], axis=0)
    dkv = MLA_NOPE + MLA_V
    zk = lambda n: jnp.zeros((MLA_KV_RANK, n), w_ukv.dtype)
    kcols, vcols = [], []
    for h in range(MLA_HEADS):
        wh = w_ukv[:, h * dkv:(h + 1) * dkv]
        kcols += [wh[:, :MLA_NOPE], zk(128 - MLA_NOPE)]
        vcols += [wh[:, MLA_NOPE:]]
    wkv = jnp.concatenate(kcols + vcols, axis=1)
    return _bf16(wq), _bf16(wkv)


def kernel(x, p, positions, w_in, mla_q_norm, mla_w_uq, mla_kv_norm, mla_w_ukv, diff_lambda_q1, diff_lambda_k1, diff_lambda_q2, diff_lambda_k2, diff_subln, rel_bias, w_o, ln1_g, ln1_b, router_w, router_b, w_gate, w_up, w_down, ple_proj, ple_gate, ln2_g, ln2_b):
    B, S, _ = x.shape
    depth = w_in.shape[0]
    T = B * S
    alpha = (2 * depth) ** 0.25
    t_attn = 256
    nq = S // t_attn

    pos_col = positions.reshape(T, 1)
    pos_tiles = positions.reshape(B * nq, 1, t_attn)
    tile_pos = positions.reshape(B * nq, t_attn)
    qmin = jnp.min(tile_pos, axis=1)
    kmax = jnp.max(tile_pos, axis=1)
    rb_flat = rel_bias.T.reshape(-1).astype(jnp.float32)

    half = MLA_ROPE // 2
    inv = ROPE_THETA ** (-jnp.arange(half, dtype=jnp.float32) / half)
    inv_lane = jnp.concatenate([jnp.zeros((MLA_NOPE,), jnp.float32), inv, inv,
                                jnp.zeros((128 - MLA_NOPE - MLA_ROPE,), jnp.float32)]).reshape(1, 128)

    rwt = router_w.T.astype(jnp.float32)
    rb_col = router_b.reshape(N_EXPERTS, 1).astype(jnp.float32)
    row = lambda a: a.reshape(1, -1)

    h = x.reshape(T, D_MODEL)
    for i in range(depth):
        lam_init = 0.8 - 0.6 * math.exp(-0.3 * i)
        ua, ub = _in_proj(h, _in_proj_weights(w_in[i]))
        wq, wkv = _mla_up_weights(mla_w_uq[i], mla_w_ukv[i])
        gq = jnp.concatenate([mla_q_norm[i], jnp.zeros((256 - MLA_Q_RANK,), mla_q_norm.dtype)]).reshape(1, 256)
        q_m, k_m, v_m = _mla_prep(ua, pos_col, inv_lane, gq, row(mla_kv_norm[i]), wq, wkv)
        y_mla = _mla_attn(q_m, k_m, v_m, B, S, t=t_attn)
        y_diff = _diff_attn(ub, pos_col, pos_tiles, qmin, kmax, rb_flat,
                            row(diff_lambda_q1[i]), row(diff_lambda_k1[i]),
                            row(diff_lambda_q2[i]), row(diff_lambda_k2[i]), row(diff_subln[i]),
                            B, S, lam_init, t=t_attn)
        y_sb = _sb_attn(ub, B, S, t=t_attn)
        wo = _bf16(w_o[i])
        n_m, n_d = MLA_HEADS * MLA_V, DIFF_HEADS * DIFF_V
        h = _out_proj(h, y_mla, y_diff, y_sb, wo[:n_m], wo[n_m:n_m + n_d], wo[n_m + n_d:],
                      row(ln1_g[i]), row(ln1_b[i]), alpha)
        gate = _router(h, rwt, rb_col).T
        h = _moe(h, gate, p[i].reshape(T, PLE_DIM), _bf16(w_gate[i]), _bf16(w_up[i]), _bf16(w_down[i]),
                 _bf16(ple_gate[i]), _bf16(ple_proj[i]), row(ln2_g[i]), row(ln2_b[i]), alpha)
    return h.reshape(B, S, D_MODEL)
```

```python
import functools
import math

import jax
import jax.numpy as jnp
from jax import lax
from jax.experimental import pallas as pl
from jax.experimental.pallas import tpu as pltpu

D_MODEL = 1024
CHUNK = 64
PLE_DIM = 256
MLA_HEADS = 4
MLA_NOPE = 64
MLA_ROPE = 32
MLA_V = 64
MLA_Q_RANK = 192
MLA_KV_RANK = 128
ROPE_THETA = 10000.0
DIFF_HEADS = 4
DIFF_QK = 64
DIFF_V = 2 * DIFF_QK
SB_HEADS = 4
SB_D = 64
REL_BUCKETS = 32
REL_MAX_DIST = 128
N_EXPERTS = 16
N_GROUPS = 4
EXPERTS_PER_GROUP = N_EXPERTS // N_GROUPS
D_EXPERT = 512
MLA_IN = MLA_Q_RANK + MLA_KV_RANK + MLA_ROPE
DIFF_IN = 2 * DIFF_HEADS * 2 * DIFF_QK + DIFF_HEADS * DIFF_V
SB_IN = 3 * SB_HEADS * SB_D
EPS = 1e-5
NEG_INF = -1e30

LANES = 128
VMEM_LIMIT = 56 * 1024 * 1024

UA_CKV = 0
UA_KR = 128
UA_KRS = 256
UA_CQ = 384
UA_W = 640
UB_W = DIFF_IN + SB_IN

T5_FAR = 91
SB_LOG_ZERO = -88.0

_NT = (((1,), (1,)), ((), ()))
LOG2E = math.log2(math.e)


def _cparams(sem):
    return pltpu.CompilerParams(dimension_semantics=sem, vmem_limit_bytes=VMEM_LIMIT)


def _bf16(a):
    return a.astype(jnp.bfloat16)


def _dot(a, b):
    return jnp.dot(a, b, preferred_element_type=jnp.float32)


def _in_proj_kernel(x_ref, w_ref, ua_ref, ub_ref):
    x = _bf16(x_ref[...])
    step = 512
    for c in range(0, UA_W, step):
        e = min(c + step, UA_W)
        ua_ref[:, c:e] = _dot(x, w_ref[:, c:e])
    for c in range(0, UB_W, step):
        e = min(c + step, UB_W)
        ub_ref[:, c:e] = _bf16(_dot(x, w_ref[:, UA_W + c:UA_W + e]))


def _in_proj(h2, w_all, tm=512):
    T = h2.shape[0]
    return pl.pallas_call(
        _in_proj_kernel,
        grid=(T // tm,),
        in_specs=[pl.BlockSpec((tm, D_MODEL), lambda i: (i, 0)),
                  pl.BlockSpec((D_MODEL, UA_W + UB_W), lambda i: (0, 0))],
        out_specs=[pl.BlockSpec((tm, UA_W), lambda i: (i, 0)),
                   pl.BlockSpec((tm, UB_W), lambda i: (i, 0))],
        out_shape=[jax.ShapeDtypeStruct((T, UA_W), jnp.float32),
                   jax.ShapeDtypeStruct((T, UB_W), jnp.bfloat16)],
        compiler_params=_cparams(("parallel",)),
        name="in_proj",
    )(h2, w_all)


def _mla_prep_kernel(ua_ref, pos_ref, inv_ref, gq_ref, gkv_ref, wq_ref, wkv_ref,
                     q_ref, k_ref, v_ref):
    ckv = ua_ref[:, UA_CKV:UA_CKV + 128]
    kr = ua_ref[:, UA_KR:UA_KR + 128]
    krs = ua_ref[:, UA_KRS:UA_KRS + 128]
    cq = ua_ref[:, UA_CQ:UA_CQ + 256]
    cqn = cq * lax.rsqrt(jnp.sum(cq * cq, -1, keepdims=True) * (1.0 / MLA_Q_RANK) + EPS) * gq_ref[...]
    ckvn = ckv * lax.rsqrt(jnp.sum(ckv * ckv, -1, keepdims=True) * (1.0 / MLA_KV_RANK) + EPS) * gkv_ref[...]
    ang = pos_ref[...].astype(jnp.float32) * inv_ref[...]
    lane = lax.broadcasted_iota(jnp.int32, ang.shape, 1)
    rope_lane = (lane >= MLA_NOPE) & (lane < MLA_NOPE + MLA_ROPE)
    cosf = jnp.where(lane < MLA_NOPE, 1.0, jnp.where(rope_lane, jnp.cos(ang), 0.0))
    sinf = jnp.where(rope_lane, jnp.sin(ang), 0.0)
    scale = (MLA_NOPE + MLA_ROPE) ** -0.5 * LOG2E
    cqb = _bf16(cqn)
    ckvb = _bf16(ckvn)
    k_rope = kr * cosf + krs * sinf
    for h in range(MLA_HEADS):
        a = _dot(cqb, wq_ref[:, h * 128:(h + 1) * 128])
        b = _dot(cqb, wq_ref[:, 512 + h * 128:512 + (h + 1) * 128])
        q_ref[:, h * 128:(h + 1) * 128] = _bf16((a * cosf + b * sinf) * scale)
        kn = _dot(ckvb, wkv_ref[:, h * 128:(h + 1) * 128])
        k_ref[:, h * 128:(h + 1) * 128] = _bf16(kn + k_rope)
    v_ref[...] = _bf16(_dot(ckvb, wkv_ref[:, 512:768]))


def _mla_prep(ua, pos_col, inv_lane, gq, gkv, wq, wkv, tm=512):
    T = ua.shape[0]
    full = lambda a: pl.BlockSpec(a.shape, lambda i: (0, 0))
    return pl.pallas_call(
        _mla_prep_kernel,
        grid=(T // tm,),
        in_specs=[pl.BlockSpec((tm, UA_W), lambda i: (i, 0)),
                  pl.BlockSpec((tm, 1), lambda i: (i, 0)),
                  full(inv_lane), full(gq), full(gkv), full(wq), full(wkv)],
        out_specs=[pl.BlockSpec((tm, 512), lambda i: (i, 0)),
                   pl.BlockSpec((tm, 512), lambda i: (i, 0)),
                   pl.BlockSpec((tm, 256), lambda i: (i, 0))],
        out_shape=[jax.ShapeDtypeStruct((T, 512), jnp.bfloat16),
                   jax.ShapeDtypeStruct((T, 512), jnp.bfloat16),
                   jax.ShapeDtypeStruct((T, 256), jnp.bfloat16)],
        compiler_params=_cparams(("parallel",)),
        name="mla_prep",
    )(ua, pos_col, inv_lane, gq, gkv, wq, wkv)


def _chunk_mask(tq, tk):
    qc = lax.broadcasted_iota(jnp.int32, (tq, tk), 0) // CHUNK
    kc = lax.broadcasted_iota(jnp.int32, (tq, tk), 1) // CHUNK
    return kc <= qc


def _softmax_update(s, v, m_ref, l_ref, acc_ref, idx, shift=None):
    tk = s.shape[1]
    m_old = m_ref[idx]
    row_max = jnp.max(s, axis=1, keepdims=True)
    if shift is not None:
        row_max = row_max + shift
    m_new = jnp.maximum(m_old, row_max)
    alpha = jnp.exp2(m_old - m_new)
    m_sub = m_new if shift is None else m_new - shift
    p = jnp.exp2(s - pltpu.repeat(m_sub, tk // LANES, axis=1))
    psum = p[:, :LANES]
    for c in range(LANES, tk, LANES):
        psum = psum + p[:, c:c + LANES]
    l_ref[idx] = alpha * l_ref[idx] + psum
    acc_ref[idx] = alpha * acc_ref[idx] + _dot(_bf16(p), v)
    m_ref[idx] = m_new


def _softmax_init(m_ref, l_ref, acc_ref):
    m_ref[...] = jnp.full(m_ref.shape, NEG_INF, jnp.float32)
    l_ref[...] = jnp.zeros(l_ref.shape, jnp.float32)
    acc_ref[...] = jnp.zeros(acc_ref.shape, jnp.float32)


def _softmax_result(l_ref, acc_ref, idx):
    return acc_ref[idx] / jnp.sum(l_ref[idx], axis=1, keepdims=True)


def _mla_attn_kernel(q_ref, k_ref, v_ref, o_ref, m_ref, l_ref, acc_ref, *, t):
    qi = pl.program_id(2)
    _softmax_init(m_ref, l_ref, acc_ref)

    def tile(j, masked):
        ks = pl.multiple_of(j * t, t)
        v = v_ref[pl.ds(ks, t), :]
        for hh in range(2):
            q = q_ref[:, hh * 128:(hh + 1) * 128]
            k = k_ref[pl.ds(ks, t), hh * 128:(hh + 1) * 128]
            s = lax.dot_general(q, k, _NT, preferred_element_type=jnp.float32)
            if masked:
                s = jnp.where(_chunk_mask(t, t), s, NEG_INF)
            _softmax_update(s, v, m_ref, l_ref, acc_ref, hh)

    def body(j, carry):
        tile(j, False)
        return carry

    lax.fori_loop(0, qi, body, 0)
    tile(qi, True)
    lane = lax.broadcasted_iota(jnp.int32, (t, 128), 1)
    o_ref[...] = _bf16(jnp.where(lane < MLA_V, _softmax_result(l_ref, acc_ref, 0),
                                 _softmax_result(l_ref, acc_ref, 1)))


def _mla_attn(q, k, v, B, S, t=256):
    nq = S // t
    return pl.pallas_call(
        functools.partial(_mla_attn_kernel, t=t),
        grid=(B, 2, nq),
        in_specs=[pl.BlockSpec((t, 256), lambda b, hp, i: (b * nq + i, hp)),
                  pl.BlockSpec((S, 256), lambda b, hp, i: (b, hp)),
                  pl.BlockSpec((S, 128), lambda b, hp, i: (b, hp))],
        out_specs=pl.BlockSpec((t, 128), lambda b, hp, i: (b * nq + i, hp)),
        out_shape=jax.ShapeDtypeStruct((B * S, MLA_HEADS * MLA_V), jnp.bfloat16),
        scratch_shapes=[pltpu.VMEM((2, t, LANES), jnp.float32)] * 3,
        compiler_params=_cparams(("parallel", "parallel", "arbitrary")),
        name="mla_attn",
    )(q, k, v)


def _t5_bias(rel, rb_ref, h):
    nb = REL_BUCKETS // 2
    max_exact = nb // 2
    n = jnp.abs(rel)
    nf = jnp.maximum(n, 1).astype(jnp.float32)
    large = max_exact + (jnp.log(nf / max_exact) / math.log(REL_MAX_DIST / max_exact)
                         * (nb - max_exact)).astype(jnp.int32)
    large = jnp.minimum(large, nb - 1)
    low = jnp.where(n < max_exact, n, large)
    neg = jnp.zeros(rel.shape, jnp.float32)
    pos = jnp.zeros(rel.shape, jnp.float32)
    for j in range(nb):
        eq = low == j
        neg = jnp.where(eq, rb_ref[h * REL_BUCKETS + j], neg)
        pos = jnp.where(eq, rb_ref[h * REL_BUCKETS + nb + j], pos)
    return jnp.where(rel > 0, pos, neg)


def _diff_attn_kernel(qmin_ref, kmax_ref, first_ref, consec_ref,
                      rb_ref, q_ref, k_ref, v_ref, posq_ref, posk_ref,
                      lq1_ref, lk1_ref, lq2_ref, lk2_ref, sub_ref,
                      o_ref, m_ref, l_ref, acc_ref, *, t, nq, lam_init):
    b = pl.program_id(0)
    h = pl.program_id(1)
    qi = pl.program_id(2)
    _softmax_init(m_ref, l_ref, acc_ref)

    q = q_ref[...]
    lane = lax.broadcasted_iota(jnp.int32, q.shape, 1)
    zero = jnp.zeros_like(q)
    qs = jnp.concatenate([jnp.where(lane < DIFF_QK, q, zero), jnp.where(lane < DIFF_QK, zero, q)], axis=0)
    tq = b * nq + qi
    qmin = qmin_ref[tq]

    def toeplitz_bias(j):
        d0 = first_ref[b * nq + j] - first_ref[tq]
        x = lax.broadcasted_iota(jnp.int32, (1, 2 * t), 1)
        g = _t5_bias(d0 + jnp.where(x < t, x, x - 2 * t), rb_ref, h)
        g = pltpu.roll(jnp.broadcast_to(g, (t, 2 * t)), 0, 1, stride=1, stride_axis=0)
        return g[:, :t]

    def general_bias(j):
        rel = posk_ref[j] - posq_ref[...]
        return _t5_bias(rel, rb_ref, h)

    def tile(j, masked, bias_fn):
        ks = pl.multiple_of(j * t, t)
        k = k_ref[pl.ds(ks, t), :]
        v = v_ref[pl.ds(ks, t), :]
        s = lax.dot_general(qs, k, _NT, preferred_element_type=jnp.float32)
        if bias_fn is None:
            shift = rb_ref[h * REL_BUCKETS + REL_BUCKETS // 2 - 1] * LOG2E
        else:
            shift = None
            s = s.reshape(2, t, t) + (bias_fn(j) * LOG2E)[None]
            if masked:
                s = jnp.where(_chunk_mask(t, t)[None], s, NEG_INF)
            s = s.reshape(2 * t, t)
        _softmax_update(s, v, m_ref, l_ref, acc_ref, 0, shift)

    def near_tile(j, masked):
        consecutive = jnp.logical_and(consec_ref[tq] == 1, consec_ref[b * nq + j] == 1)

        @pl.when(consecutive)
        def _():
            tile(j, masked, toeplitz_bias)

        @pl.when(jnp.logical_not(consecutive))
        def _():
            tile(j, masked, general_bias)

    def body(j, carry):
        far = kmax_ref[b * nq + j] - qmin <= -T5_FAR

        @pl.when(far)
        def _():
            tile(j, False, None)

        @pl.when(jnp.logical_not(far))
        def _():
            near_tile(j, False)

        return carry

    lax.fori_loop(0, qi, body, 0)
    near_tile(qi, True)

    f32 = jnp.float32
    lam = (jnp.exp(jnp.sum(lq1_ref[...].astype(f32) * lk1_ref[...].astype(f32), keepdims=True))
           - jnp.exp(jnp.sum(lq2_ref[...].astype(f32) * lk2_ref[...].astype(f32), keepdims=True))
           + lam_init)
    a = _softmax_result(l_ref, acc_ref, 0)
    o = a[:t] - lam * a[t:]
    o = o * lax.rsqrt(jnp.mean(o * o, -1, keepdims=True) + EPS) * sub_ref[...] * (1.0 - lam_init)
    o_ref[...] = _bf16(o)


def _diff_attn(ub, pos_col, pos_tiles, qmin, kmax, first, consec, rb_flat, lq1, lk1, lq2, lk2, subln,
               B, S, lam_init, t=256):
    nq = S // t
    H = DIFF_HEADS
    small = lambda a: pl.BlockSpec(a.shape, lambda b, h, i, *_: (0, 0))
    grid_spec = pltpu.PrefetchScalarGridSpec(
        num_scalar_prefetch=4,
        grid=(B, H, nq),
        in_specs=[pl.BlockSpec(memory_space=pltpu.SMEM),
                  pl.BlockSpec((t, 128), lambda b, h, i, *_: (b * nq + i, h)),
                  pl.BlockSpec((S, 128), lambda b, h, i, *_: (b, H + h)),
                  pl.BlockSpec((S, 128), lambda b, h, i, *_: (b, 2 * H + h)),
                  pl.BlockSpec((t, 1), lambda b, h, i, *_: (b * nq + i, 0)),
                  pl.BlockSpec((nq, 1, t), lambda b, h, i, *_: (b, 0, 0)),
                  small(lq1), small(lk1), small(lq2), small(lk2), small(subln)],
        out_specs=pl.BlockSpec((t, 128), lambda b, h, i, *_: (b * nq + i, h)),
        scratch_shapes=[pltpu.VMEM((1, 2 * t, LANES), jnp.float32)] * 3,
    )
    return pl.pallas_call(
        functools.partial(_diff_attn_kernel, t=t, nq=nq, lam_init=lam_init),
        grid_spec=grid_spec,
        out_shape=jax.ShapeDtypeStruct((B * S, H * DIFF_V), jnp.bfloat16),
        compiler_params=_cparams(("parallel", "parallel", "arbitrary")),
        name="diff_attn",
    )(qmin, kmax, first, consec, rb_flat, ub, ub, ub, pos_col, pos_tiles, lq1, lk1, lq2, lk2, subln)


def _sb_attn_kernel(q_ref, k_ref, v_ref, o_ref, c_ref, acc_ref, *, t):
    qi = pl.program_id(2)
    row = lax.broadcasted_iota(jnp.int32, (t, t), 0)
    col = lax.broadcasted_iota(jnp.int32, (t, t), 1)
    tri = jnp.where(row > col, 1.0, 0.0).astype(jnp.bfloat16)
    strict = col < row
    q = q_ref[...]
    lane = lax.broadcasted_iota(jnp.int32, q.shape, 1)
    out = []
    for hh in range(2):
        qm = jnp.where((lane < SB_D) == (hh == 0), q, jnp.zeros_like(q))
        c_ref[...] = jnp.zeros(c_ref.shape, jnp.float32)
        acc_ref[...] = jnp.zeros(acc_ref.shape, jnp.float32)

        def tile(j, diag, qm=qm):
            ks = pl.multiple_of(j * t, t)
            k = k_ref[pl.ds(ks, t), :]
            v = v_ref[pl.ds(ks, t), :]
            z = lax.dot_general(qm, k, _NT, preferred_element_type=jnp.float32)
            lf = -(jnp.maximum(z, 0.0) + jnp.log(1.0 + jnp.exp(-jnp.abs(z))))
            if diag:
                lf = jnp.where(strict, lf, 0.0)
            hi = _bf16(lf)
            lo = _bf16(lf - hi.astype(jnp.float32))
            later = _dot(hi, tri) + _dot(lo, tri) + c_ref[...]
            w = jnp.exp(lf + z + later)
            if diag:
                w = jnp.where(strict, w, 0.0)
            acc_ref[...] += _dot(_bf16(w), v)
            c_ref[...] += jnp.sum(lf, axis=1, keepdims=True)

        tile(qi, True)

        def cond(carry):
            j, cmax = carry
            return jnp.logical_and(j >= 0, cmax > SB_LOG_ZERO)

        def body(carry):
            j, _ = carry
            tile(j, False)
            return j - 1, jnp.max(c_ref[...])

        lax.while_loop(cond, body, (qi - 1, jnp.max(c_ref[...])))
        out.append(acc_ref[...])
    o_ref[...] = _bf16(jnp.where(lane < SB_D, out[0], out[1]))


def _sb_attn(ub, B, S, t=256):
    nq = S // t
    c0 = DIFF_IN // 128
    return pl.pallas_call(
        functools.partial(_sb_attn_kernel, t=t),
        grid=(B, 2, nq),
        in_specs=[pl.BlockSpec((t, 128), lambda b, hp, i: (b * nq + i, c0 + hp)),
                  pl.BlockSpec((S, 128), lambda b, hp, i: (b, c0 + 2 + hp)),
                  pl.BlockSpec((S, 128), lambda b, hp, i: (b, c0 + 4 + hp))],
        out_specs=pl.BlockSpec((t, 128), lambda b, hp, i: (b * nq + i, hp)),
        out_shape=jax.ShapeDtypeStruct((B * S, SB_HEADS * SB_D), jnp.bfloat16),
        scratch_shapes=[pltpu.VMEM((t, 1), jnp.float32),
                        pltpu.VMEM((t, 128), jnp.float32)],
        compiler_params=_cparams(("parallel", "parallel", "arbitrary")),
        name="sb_attn",
    )(ub, ub, ub)


def _layer_norm(x, g, b):
    mu = jnp.mean(x, -1, keepdims=True)
    xc = x - mu
    var = jnp.mean(xc * xc, -1, keepdims=True)
    return xc * lax.rsqrt(var + EPS) * g + b


def _out_proj_kernel(h_ref, ym_ref, yd_ref, ys_ref, wm_ref, wd_ref, ws_ref, g_ref, b_ref, o_ref, *, alpha):
    mix = _dot(ym_ref[...], wm_ref[...]) + _dot(yd_ref[...], wd_ref[...]) + _dot(ys_ref[...], ws_ref[...])
    o_ref[...] = _layer_norm(alpha * h_ref[...] + mix, g_ref[...], b_ref[...])


def _out_proj(h2, ym, yd, ys, wm, wd, ws, g, b, alpha, tm=512):
    T = h2.shape[0]
    row = lambda a: pl.BlockSpec((tm, a.shape[1]), lambda i: (i, 0))
    full = lambda a: pl.BlockSpec(a.shape, lambda i: (0, 0))
    return pl.pallas_call(
        functools.partial(_out_proj_kernel, alpha=alpha),
        grid=(T // tm,),
        in_specs=[row(h2), row(ym), row(yd), row(ys), full(wm), full(wd), full(ws), full(g), full(b)],
        out_specs=pl.BlockSpec((tm, D_MODEL), lambda i: (i, 0)),
        out_shape=jax.ShapeDtypeStruct((T, D_MODEL), jnp.float32),
        compiler_params=_cparams(("parallel",)),
        name="out_proj_ln",
    )(h2, ym, yd, ys, wm, wd, ws, g, b)


def _first_max(vals):
    m = vals[0]
    for v in vals[1:]:
        m = jnp.maximum(m, v)
    taken = jnp.zeros(m.shape, jnp.bool_)
    hot = []
    for v in vals:
        is_first = jnp.logical_and(v == m, jnp.logical_not(taken))
        hot.append(is_first)
        taken = jnp.logical_or(taken, is_first)
    return m, hot


def _router_kernel(h_ref, rwt_ref, rb_ref, gate_ref):
    logits = lax.dot_general(rwt_ref[...], h_ref[...], _NT, preferred_element_type=jnp.float32,
                             precision=lax.Precision.HIGHEST)
    scores = jax.nn.sigmoid(logits)
    sel = scores + rb_ref[...]
    ninf = -jnp.inf
    group_score, first, second = [], [], []
    for g in range(N_GROUPS):
        vals = [sel[g * EXPERTS_PER_GROUP + k:g * EXPERTS_PER_GROUP + k + 1, :] for k in range(EXPERTS_PER_GROUP)]
        m1, hot1 = _first_max(vals)
        m2, hot2 = _first_max([jnp.where(hh, ninf, v) for hh, v in zip(hot1, vals)])
        group_score.append(m1 + m2)
        first.append(hot1)
        second.append(hot2)
    _, best = _first_max(group_score)
    w1 = jnp.zeros_like(group_score[0])
    w2 = jnp.zeros_like(group_score[0])
    for g in range(N_GROUPS):
        for k in range(EXPERTS_PER_GROUP):
            e = g * EXPERTS_PER_GROUP + k
            sc = scores[e:e + 1, :]
            w1 = jnp.where(jnp.logical_and(best[g], first[g][k]), sc, w1)
            w2 = jnp.where(jnp.logical_and(best[g], second[g][k]), sc, w2)
    tot = w1 + w2
    for g in range(N_GROUPS):
        for k in range(EXPERTS_PER_GROUP):
            e = g * EXPERTS_PER_GROUP + k
            ge = (jnp.where(jnp.logical_and(best[g], first[g][k]), w1 / tot, 0.0)
                  + jnp.where(jnp.logical_and(best[g], second[g][k]), w2 / tot, 0.0))
            gate_ref[e:e + 1, :] = ge


def _router(h2, rwt, rb_col, tm=512):
    T = h2.shape[0]
    return pl.pallas_call(
        _router_kernel,
        grid=(T // tm,),
        in_specs=[pl.BlockSpec((tm, D_MODEL), lambda i: (i, 0)),
                  pl.BlockSpec(rwt.shape, lambda i: (0, 0)),
                  pl.BlockSpec(rb_col.shape, lambda i: (0, 0))],
        out_specs=pl.BlockSpec((N_EXPERTS, tm), lambda i: (0, i)),
        out_shape=jax.ShapeDtypeStruct((N_EXPERTS, T), jnp.float32),
        compiler_params=_cparams(("parallel",)),
        name="router",
    )(h2, rwt, rb_col)


def _moe_kernel(h_ref, gate_ref, p_ref, wg_ref, wu_ref, wd_ref, pg_ref, pp_ref, g_ref, b_ref,
                o_ref, acc_ref, xb_ref, *, alpha):
    e = pl.program_id(1)

    @pl.when(e == 0)
    def _():
        h = h_ref[...]
        xb = _bf16(h)
        xb_ref[...] = xb
        ple = jax.nn.sigmoid(_dot(xb, pg_ref[...])) * _dot(_bf16(p_ref[...]), pp_ref[...])
        acc_ref[...] = alpha * h + ple

    xb = xb_ref[...]
    hid = jax.nn.silu(_dot(xb, wg_ref[0])) * _dot(xb, wu_ref[0])
    lane = lax.broadcasted_iota(jnp.int32, gate_ref.shape, 1)
    ge = jnp.sum(jnp.where(lane == e, gate_ref[...], 0.0), axis=1, keepdims=True)
    acc_ref[...] += ge * _dot(_bf16(hid), wd_ref[0])

    @pl.when(e == N_EXPERTS - 1)
    def _():
        o_ref[...] = _layer_norm(acc_ref[...], g_ref[...], b_ref[...])


def _moe(h2, gate, p2, wg, wu, wd, pg, pp, g, b, alpha, tm=1024):
    T = h2.shape[0]
    full = lambda a: pl.BlockSpec(a.shape, lambda i, e: (0, 0))
    return pl.pallas_call(
        functools.partial(_moe_kernel, alpha=alpha),
        grid=(T // tm, N_EXPERTS),
        in_specs=[pl.BlockSpec((tm, D_MODEL), lambda i, e: (i, 0)),
                  pl.BlockSpec((tm, N_EXPERTS), lambda i, e: (i, 0)),
                  pl.BlockSpec((tm, PLE_DIM), lambda i, e: (i, 0)),
                  pl.BlockSpec((1, D_MODEL, D_EXPERT), lambda i, e: (e, 0, 0)),
                  pl.BlockSpec((1, D_MODEL, D_EXPERT), lambda i, e: (e, 0, 0)),
                  pl.BlockSpec((1, D_EXPERT, D_MODEL), lambda i, e: (e, 0, 0)),
                  full(pg), full(pp), full(g), full(b)],
        out_specs=pl.BlockSpec((tm, D_MODEL), lambda i, e: (i, 0)),
        out_shape=jax.ShapeDtypeStruct((T, D_MODEL), jnp.float32),
        scratch_shapes=[pltpu.VMEM((tm, D_MODEL), jnp.float32),
                        pltpu.VMEM((tm, D_MODEL), jnp.bfloat16)],
        compiler_params=_cparams(("parallel", "arbitrary")),
        name="moe_ple_ln",
    )(h2, gate, p2, wg, wu, wd, pg, pp, g, b)


def _rotate_half_cols(w):
    half = w.shape[1] // 2
    return jnp.concatenate([-w[:, half:], w[:, :half]], axis=1)


def _in_proj_weights(w_in):
    z = lambda n: jnp.zeros((D_MODEL, n), w_in.dtype)
    cq = w_in[:, :MLA_Q_RANK]
    ckv = w_in[:, MLA_Q_RANK:MLA_Q_RANK + MLA_KV_RANK]
    kr = w_in[:, MLA_Q_RANK + MLA_KV_RANK:MLA_IN]
    n_dq = DIFF_HEADS * 2 * DIFF_QK
    sb0 = MLA_IN + DIFF_IN
    n_sq = SB_HEADS * SB_D
    cols = [ckv,
            z(MLA_NOPE), kr, z(128 - MLA_NOPE - MLA_ROPE),
            z(MLA_NOPE), _rotate_half_cols(kr), z(128 - MLA_NOPE - MLA_ROPE),
            cq, z(256 - MLA_Q_RANK),
            w_in[:, MLA_IN:MLA_IN + n_dq] * (DIFF_QK ** -0.5 * LOG2E),
            w_in[:, MLA_IN + n_dq:sb0],
            w_in[:, sb0:sb0 + n_sq] * (SB_D ** -0.5),
            w_in[:, sb0 + n_sq:]]
    return _bf16(jnp.concatenate(cols, axis=1))


def _mla_up_weights(w_uq, w_ukv):
    dq = MLA_NOPE + MLA_ROPE
    zq = lambda n: jnp.zeros((MLA_Q_RANK, n), w_uq.dtype)
    plain, rot = [], []
    for h in range(MLA_HEADS):
        wh = w_uq[:, h * dq:(h + 1) * dq]
        plain += [wh, zq(128 - dq)]
        rot += [zq(MLA_NOPE), _rotate_half_cols(wh[:, MLA_NOPE:]), zq(128 - dq)]
    wq = jnp.concatenate(plain + rot, axis=1)
    wq = jnp.concatenate([wq, jnp.zeros((256 - MLA_Q_RANK, wq.shape[1]), wq.dtype)], axis=0)
    dkv = MLA_NOPE + MLA_V
    zk = lambda n: jnp.zeros((MLA_KV_RANK, n), w_ukv.dtype)
    kcols, vcols = [], []
    for h in range(MLA_HEADS):
        wh = w_ukv[:, h * dkv:(h + 1) * dkv]
        kcols += [wh[:, :MLA_NOPE], zk(128 - MLA_NOPE)]
        vcols += [wh[:, MLA_NOPE:]]
    wkv = jnp.concatenate(kcols + vcols, axis=1)
    return _bf16(wq), _bf16(wkv)


def _attn_tiles(S):
    pick = lambda want: max(c for c in (128, 256, 512, 1024) if c <= want and S % c == 0)
    return pick(512), pick(512), pick(256)


def kernel(x, p, positions, w_in, mla_q_norm, mla_w_uq, mla_kv_norm, mla_w_ukv, diff_lambda_q1, diff_lambda_k1, diff_lambda_q2, diff_lambda_k2, diff_subln, rel_bias, w_o, ln1_g, ln1_b, router_w, router_b, w_gate, w_up, w_down, ple_proj, ple_gate, ln2_g, ln2_b):
    B, S, _ = x.shape
    depth = w_in.shape[0]
    T = B * S
    alpha = (2 * depth) ** 0.25
    t_mla, t_diff, t_sb = _attn_tiles(S)
    nq = S // t_diff

    pos_col = positions.reshape(T, 1)
    pos_tiles = positions.reshape(B * nq, 1, t_diff)
    tile_pos = positions.reshape(B * nq, t_diff)
    qmin = jnp.min(tile_pos, axis=1)
    kmax = jnp.max(tile_pos, axis=1)
    first = tile_pos[:, 0]
    consec = jnp.all(tile_pos[:, 1:] - tile_pos[:, :-1] == 1, axis=1).astype(jnp.int32)
    rb_flat = rel_bias.T.reshape(-1).astype(jnp.float32)

    half = MLA_ROPE // 2
    inv = ROPE_THETA ** (-jnp.arange(half, dtype=jnp.float32) / half)
    inv_lane = jnp.concatenate([jnp.zeros((MLA_NOPE,), jnp.float32), inv, inv,
                                jnp.zeros((128 - MLA_NOPE - MLA_ROPE,), jnp.float32)]).reshape(1, 128)

    rwt = router_w.T.astype(jnp.float32)
    rb_col = router_b.reshape(N_EXPERTS, 1).astype(jnp.float32)
    row = lambda a: a.reshape(1, -1)

    h = x.reshape(T, D_MODEL)
    for i in range(depth):
        lam_init = 0.8 - 0.6 * math.exp(-0.3 * i)
        ua, ub = _in_proj(h, _in_proj_weights(w_in[i]))
        wq, wkv = _mla_up_weights(mla_w_uq[i], mla_w_ukv[i])
        gq = jnp.concatenate([mla_q_norm[i], jnp.zeros((256 - MLA_Q_RANK,), mla_q_norm.dtype)]).reshape(1, 256)
        q_m, k_m, v_m = _mla_prep(ua, pos_col, inv_lane, gq, row(mla_kv_norm[i]), wq, wkv)
        y_mla = _mla_attn(q_m, k_m, v_m, B, S, t=t_mla)
        y_diff = _diff_attn(ub, pos_col, pos_tiles, qmin, kmax, first, consec, rb_flat,
                            row(diff_lambda_q1[i]), row(diff_lambda_k1[i]),
                            row(diff_lambda_q2[i]), row(diff_lambda_k2[i]), row(diff_subln[i]),
                            B, S, lam_init, t=t_diff)
        y_sb = _sb_attn(ub, B, S, t=t_sb)
        wo = _bf16(w_o[i])
        n_m, n_d = MLA_HEADS * MLA_V, DIFF_HEADS * DIFF_V
        h = _out_proj(h, y_mla, y_diff, y_sb, wo[:n_m], wo[n_m:n_m + n_d], wo[n_m + n_d:],
                      row(ln1_g[i]), row(ln1_b[i]), alpha)
        gate = _router(h, rwt, rb_col).T
        h = _moe(h, gate, p[i].reshape(T, PLE_DIM), _bf16(w_gate[i]), _bf16(w_up[i]), _bf16(w_down[i]),
                 _bf16(ple_gate[i]), _bf16(ple_proj[i]), row(ln2_g[i]), row(ln2_b[i]), alpha)
    return h.reshape(B, S, D_MODEL)
```

```python
import functools
import math

import jax
import jax.numpy as jnp
from jax import lax
from jax.experimental import pallas as pl
from jax.experimental.pallas import tpu as pltpu

D_MODEL = 1024
CHUNK = 64
PLE_DIM = 256
MLA_HEADS = 4
MLA_NOPE = 64
MLA_ROPE = 32
MLA_V = 64
MLA_Q_RANK = 192
MLA_KV_RANK = 128
ROPE_THETA = 10000.0
DIFF_HEADS = 4
DIFF_QK = 64
DIFF_V = 2 * DIFF_QK
SB_HEADS = 4
SB_D = 64
REL_BUCKETS = 32
REL_MAX_DIST = 128
N_EXPERTS = 16
N_GROUPS = 4
EXPERTS_PER_GROUP = N_EXPERTS // N_GROUPS
D_EXPERT = 512
MLA_IN = MLA_Q_RANK + MLA_KV_RANK + MLA_ROPE
DIFF_IN = 2 * DIFF_HEADS * 2 * DIFF_QK + DIFF_HEADS * DIFF_V
SB_IN = 3 * SB_HEADS * SB_D
EPS = 1e-5
NEG_INF = -1e30

LANES = 128
VMEM_LIMIT = 56 * 1024 * 1024

UA_CKV = 0
UA_KR = 128
UA_KRS = 256
UA_CQ = 384
UA_W = 640
UB_W = DIFF_IN + SB_IN

T5_FAR = 91
SB_LOG_ZERO = -88.0
MOE_SUB = 512
MOE_SLOTS = 128

_NT = (((1,), (1,)), ((), ()))
LOG2E = math.log2(math.e)


def _cparams(sem):
    return pltpu.CompilerParams(dimension_semantics=sem, vmem_limit_bytes=VMEM_LIMIT)


def _bf16(a):
    return a.astype(jnp.bfloat16)


def _dot(a, b):
    return jnp.dot(a, b, preferred_element_type=jnp.float32)


def _in_proj_kernel(x_ref, w_ref, ua_ref, ub_ref):
    x = _bf16(x_ref[...])
    step = 512
    for c in range(0, UA_W, step):
        e = min(c + step, UA_W)
        ua_ref[:, c:e] = _dot(x, w_ref[:, c:e])
    for c in range(0, UB_W, step):
        e = min(c + step, UB_W)
        ub_ref[:, c:e] = _bf16(_dot(x, w_ref[:, UA_W + c:UA_W + e]))


def _in_proj(h2, w_all, tm=512):
    T = h2.shape[0]
    return pl.pallas_call(
        _in_proj_kernel,
        grid=(T // tm,),
        in_specs=[pl.BlockSpec((tm, D_MODEL), lambda i: (i, 0)),
                  pl.BlockSpec((D_MODEL, UA_W + UB_W), lambda i: (0, 0))],
        out_specs=[pl.BlockSpec((tm, UA_W), lambda i: (i, 0)),
                   pl.BlockSpec((tm, UB_W), lambda i: (i, 0))],
        out_shape=[jax.ShapeDtypeStruct((T, UA_W), jnp.float32),
                   jax.ShapeDtypeStruct((T, UB_W), jnp.bfloat16)],
        compiler_params=_cparams(("parallel",)),
        name="in_proj",
    )(h2, w_all)


def _mla_prep_kernel(ua_ref, pos_ref, inv_ref, gq_ref, gkv_ref, wq_ref, wkv_ref,
                     q_ref, k_ref, v_ref):
    ckv = ua_ref[:, UA_CKV:UA_CKV + 128]
    kr = ua_ref[:, UA_KR:UA_KR + 128]
    krs = ua_ref[:, UA_KRS:UA_KRS + 128]
    cq = ua_ref[:, UA_CQ:UA_CQ + 256]
    cqn = cq * lax.rsqrt(jnp.sum(cq * cq, -1, keepdims=True) * (1.0 / MLA_Q_RANK) + EPS) * gq_ref[...]
    ckvn = ckv * lax.rsqrt(jnp.sum(ckv * ckv, -1, keepdims=True) * (1.0 / MLA_KV_RANK) + EPS) * gkv_ref[...]
    ang = pos_ref[...].astype(jnp.float32) * inv_ref[...]
    lane = lax.broadcasted_iota(jnp.int32, ang.shape, 1)
    rope_lane = (lane >= MLA_NOPE) & (lane < MLA_NOPE + MLA_ROPE)
    cosf = jnp.where(lane < MLA_NOPE, 1.0, jnp.where(rope_lane, jnp.cos(ang), 0.0))
    sinf = jnp.where(rope_lane, jnp.sin(ang), 0.0)
    scale = (MLA_NOPE + MLA_ROPE) ** -0.5 * LOG2E
    cqb = _bf16(cqn)
    ckvb = _bf16(ckvn)
    k_rope = kr * cosf + krs * sinf
    for h in range(MLA_HEADS):
        a = _dot(cqb, wq_ref[:, h * 128:(h + 1) * 128])
        b = _dot(cqb, wq_ref[:, 512 + h * 128:512 + (h + 1) * 128])
        q_ref[:, h * 128:(h + 1) * 128] = _bf16((a * cosf + b * sinf) * scale)
        kn = _dot(ckvb, wkv_ref[:, h * 128:(h + 1) * 128])
        k_ref[:, h * 128:(h + 1) * 128] = _bf16(kn + k_rope)
    v_ref[...] = _bf16(_dot(ckvb, wkv_ref[:, 512:768]))


def _mla_prep(ua, pos_col, inv_lane, gq, gkv, wq, wkv, tm=512):
    T = ua.shape[0]
    full = lambda a: pl.BlockSpec(a.shape, lambda i: (0, 0))
    return pl.pallas_call(
        _mla_prep_kernel,
        grid=(T // tm,),
        in_specs=[pl.BlockSpec((tm, UA_W), lambda i: (i, 0)),
                  pl.BlockSpec((tm, 1), lambda i: (i, 0)),
                  full(inv_lane), full(gq), full(gkv), full(wq), full(wkv)],
        out_specs=[pl.BlockSpec((tm, 512), lambda i: (i, 0)),
                   pl.BlockSpec((tm, 512), lambda i: (i, 0)),
                   pl.BlockSpec((tm, 256), lambda i: (i, 0))],
        out_shape=[jax.ShapeDtypeStruct((T, 512), jnp.bfloat16),
                   jax.ShapeDtypeStruct((T, 512), jnp.bfloat16),
                   jax.ShapeDtypeStruct((T, 256), jnp.bfloat16)],
        compiler_params=_cparams(("parallel",)),
        name="mla_prep",
    )(ua, pos_col, inv_lane, gq, gkv, wq, wkv)


def _chunk_mask(tq, tk):
    qc = lax.broadcasted_iota(jnp.int32, (tq, tk), 0) // CHUNK
    kc = lax.broadcasted_iota(jnp.int32, (tq, tk), 1) // CHUNK
    return kc <= qc


def _softmax_update(s, v, m_ref, l_ref, acc_ref, idx, shift=None):
    tk = s.shape[1]
    m_old = m_ref[idx]
    row_max = jnp.max(s, axis=1, keepdims=True)
    if shift is not None:
        row_max = row_max + shift
    m_new = jnp.maximum(m_old, row_max)
    alpha = jnp.exp2(m_old - m_new)
    m_sub = m_new if shift is None else m_new - shift
    p = jnp.exp2(s - jnp.concatenate([m_sub] * (tk // LANES), axis=1))
    psum = p[:, :LANES]
    for c in range(LANES, tk, LANES):
        psum = psum + p[:, c:c + LANES]
    l_ref[idx] = alpha * l_ref[idx] + psum
    acc_ref[idx] = alpha * acc_ref[idx] + _dot(_bf16(p), v)
    m_ref[idx] = m_new


def _softmax_init(m_ref, l_ref, acc_ref):
    m_ref[...] = jnp.full(m_ref.shape, NEG_INF, jnp.float32)
    l_ref[...] = jnp.zeros(l_ref.shape, jnp.float32)
    acc_ref[...] = jnp.zeros(acc_ref.shape, jnp.float32)


def _softmax_result(l_ref, acc_ref, idx):
    return acc_ref[idx] / jnp.sum(l_ref[idx], axis=1, keepdims=True)


def _mla_attn_kernel(q_ref, k_ref, v_ref, o_ref, m_ref, l_ref, acc_ref, *, t):
    qi = pl.program_id(2)
    _softmax_init(m_ref, l_ref, acc_ref)

    def tile(j, masked):
        ks = pl.multiple_of(j * t, t)
        v = v_ref[pl.ds(ks, t), :]
        for hh in range(2):
            q = q_ref[:, hh * 128:(hh + 1) * 128]
            k = k_ref[pl.ds(ks, t), hh * 128:(hh + 1) * 128]
            s = lax.dot_general(q, k, _NT, preferred_element_type=jnp.float32)
            if masked:
                s = jnp.where(_chunk_mask(t, t), s, NEG_INF)
            _softmax_update(s, v, m_ref, l_ref, acc_ref, hh)

    def body(i, carry):
        tile(2 * i, False)
        tile(2 * i + 1, False)
        return carry

    lax.fori_loop(0, qi // 2, body, 0)

    @pl.when(qi % 2 == 1)
    def _():
        tile(qi - 1, False)

    tile(qi, True)
    lane = lax.broadcasted_iota(jnp.int32, (t, 128), 1)
    o_ref[...] = _bf16(jnp.where(lane < MLA_V, _softmax_result(l_ref, acc_ref, 0),
                                 _softmax_result(l_ref, acc_ref, 1)))


def _mla_attn(q, k, v, B, S, t=256):
    nq = S // t
    return pl.pallas_call(
        functools.partial(_mla_attn_kernel, t=t),
        grid=(B, 2, nq),
        in_specs=[pl.BlockSpec((t, 256), lambda b, hp, i: (b * nq + i, hp)),
                  pl.BlockSpec((S, 256), lambda b, hp, i: (b, hp)),
                  pl.BlockSpec((S, 128), lambda b, hp, i: (b, hp))],
        out_specs=pl.BlockSpec((t, 128), lambda b, hp, i: (b * nq + i, hp)),
        out_shape=jax.ShapeDtypeStruct((B * S, MLA_HEADS * MLA_V), jnp.bfloat16),
        scratch_shapes=[pltpu.VMEM((2, t, LANES), jnp.float32)] * 3,
        compiler_params=_cparams(("parallel", "parallel", "arbitrary")),
        name="mla_attn",
    )(q, k, v)


def _t5_bias(rel, rb_ref, h):
    nb = REL_BUCKETS // 2
    max_exact = nb // 2
    n = jnp.abs(rel)
    nf = jnp.maximum(n, 1).astype(jnp.float32)
    large = max_exact + (jnp.log(nf / max_exact) / math.log(REL_MAX_DIST / max_exact)
                         * (nb - max_exact)).astype(jnp.int32)
    large = jnp.minimum(large, nb - 1)
    low = jnp.where(n < max_exact, n, large)
    neg = jnp.zeros(rel.shape, jnp.float32)
    pos = jnp.zeros(rel.shape, jnp.float32)
    for j in range(nb):
        eq = low == j
        neg = jnp.where(eq, rb_ref[h * REL_BUCKETS + j], neg)
        pos = jnp.where(eq, rb_ref[h * REL_BUCKETS + nb + j], pos)
    return jnp.where(rel > 0, pos, neg)


def _diff_attn_kernel(qmin_ref, kmax_ref, first_ref, consec_ref,
                      rb_ref, q_ref, k_ref, v_ref, posq_ref, posk_ref,
                      lq1_ref, lk1_ref, lq2_ref, lk2_ref, sub_ref,
                      o_ref, m_ref, l_ref, acc_ref, *, t, nq, lam_init):
    b = pl.program_id(0)
    h = pl.program_id(1)
    qi = pl.program_id(2)
    _softmax_init(m_ref, l_ref, acc_ref)

    q = q_ref[...]
    lane = lax.broadcasted_iota(jnp.int32, q.shape, 1)
    zero = jnp.zeros_like(q)
    qs = jnp.concatenate([jnp.where(lane < DIFF_QK, q, zero), jnp.where(lane < DIFF_QK, zero, q)], axis=0)
    tq = b * nq + qi
    qmin = qmin_ref[tq]

    def toeplitz_bias(j):
        d0 = first_ref[b * nq + j] - first_ref[tq]
        x = lax.broadcasted_iota(jnp.int32, (1, 2 * t), 1)
        g = _t5_bias(d0 + jnp.where(x < t, x, x - 2 * t), rb_ref, h)
        g = pltpu.roll(jnp.broadcast_to(g, (t, 2 * t)), 0, 1, stride=1, stride_axis=0)
        return g[:, :t]

    def general_bias(j):
        rel = posk_ref[j] - posq_ref[...]
        return _t5_bias(rel, rb_ref, h)

    def tile(j, masked, bias_fn):
        ks = pl.multiple_of(j * t, t)
        k = k_ref[pl.ds(ks, t), :]
        v = v_ref[pl.ds(ks, t), :]
        s = lax.dot_general(qs, k, _NT, preferred_element_type=jnp.float32)
        if bias_fn is None:
            shift = rb_ref[h * REL_BUCKETS + REL_BUCKETS // 2 - 1] * LOG2E
        else:
            shift = None
            s = s.reshape(2, t, t) + (bias_fn(j) * LOG2E)[None]
            if masked:
                s = jnp.where(_chunk_mask(t, t)[None], s, NEG_INF)
            s = s.reshape(2 * t, t)
        _softmax_update(s, v, m_ref, l_ref, acc_ref, 0, shift)

    def near_tile(j, masked):
        consecutive = jnp.logical_and(consec_ref[tq] == 1, consec_ref[b * nq + j] == 1)

        @pl.when(consecutive)
        def _():
            tile(j, masked, toeplitz_bias)

        @pl.when(jnp.logical_not(consecutive))
        def _():
            tile(j, masked, general_bias)

    def body(j, carry):
        far = kmax_ref[b * nq + j] - qmin <= -T5_FAR

        @pl.when(far)
        def _():
            tile(j, False, None)

        @pl.when(jnp.logical_not(far))
        def _():
            near_tile(j, False)

        return carry

    lax.fori_loop(0, qi, body, 0)
    near_tile(qi, True)

    f32 = jnp.float32
    lam = (jnp.exp(jnp.sum(lq1_ref[...].astype(f32) * lk1_ref[...].astype(f32), keepdims=True))
           - jnp.exp(jnp.sum(lq2_ref[...].astype(f32) * lk2_ref[...].astype(f32), keepdims=True))
           + lam_init)
    a = _softmax_result(l_ref, acc_ref, 0)
    o = a[:t] - lam * a[t:]
    o = o * lax.rsqrt(jnp.mean(o * o, -1, keepdims=True) + EPS) * sub_ref[...] * (1.0 - lam_init)
    o_ref[...] = _bf16(o)


def _diff_attn(ub, pos_col, pos_tiles, qmin, kmax, first, consec, rb_flat, lq1, lk1, lq2, lk2, subln,
               B, S, lam_init, t=256):
    nq = S // t
    H = DIFF_HEADS
    small = lambda a: pl.BlockSpec(a.shape, lambda b, h, i, *_: (0, 0))
    grid_spec = pltpu.PrefetchScalarGridSpec(
        num_scalar_prefetch=4,
        grid=(B, H, nq),
        in_specs=[pl.BlockSpec(memory_space=pltpu.SMEM),
                  pl.BlockSpec((t, 128), lambda b, h, i, *_: (b * nq + i, h)),
                  pl.BlockSpec((S, 128), lambda b, h, i, *_: (b, H + h)),
                  pl.BlockSpec((S, 128), lambda b, h, i, *_: (b, 2 * H + h)),
                  pl.BlockSpec((t, 1), lambda b, h, i, *_: (b * nq + i, 0)),
                  pl.BlockSpec((nq, 1, t), lambda b, h, i, *_: (b, 0, 0)),
                  small(lq1), small(lk1), small(lq2), small(lk2), small(subln)],
        out_specs=pl.BlockSpec((t, 128), lambda b, h, i, *_: (b * nq + i, h)),
        scratch_shapes=[pltpu.VMEM((1, 2 * t, LANES), jnp.float32)] * 3,
    )
    return pl.pallas_call(
        functools.partial(_diff_attn_kernel, t=t, nq=nq, lam_init=lam_init),
        grid_spec=grid_spec,
        out_shape=jax.ShapeDtypeStruct((B * S, H * DIFF_V), jnp.bfloat16),
        compiler_params=_cparams(("parallel", "parallel", "arbitrary")),
        name="diff_attn",
    )(qmin, kmax, first, consec, rb_flat, ub, ub, ub, pos_col, pos_tiles, lq1, lk1, lq2, lk2, subln)


def _sb_attn_kernel(q_ref, k_ref, v_ref, o_ref, c_ref, acc_ref, *, t):
    qi = pl.program_id(2)
    row = lax.broadcasted_iota(jnp.int32, (t, t), 0)
    col = lax.broadcasted_iota(jnp.int32, (t, t), 1)
    tri = jnp.where(row > col, 1.0, 0.0).astype(jnp.bfloat16)
    strict = (col < row)[None]
    q = q_ref[...]
    lane = lax.broadcasted_iota(jnp.int32, q.shape, 1)
    zero = jnp.zeros_like(q)
    qs = jnp.concatenate([jnp.where(lane < SB_D, q, zero), jnp.where(lane < SB_D, zero, q)], axis=0)
    c_ref[...] = jnp.zeros(c_ref.shape, jnp.float32)
    acc_ref[...] = jnp.zeros(acc_ref.shape, jnp.float32)

    def tile(j, diag):
        ks = pl.multiple_of(j * t, t)
        k = k_ref[pl.ds(ks, t), :]
        v = v_ref[pl.ds(ks, t), :]
        z = lax.dot_general(qs, k, _NT, preferred_element_type=jnp.float32)
        lf = -(jnp.maximum(z, 0.0) + jnp.log(1.0 + jnp.exp(-jnp.abs(z))))
        if diag:
            lf = jnp.where(strict, lf.reshape(2, t, t), 0.0).reshape(2 * t, t)
        hi = _bf16(lf)
        lo = _bf16(lf - hi.astype(jnp.float32))
        c = c_ref[...]
        later = _dot(hi, tri) + _dot(lo, tri) + jnp.concatenate([c] * (t // LANES), axis=1)
        w = jnp.exp(lf + z + later)
        if diag:
            w = jnp.where(strict, w.reshape(2, t, t), 0.0).reshape(2 * t, t)
        acc_ref[...] += _dot(_bf16(w), v)
        c_ref[...] = c + jnp.sum(lf, axis=1, keepdims=True)

    tile(qi, True)

    def cond(carry):
        j, cmax = carry
        return jnp.logical_and(j >= 0, cmax > SB_LOG_ZERO)

    def body(carry):
        j, _ = carry
        tile(j, False)
        return j - 1, jnp.max(c_ref[...])

    lax.while_loop(cond, body, (qi - 1, jnp.max(c_ref[...])))
    acc = acc_ref[...]
    o_ref[...] = _bf16(jnp.where(lane < SB_D, acc[:t], acc[t:]))


def _sb_attn(ub, B, S, t=256):
    nq = S // t
    c0 = DIFF_IN // 128
    return pl.pallas_call(
        functools.partial(_sb_attn_kernel, t=t),
        grid=(B, 2, nq),
        in_specs=[pl.BlockSpec((t, 128), lambda b, hp, i: (b * nq + i, c0 + hp)),
                  pl.BlockSpec((S, 128), lambda b, hp, i: (b, c0 + 2 + hp)),
                  pl.BlockSpec((S, 128), lambda b, hp, i: (b, c0 + 4 + hp))],
        out_specs=pl.BlockSpec((t, 128), lambda b, hp, i: (b * nq + i, hp)),
        out_shape=jax.ShapeDtypeStruct((B * S, SB_HEADS * SB_D), jnp.bfloat16),
        scratch_shapes=[pltpu.VMEM((2 * t, LANES), jnp.float32)] * 2,
        compiler_params=_cparams(("parallel", "parallel", "arbitrary")),
        name="sb_attn",
    )(ub, ub, ub)


def _layer_norm(x, g, b):
    mu = jnp.mean(x, -1, keepdims=True)
    xc = x - mu
    var = jnp.mean(xc * xc, -1, keepdims=True)
    return xc * lax.rsqrt(var + EPS) * g + b


def _out_proj_kernel(h_ref, ym_ref, yd_ref, ys_ref, wm_ref, wd_ref, ws_ref, g_ref, b_ref, o_ref, *, alpha):
    mix = _dot(ym_ref[...], wm_ref[...]) + _dot(yd_ref[...], wd_ref[...]) + _dot(ys_ref[...], ws_ref[...])
    o_ref[...] = _layer_norm(alpha * h_ref[...] + mix, g_ref[...], b_ref[...])


def _out_proj(h2, ym, yd, ys, wm, wd, ws, g, b, alpha, tm=512):
    T = h2.shape[0]
    row = lambda a: pl.BlockSpec((tm, a.shape[1]), lambda i: (i, 0))
    full = lambda a: pl.BlockSpec(a.shape, lambda i: (0, 0))
    return pl.pallas_call(
        functools.partial(_out_proj_kernel, alpha=alpha),
        grid=(T // tm,),
        in_specs=[row(h2), row(ym), row(yd), row(ys), full(wm), full(wd), full(ws), full(g), full(b)],
        out_specs=pl.BlockSpec((tm, D_MODEL), lambda i: (i, 0)),
        out_shape=jax.ShapeDtypeStruct((T, D_MODEL), jnp.float32),
        compiler_params=_cparams(("parallel",)),
        name="out_proj_ln",
    )(h2, ym, yd, ys, wm, wd, ws, g, b)


def _first_max(vals):
    m = vals[0]
    for v in vals[1:]:
        m = jnp.maximum(m, v)
    taken = jnp.zeros(m.shape, jnp.bool_)
    hot = []
    for v in vals:
        is_first = jnp.logical_and(v == m, jnp.logical_not(taken))
        hot.append(is_first)
        taken = jnp.logical_or(taken, is_first)
    return m, hot


def _router_kernel(h_ref, rwt_ref, rb_ref, gate_ref, slot_ref, chosen_ref):
    logits = lax.dot_general(rwt_ref[...], h_ref[...], _NT, preferred_element_type=jnp.float32,
                             precision=lax.Precision.HIGHEST)
    scores = jax.nn.sigmoid(logits)
    sel = scores + rb_ref[...]
    ninf = -jnp.inf
    group_score, first, second = [], [], []
    for g in range(N_GROUPS):
        vals = [sel[g * EXPERTS_PER_GROUP + k:g * EXPERTS_PER_GROUP + k + 1, :] for k in range(EXPERTS_PER_GROUP)]
        m1, hot1 = _first_max(vals)
        m2, hot2 = _first_max([jnp.where(hh, ninf, v) for hh, v in zip(hot1, vals)])
        group_score.append(m1 + m2)
        first.append(hot1)
        second.append(hot2)
    _, best = _first_max(group_score)
    w1 = jnp.zeros_like(group_score[0])
    w2 = jnp.zeros_like(group_score[0])
    for g in range(N_GROUPS):
        for k in range(EXPERTS_PER_GROUP):
            e = g * EXPERTS_PER_GROUP + k
            sc = scores[e:e + 1, :]
            w1 = jnp.where(jnp.logical_and(best[g], first[g][k]), sc, w1)
            w2 = jnp.where(jnp.logical_and(best[g], second[g][k]), sc, w2)
    tot = w1 + w2
    for g in range(N_GROUPS):
        for k in range(EXPERTS_PER_GROUP):
            e = g * EXPERTS_PER_GROUP + k
            is1 = jnp.logical_and(best[g], first[g][k])
            is2 = jnp.logical_and(best[g], second[g][k])
            gate_ref[e:e + 1, :] = jnp.where(is1, w1 / tot, 0.0) + jnp.where(is2, w2 / tot, 0.0)
            chosen_ref[e:e + 1, :] = jnp.where(jnp.logical_or(is1, is2), 1.0, 0.0)
    chosen = chosen_ref[...]
    tm = chosen.shape[1]
    before = (lax.broadcasted_iota(jnp.int32, (tm, tm), 0) < lax.broadcasted_iota(jnp.int32, (tm, tm), 1))
    rank = _dot(_bf16(chosen), jnp.where(before, 1.0, 0.0).astype(jnp.bfloat16))
    slot_ref[...] = jnp.where(chosen > 0.0, rank.astype(jnp.int32), -1)


def _router(h2, rwt, rb_col):
    T = h2.shape[0]
    tm = MOE_SUB
    return pl.pallas_call(
        _router_kernel,
        grid=(T // tm,),
        in_specs=[pl.BlockSpec((tm, D_MODEL), lambda i: (i, 0)),
                  pl.BlockSpec(rwt.shape, lambda i: (0, 0)),
                  pl.BlockSpec(rb_col.shape, lambda i: (0, 0))],
        out_specs=[pl.BlockSpec((N_EXPERTS, tm), lambda i: (0, i)),
                   pl.BlockSpec((N_EXPERTS, tm), lambda i: (0, i))],
        out_shape=[jax.ShapeDtypeStruct((N_EXPERTS, T), jnp.float32),
                   jax.ShapeDtypeStruct((N_EXPERTS, T), jnp.int32)],
        scratch_shapes=[pltpu.VMEM((N_EXPERTS, tm), jnp.float32)],
        compiler_params=_cparams(("parallel",)),
        name="router",
    )(h2, rwt, rb_col)


def _moe_kernel(npass_ref, h_ref, slot_ref, slot_t_ref, gate_ref, p_ref, wg_ref, wu_ref, wd_ref,
                pg_ref, pp_ref, g_ref, b_ref, o_ref, over_ref, xb_ref, y_ref, *, alpha, n_sub):
    i = pl.program_id(0)
    e = pl.program_id(1)
    R, SUB = MOE_SLOTS, MOE_SUB

    @pl.when(e == 0)
    def _():
        xb_ref[...] = _bf16(h_ref[...])
        over_ref[...] = jnp.zeros(over_ref.shape, jnp.float32)

    slot_row = slot_ref[pl.ds(e, 1), :]
    gate_row = gate_ref[pl.ds(e, 1), :]

    def expert_pass(c):
        xs, gs = [], []
        for j in range(n_sub):
            sl = slot_row[:, j * SUB:(j + 1) * SUB] - c * R
            hit = lax.broadcasted_iota(jnp.int32, (R, SUB), 0) == sl
            onehot = jnp.where(hit, 1.0, 0.0).astype(jnp.bfloat16)
            xs.append(_bf16(_dot(onehot, xb_ref[j * SUB:(j + 1) * SUB, :])))
            gs.append(jnp.sum(jnp.where(hit, gate_row[:, j * SUB:(j + 1) * SUB], 0.0), axis=1, keepdims=True))
        xe = jnp.concatenate(xs, axis=0)
        hid = jax.nn.silu(_dot(xe, wg_ref[0])) * _dot(xe, wu_ref[0])
        y = _dot(_bf16(hid), wd_ref[0])
        return [_bf16(y[j * R:(j + 1) * R] * gs[j]) for j in range(n_sub)]

    ys = expert_pass(0)
    for j in range(n_sub):
        y_ref[j, pl.ds(pl.multiple_of(e * R, R), R), :] = ys[j]

    def overflow_pass(c, carry):
        ys = expert_pass(c)
        lane16 = lax.broadcasted_iota(jnp.int32, (SUB, N_EXPERTS), 1)
        for j in range(n_sub):
            rows = slice(j * SUB, (j + 1) * SUB)
            slot_col = jnp.sum(jnp.where(lane16 == e, slot_t_ref[rows, :], 0), axis=1, keepdims=True)
            hit_t = lax.broadcasted_iota(jnp.int32, (SUB, R), 1) == slot_col - c * R
            over_ref[rows, :] += _dot(jnp.where(hit_t, 1.0, 0.0).astype(jnp.bfloat16), ys[j])
        return carry

    lax.fori_loop(1, npass_ref[i * N_EXPERTS + e], overflow_pass, 0)

    @pl.when(e == N_EXPERTS - 1)
    def _():
        lane = lax.broadcasted_iota(jnp.int32, (SUB, R), 1)
        for j in range(n_sub):
            rows = slice(j * SUB, (j + 1) * SUB)
            slot_t = slot_t_ref[rows, :]
            scatter = jnp.concatenate(
                [jnp.where(lane == slot_t[:, k:k + 1], 1.0, 0.0).astype(jnp.bfloat16) for k in range(N_EXPERTS)],
                axis=1)
            ffn = _dot(scatter, y_ref[j]) + over_ref[rows, :]
            h = h_ref[rows, :]
            ple = jax.nn.sigmoid(_dot(xb_ref[rows, :], pg_ref[...])) * _dot(_bf16(p_ref[rows, :]), pp_ref[...])
            o_ref[rows, :] = _layer_norm(alpha * h + ffn + ple, g_ref[...], b_ref[...])


def _moe(h2, slot, gate, p2, wg, wu, wd, pg, pp, g, b, alpha, tm=1024):
    T = h2.shape[0]
    n_sub = tm // MOE_SUB
    n_tiles = T // tm
    count = jnp.sum((slot >= 0).reshape(N_EXPERTS, n_tiles, n_sub, MOE_SUB), axis=3)
    npass = jnp.maximum(1, (jnp.max(count, axis=2) + MOE_SLOTS - 1) // MOE_SLOTS).T.reshape(-1).astype(jnp.int32)
    full = lambda a: pl.BlockSpec(a.shape, lambda i, e, *_: (0, 0))
    grid_spec = pltpu.PrefetchScalarGridSpec(
        num_scalar_prefetch=1,
        grid=(n_tiles, N_EXPERTS),
        in_specs=[pl.BlockSpec((tm, D_MODEL), lambda i, e, *_: (i, 0)),
                  pl.BlockSpec((N_EXPERTS, tm), lambda i, e, *_: (0, i)),
                  pl.BlockSpec((tm, N_EXPERTS), lambda i, e, *_: (i, 0)),
                  pl.BlockSpec((N_EXPERTS, tm), lambda i, e, *_: (0, i)),
                  pl.BlockSpec((tm, PLE_DIM), lambda i, e, *_: (i, 0)),
                  pl.BlockSpec((1, D_MODEL, D_EXPERT), lambda i, e, *_: (e, 0, 0)),
                  pl.BlockSpec((1, D_MODEL, D_EXPERT), lambda i, e, *_: (e, 0, 0)),
                  pl.BlockSpec((1, D_EXPERT, D_MODEL), lambda i, e, *_: (e, 0, 0)),
                  full(pg), full(pp), full(g), full(b)],
        out_specs=pl.BlockSpec((tm, D_MODEL), lambda i, e, *_: (i, 0)),
        scratch_shapes=[pltpu.VMEM((tm, D_MODEL), jnp.float32),
                        pltpu.VMEM((tm, D_MODEL), jnp.bfloat16),
                        pltpu.VMEM((n_sub, N_EXPERTS * MOE_SLOTS, D_MODEL), jnp.bfloat16)],
    )
    return pl.pallas_call(
        functools.partial(_moe_kernel, alpha=alpha, n_sub=n_sub),
        grid_spec=grid_spec,
        out_shape=jax.ShapeDtypeStruct((T, D_MODEL), jnp.float32),
        compiler_params=_cparams(("parallel", "arbitrary")),
        name="moe_ple_ln",
    )(npass, h2, slot, slot.T, gate, p2, wg, wu, wd, pg, pp, g, b)


def _rotate_half_cols(w):
    half = w.shape[1] // 2
    return jnp.concatenate([-w[:, half:], w[:, :half]], axis=1)


def _in_proj_weights(w_in):
    z = lambda n: jnp.zeros((D_MODEL, n), w_in.dtype)
    cq = w_in[:, :MLA_Q_RANK]
    ckv = w_in[:, MLA_Q_RANK:MLA_Q_RANK + MLA_KV_RANK]
    kr = w_in[:, MLA_Q_RANK + MLA_KV_RANK:MLA_IN]
    n_dq = DIFF_HEADS * 2 * DIFF_QK
    sb0 = MLA_IN + DIFF_IN
    n_sq = SB_HEADS * SB_D
    cols = [ckv,
            z(MLA_NOPE), kr, z(128 - MLA_NOPE - MLA_ROPE),
            z(MLA_NOPE), _rotate_half_cols(kr), z(128 - MLA_NOPE - MLA_ROPE),
            cq, z(256 - MLA_Q_RANK),
            w_in[:, MLA_IN:MLA_IN + n_dq] * (DIFF_QK ** -0.5 * LOG2E),
            w_in[:, MLA_IN + n_dq:sb0],
            w_in[:, sb0:sb0 + n_sq] * (SB_D ** -0.5),
            w_in[:, sb0 + n_sq:]]
    return _bf16(jnp.concatenate(cols, axis=1))


def _mla_up_weights(w_uq, w_ukv):
    dq = MLA_NOPE + MLA_ROPE
    zq = lambda n: jnp.zeros((MLA_Q_RANK, n), w_uq.dtype)
    plain, rot = [], []
    for h in range(MLA_HEADS):
        wh = w_uq[:, h * dq:(h + 1) * dq]
        plain += [wh, zq(128 - dq)]
        rot += [zq(MLA_NOPE), _rotate_half_cols(wh[:, MLA_NOPE:]), zq(128 - dq)]
    wq = jnp.concatenate(plain + rot, axis=1)
    wq = jnp.concatenate([wq, jnp.zeros((256 - MLA_Q_RANK, wq.shape[1]), wq.dtype)], axis=0)
    dkv = MLA_NOPE + MLA_V
    zk = lambda n: jnp.zeros((MLA_KV_RANK, n), w_ukv.dtype)
    kcols, vcols = [], []
    for h in range(MLA_HEADS):
        wh = w_ukv[:, h * dkv:(h + 1) * dkv]
        kcols += [wh[:, :MLA_NOPE], zk(128 - MLA_NOPE)]
        vcols += [wh[:, MLA_NOPE:]]
    wkv = jnp.concatenate(kcols + vcols, axis=1)
    return _bf16(wq), _bf16(wkv)


def _attn_tiles(S):
    pick = lambda want: max(c for c in (128, 256, 512, 1024) if c <= want and S % c == 0)
    return pick(512), pick(512), pick(256)


def kernel(x, p, positions, w_in, mla_q_norm, mla_w_uq, mla_kv_norm, mla_w_ukv, diff_lambda_q1, diff_lambda_k1, diff_lambda_q2, diff_lambda_k2, diff_subln, rel_bias, w_o, ln1_g, ln1_b, router_w, router_b, w_gate, w_up, w_down, ple_proj, ple_gate, ln2_g, ln2_b):
    B, S, _ = x.shape
    depth = w_in.shape[0]
    T = B * S
    alpha = (2 * depth) ** 0.25
    t_mla, t_diff, t_sb = _attn_tiles(S)
    nq = S // t_diff

    pos_col = positions.reshape(T, 1)
    pos_tiles = positions.reshape(B * nq, 1, t_diff)
    tile_pos = positions.reshape(B * nq, t_diff)
    qmin = jnp.min(tile_pos, axis=1)
    kmax = jnp.max(tile_pos, axis=1)
    first = tile_pos[:, 0]
    consec = jnp.all(tile_pos[:, 1:] - tile_pos[:, :-1] == 1, axis=1).astype(jnp.int32)
    rb_flat = rel_bias.T.reshape(-1).astype(jnp.float32)

    half = MLA_ROPE // 2
    inv = ROPE_THETA ** (-jnp.arange(half, dtype=jnp.float32) / half)
    inv_lane = jnp.concatenate([jnp.zeros((MLA_NOPE,), jnp.float32), inv, inv,
                                jnp.zeros((128 - MLA_NOPE - MLA_ROPE,), jnp.float32)]).reshape(1, 128)

    rwt = router_w.T.astype(jnp.float32)
    rb_col = router_b.reshape(N_EXPERTS, 1).astype(jnp.float32)
    row = lambda a: a.reshape(1, -1)

    h = x.reshape(T, D_MODEL)
    for i in range(depth):
        lam_init = 0.8 - 0.6 * math.exp(-0.3 * i)
        ua, ub = _in_proj(h, _in_proj_weights(w_in[i]))
        wq, wkv = _mla_up_weights(mla_w_uq[i], mla_w_ukv[i])
        gq = jnp.concatenate([mla_q_norm[i], jnp.zeros((256 - MLA_Q_RANK,), mla_q_norm.dtype)]).reshape(1, 256)
        q_m, k_m, v_m = _mla_prep(ua, pos_col, inv_lane, gq, row(mla_kv_norm[i]), wq, wkv)
        y_mla = _mla_attn(q_m, k_m, v_m, B, S, t=t_mla)
        y_diff = _diff_attn(ub, pos_col, pos_tiles, qmin, kmax, first, consec, rb_flat,
                            row(diff_lambda_q1[i]), row(diff_lambda_k1[i]),
                            row(diff_lambda_q2[i]), row(diff_lambda_k2[i]), row(diff_subln[i]),
                            B, S, lam_init, t=t_diff)
        y_sb = _sb_attn(ub, B, S, t=t_sb)
        wo = _bf16(w_o[i])
        n_m, n_d = MLA_HEADS * MLA_V, DIFF_HEADS * DIFF_V
        h = _out_proj(h, y_mla, y_diff, y_sb, wo[:n_m], wo[n_m:n_m + n_d], wo[n_m + n_d:],
                      row(ln1_g[i]), row(ln1_b[i]), alpha)
        gate, slot = _router(h, rwt, rb_col)
        h = _moe(h, slot, gate, p[i].reshape(T, PLE_DIM), _bf16(w_gate[i]), _bf16(w_up[i]), _bf16(w_down[i]),
                 _bf16(ple_gate[i]), _bf16(ple_proj[i]), row(ln2_g[i]), row(ln2_b[i]), alpha)
    return h.reshape(B, S, D_MODEL)
```

```python
import functools
import math

import jax
import jax.numpy as jnp
from jax import lax
from jax.experimental import pallas as pl
from jax.experimental.pallas import tpu as pltpu

D_MODEL = 1024
CHUNK = 64
PLE_DIM = 256
MLA_HEADS = 4
MLA_NOPE = 64
MLA_ROPE = 32
MLA_V = 64
MLA_Q_RANK = 192
MLA_KV_RANK = 128
ROPE_THETA = 10000.0
DIFF_HEADS = 4
DIFF_QK = 64
DIFF_V = 2 * DIFF_QK
SB_HEADS = 4
SB_D = 64
REL_BUCKETS = 32
REL_MAX_DIST = 128
N_EXPERTS = 16
N_GROUPS = 4
EXPERTS_PER_GROUP = N_EXPERTS // N_GROUPS
D_EXPERT = 512
MLA_IN = MLA_Q_RANK + MLA_KV_RANK + MLA_ROPE
DIFF_IN = 2 * DIFF_HEADS * 2 * DIFF_QK + DIFF_HEADS * DIFF_V
SB_IN = 3 * SB_HEADS * SB_D
EPS = 1e-5
NEG_INF = -1e30

LANES = 128
VMEM_LIMIT = 56 * 1024 * 1024

UA_CKV = 0
UA_KR = 128
UA_KRS = 256
UA_CQ = 384
UA_W = 640
UB_W = DIFF_IN + SB_IN

T5_FAR = 91
SB_LOG_ZERO = -88.0
MOE_SUB = 512
MOE_SLOTS = 96

_NT = (((1,), (1,)), ((), ()))
LOG2E = math.log2(math.e)


def _cparams(sem):
    return pltpu.CompilerParams(dimension_semantics=sem, vmem_limit_bytes=VMEM_LIMIT)


def _bf16(a):
    return a.astype(jnp.bfloat16)


def _dot(a, b):
    return jnp.dot(a, b, preferred_element_type=jnp.float32)


def _in_proj_kernel(x_ref, w_ref, ua_ref, ub_ref):
    x = _bf16(x_ref[...])
    step = 512
    for c in range(0, UA_W, step):
        e = min(c + step, UA_W)
        ua_ref[:, c:e] = _dot(x, w_ref[:, c:e])
    for c in range(0, UB_W, step):
        e = min(c + step, UB_W)
        ub_ref[:, c:e] = _bf16(_dot(x, w_ref[:, UA_W + c:UA_W + e]))


def _in_proj(h2, w_all, tm=512):
    T = h2.shape[0]
    return pl.pallas_call(
        _in_proj_kernel,
        grid=(T // tm,),
        in_specs=[pl.BlockSpec((tm, D_MODEL), lambda i: (i, 0)),
                  pl.BlockSpec((D_MODEL, UA_W + UB_W), lambda i: (0, 0))],
        out_specs=[pl.BlockSpec((tm, UA_W), lambda i: (i, 0)),
                   pl.BlockSpec((tm, UB_W), lambda i: (i, 0))],
        out_shape=[jax.ShapeDtypeStruct((T, UA_W), jnp.float32),
                   jax.ShapeDtypeStruct((T, UB_W), jnp.bfloat16)],
        compiler_params=_cparams(("parallel",)),
        name="in_proj",
    )(h2, w_all)


def _rope_table_kernel(pos_ref, inv_ref, cos_ref, sin_ref):
    ang = pos_ref[...].astype(jnp.float32) * inv_ref[...]
    cos_ref[...] = jnp.cos(ang)
    sin_ref[...] = jnp.sin(ang)


def _rope_table(pos_row, inv_col, tn=2048):
    T = pos_row.shape[1]
    n = inv_col.shape[0]
    tn = min(tn, T)
    return pl.pallas_call(
        _rope_table_kernel,
        grid=(T // tn,),
        in_specs=[pl.BlockSpec((1, tn), lambda i: (0, i)),
                  pl.BlockSpec((n, 1), lambda i: (0, 0))],
        out_specs=[pl.BlockSpec((n, tn), lambda i: (0, i))] * 2,
        out_shape=[jax.ShapeDtypeStruct((n, T), jnp.float32)] * 2,
        compiler_params=_cparams(("parallel",)),
        name="rope_table",
    )(pos_row, inv_col)


def _mla_prep_kernel(ua_ref, cos_ref, sin_ref, gq_ref, gkv_ref, wq_ref, wkv_ref,
                     q_ref, k_ref, v_ref):
    ckv = ua_ref[:, UA_CKV:UA_CKV + 128]
    kr = ua_ref[:, UA_KR:UA_KR + 128]
    krs = ua_ref[:, UA_KRS:UA_KRS + 128]
    cq = ua_ref[:, UA_CQ:UA_CQ + 256]
    cqn = cq * lax.rsqrt(jnp.sum(cq * cq, -1, keepdims=True) * (1.0 / MLA_Q_RANK) + EPS) * gq_ref[...]
    ckvn = ckv * lax.rsqrt(jnp.sum(ckv * ckv, -1, keepdims=True) * (1.0 / MLA_KV_RANK) + EPS) * gkv_ref[...]
    cosf = cos_ref[...]
    sinf = sin_ref[...]
    scale = (MLA_NOPE + MLA_ROPE) ** -0.5 * LOG2E
    cqb = _bf16(cqn)
    ckvb = _bf16(ckvn)
    k_rope = kr * cosf + krs * sinf
    for h in range(MLA_HEADS):
        a = _dot(cqb, wq_ref[:, h * 128:(h + 1) * 128])
        b = _dot(cqb, wq_ref[:, 512 + h * 128:512 + (h + 1) * 128])
        q_ref[:, h * 128:(h + 1) * 128] = _bf16((a * cosf + b * sinf) * scale)
        kn = _dot(ckvb, wkv_ref[:, h * 128:(h + 1) * 128])
        k_ref[:, h * 128:(h + 1) * 128] = _bf16(kn + k_rope)
    v_ref[...] = _bf16(_dot(ckvb, wkv_ref[:, 512:768]))


def _mla_prep(ua, cosf, sinf, gq, gkv, wq, wkv, tm=512):
    T = ua.shape[0]
    full = lambda a: pl.BlockSpec(a.shape, lambda i: (0, 0))
    return pl.pallas_call(
        _mla_prep_kernel,
        grid=(T // tm,),
        in_specs=[pl.BlockSpec((tm, UA_W), lambda i: (i, 0)),
                  pl.BlockSpec((tm, LANES), lambda i: (i, 0)),
                  pl.BlockSpec((tm, LANES), lambda i: (i, 0)),
                  full(gq), full(gkv), full(wq), full(wkv)],
        out_specs=[pl.BlockSpec((tm, 512), lambda i: (i, 0)),
                   pl.BlockSpec((tm, 512), lambda i: (i, 0)),
                   pl.BlockSpec((tm, 256), lambda i: (i, 0))],
        out_shape=[jax.ShapeDtypeStruct((T, 512), jnp.bfloat16),
                   jax.ShapeDtypeStruct((T, 512), jnp.bfloat16),
                   jax.ShapeDtypeStruct((T, 256), jnp.bfloat16)],
        compiler_params=_cparams(("parallel",)),
        name="mla_prep",
    )(ua, cosf, sinf, gq, gkv, wq, wkv)


def _chunk_mask(tq, tk):
    qc = lax.broadcasted_iota(jnp.int32, (tq, tk), 0) // CHUNK
    kc = lax.broadcasted_iota(jnp.int32, (tq, tk), 1) // CHUNK
    return kc <= qc


def _softmax_update(s, v, m_ref, l_ref, acc_ref, idx, shift=None):
    tk = s.shape[1]
    m_old = m_ref[idx]
    row_max = jnp.max(s, axis=1, keepdims=True)
    if shift is not None:
        row_max = row_max + shift
    m_new = jnp.maximum(m_old, row_max)
    alpha = jnp.exp2(m_old - m_new)
    m_sub = m_new if shift is None else m_new - shift
    p = jnp.exp2(s - jnp.concatenate([m_sub] * (tk // LANES), axis=1))
    psum = p[:, :LANES]
    for c in range(LANES, tk, LANES):
        psum = psum + p[:, c:c + LANES]
    l_ref[idx] = alpha * l_ref[idx] + psum
    acc_ref[idx] = alpha * acc_ref[idx] + _dot(_bf16(p), v)
    m_ref[idx] = m_new


def _softmax_init(m_ref, l_ref, acc_ref):
    m_ref[...] = jnp.full(m_ref.shape, NEG_INF, jnp.float32)
    l_ref[...] = jnp.zeros(l_ref.shape, jnp.float32)
    acc_ref[...] = jnp.zeros(acc_ref.shape, jnp.float32)


def _softmax_result(l_ref, acc_ref, idx):
    return acc_ref[idx] / jnp.sum(l_ref[idx], axis=1, keepdims=True)


def _mla_attn_kernel(q_ref, k_ref, v_ref, o_ref, m_ref, l_ref, acc_ref, *, t):
    qi = pl.program_id(2)
    _softmax_init(m_ref, l_ref, acc_ref)

    def tile(j, masked):
        ks = pl.multiple_of(j * t, t)
        v = v_ref[pl.ds(ks, t), :]
        for hh in range(2):
            q = q_ref[:, hh * 128:(hh + 1) * 128]
            k = k_ref[pl.ds(ks, t), hh * 128:(hh + 1) * 128]
            s = lax.dot_general(q, k, _NT, preferred_element_type=jnp.float32)
            if masked:
                s = jnp.where(_chunk_mask(t, t), s, NEG_INF)
            _softmax_update(s, v, m_ref, l_ref, acc_ref, hh)

    def body(i, carry):
        tile(2 * i, False)
        tile(2 * i + 1, False)
        return carry

    lax.fori_loop(0, qi // 2, body, 0)

    @pl.when(qi % 2 == 1)
    def _():
        tile(qi - 1, False)

    tile(qi, True)
    lane = lax.broadcasted_iota(jnp.int32, (t, 128), 1)
    o_ref[...] = _bf16(jnp.where(lane < MLA_V, _softmax_result(l_ref, acc_ref, 0),
                                 _softmax_result(l_ref, acc_ref, 1)))


def _mla_attn(q, k, v, B, S, t=256):
    nq = S // t
    return pl.pallas_call(
        functools.partial(_mla_attn_kernel, t=t),
        grid=(B, 2, nq),
        in_specs=[pl.BlockSpec((t, 256), lambda b, hp, i: (b * nq + i, hp)),
                  pl.BlockSpec((S, 256), lambda b, hp, i: (b, hp)),
                  pl.BlockSpec((S, 128), lambda b, hp, i: (b, hp))],
        out_specs=pl.BlockSpec((t, 128), lambda b, hp, i: (b * nq + i, hp)),
        out_shape=jax.ShapeDtypeStruct((B * S, MLA_HEADS * MLA_V), jnp.bfloat16),
        scratch_shapes=[pltpu.VMEM((2, t, LANES), jnp.float32)] * 3,
        compiler_params=_cparams(("parallel", "parallel", "arbitrary")),
        name="mla_attn",
    )(q, k, v)


def _t5_bias(rel, rb_ref, h):
    nb = REL_BUCKETS // 2
    max_exact = nb // 2
    n = jnp.abs(rel)
    nf = jnp.maximum(n, 1).astype(jnp.float32)
    large = max_exact + (jnp.log(nf / max_exact) / math.log(REL_MAX_DIST / max_exact)
                         * (nb - max_exact)).astype(jnp.int32)
    large = jnp.minimum(large, nb - 1)
    low = jnp.where(n < max_exact, n, large)
    neg = jnp.zeros(rel.shape, jnp.float32)
    pos = jnp.zeros(rel.shape, jnp.float32)
    for j in range(nb):
        eq = low == j
        neg = jnp.where(eq, rb_ref[h * REL_BUCKETS + j], neg)
        pos = jnp.where(eq, rb_ref[h * REL_BUCKETS + nb + j], pos)
    return jnp.where(rel > 0, pos, neg)


def _diff_attn_kernel(qmin_ref, kmax_ref, first_ref, consec_ref,
                      rb_ref, q_ref, k_ref, v_ref, posq_ref, posk_ref,
                      lq1_ref, lk1_ref, lq2_ref, lk2_ref, sub_ref,
                      o_ref, m_ref, l_ref, acc_ref, *, t, nq, lam_init):
    b = pl.program_id(0)
    h = pl.program_id(1)
    qi = pl.program_id(2)
    _softmax_init(m_ref, l_ref, acc_ref)

    q = q_ref[...]
    lane = lax.broadcasted_iota(jnp.int32, q.shape, 1)
    zero = jnp.zeros_like(q)
    qs = jnp.concatenate([jnp.where(lane < DIFF_QK, q, zero), jnp.where(lane < DIFF_QK, zero, q)], axis=0)
    tq = b * nq + qi
    qmin = qmin_ref[tq]

    def toeplitz_bias(j):
        d0 = first_ref[b * nq + j] - first_ref[tq]
        x = lax.broadcasted_iota(jnp.int32, (1, 2 * t), 1)
        g = _t5_bias(d0 + jnp.where(x < t, x, x - 2 * t), rb_ref, h)
        g = pltpu.roll(jnp.broadcast_to(g, (t, 2 * t)), 0, 1, stride=1, stride_axis=0)
        return g[:, :t]

    def general_bias(j):
        rel = posk_ref[j] - posq_ref[...]
        return _t5_bias(rel, rb_ref, h)

    def tile(j, masked, bias_fn):
        ks = pl.multiple_of(j * t, t)
        k = k_ref[pl.ds(ks, t), :]
        v = v_ref[pl.ds(ks, t), :]
        s = lax.dot_general(qs, k, _NT, preferred_element_type=jnp.float32)
        if bias_fn is None:
            shift = rb_ref[h * REL_BUCKETS + REL_BUCKETS // 2 - 1] * LOG2E
        else:
            shift = None
            s = s.reshape(2, t, t) + (bias_fn(j) * LOG2E)[None]
            if masked:
                s = jnp.where(_chunk_mask(t, t)[None], s, NEG_INF)
            s = s.reshape(2 * t, t)
        _softmax_update(s, v, m_ref, l_ref, acc_ref, 0, shift)

    def near_tile(j, masked):
        consecutive = jnp.logical_and(consec_ref[tq] == 1, consec_ref[b * nq + j] == 1)

        @pl.when(consecutive)
        def _():
            tile(j, masked, toeplitz_bias)

        @pl.when(jnp.logical_not(consecutive))
        def _():
            tile(j, masked, general_bias)

    def is_far(j):
        return kmax_ref[b * nq + jnp.minimum(j, qi)] - qmin <= -T5_FAR

    def pair_cond(n):
        return jnp.logical_and(2 * n + 1 < qi, jnp.logical_and(is_far(2 * n), is_far(2 * n + 1)))

    def pair_body(n):
        tile(2 * n, False, None)
        tile(2 * n + 1, False, None)
        return n + 1

    def body(j, carry):
        far = is_far(j)

        @pl.when(far)
        def _():
            tile(j, False, None)

        @pl.when(jnp.logical_not(far))
        def _():
            near_tile(j, False)

        return carry

    n_pairs = lax.while_loop(pair_cond, pair_body, 0)
    lax.fori_loop(2 * n_pairs, qi, body, 0)
    near_tile(qi, True)

    f32 = jnp.float32
    lam = (jnp.exp(jnp.sum(lq1_ref[...].astype(f32) * lk1_ref[...].astype(f32), keepdims=True))
           - jnp.exp(jnp.sum(lq2_ref[...].astype(f32) * lk2_ref[...].astype(f32), keepdims=True))
           + lam_init)
    a = _softmax_result(l_ref, acc_ref, 0)
    o = a[:t] - lam * a[t:]
    o = o * lax.rsqrt(jnp.mean(o * o, -1, keepdims=True) + EPS) * sub_ref[...] * (1.0 - lam_init)
    o_ref[...] = _bf16(o)


def _diff_attn(ub, pos_col, pos_tiles, qmin, kmax, first, consec, rb_flat, lq1, lk1, lq2, lk2, subln,
               B, S, lam_init, t=256):
    nq = S // t
    H = DIFF_HEADS
    small = lambda a: pl.BlockSpec(a.shape, lambda b, h, i, *_: (0, 0))
    grid_spec = pltpu.PrefetchScalarGridSpec(
        num_scalar_prefetch=4,
        grid=(B, H, nq),
        in_specs=[pl.BlockSpec(memory_space=pltpu.SMEM),
                  pl.BlockSpec((t, 128), lambda b, h, i, *_: (b * nq + i, h)),
                  pl.BlockSpec((S, 128), lambda b, h, i, *_: (b, H + h)),
                  pl.BlockSpec((S, 128), lambda b, h, i, *_: (b, 2 * H + h)),
                  pl.BlockSpec((t, 1), lambda b, h, i, *_: (b * nq + i, 0)),
                  pl.BlockSpec((nq, 1, t), lambda b, h, i, *_: (b, 0, 0)),
                  small(lq1), small(lk1), small(lq2), small(lk2), small(subln)],
        out_specs=pl.BlockSpec((t, 128), lambda b, h, i, *_: (b * nq + i, h)),
        scratch_shapes=[pltpu.VMEM((1, 2 * t, LANES), jnp.float32)] * 3,
    )
    return pl.pallas_call(
        functools.partial(_diff_attn_kernel, t=t, nq=nq, lam_init=lam_init),
        grid_spec=grid_spec,
        out_shape=jax.ShapeDtypeStruct((B * S, H * DIFF_V), jnp.bfloat16),
        compiler_params=_cparams(("parallel", "parallel", "arbitrary")),
        name="diff_attn",
    )(qmin, kmax, first, consec, rb_flat, ub, ub, ub, pos_col, pos_tiles, lq1, lk1, lq2, lk2, subln)


def _sb_attn_kernel(q_ref, k_ref, v_ref, o_ref, c_ref, acc_ref, *, t):
    qi = pl.program_id(2)
    row = lax.broadcasted_iota(jnp.int32, (t, t), 0)
    col = lax.broadcasted_iota(jnp.int32, (t, t), 1)
    tri = jnp.where(row > col, 1.0, 0.0).astype(jnp.bfloat16)
    strict = (col < row)[None]
    q = q_ref[...]
    lane = lax.broadcasted_iota(jnp.int32, q.shape, 1)
    zero = jnp.zeros_like(q)
    qs = jnp.concatenate([jnp.where(lane < SB_D, q, zero), jnp.where(lane < SB_D, zero, q)], axis=0)
    c_ref[...] = jnp.zeros(c_ref.shape, jnp.float32)
    acc_ref[...] = jnp.zeros(acc_ref.shape, jnp.float32)

    def tile(j, diag):
        ks = pl.multiple_of(j * t, t)
        k = k_ref[pl.ds(ks, t), :]
        v = v_ref[pl.ds(ks, t), :]
        z = lax.dot_general(qs, k, _NT, preferred_element_type=jnp.float32)
        lf = -(jnp.maximum(z, 0.0) + jnp.log(1.0 + jnp.exp(-jnp.abs(z))))
        if diag:
            lf = jnp.where(strict, lf.reshape(2, t, t), 0.0).reshape(2 * t, t)
        hi = _bf16(lf)
        lo = _bf16(lf - hi.astype(jnp.float32))
        c = c_ref[...]
        later = _dot(hi, tri) + _dot(lo, tri) + jnp.concatenate([c] * (t // LANES), axis=1)
        w = jnp.exp(lf + z + later)
        if diag:
            w = jnp.where(strict, w.reshape(2, t, t), 0.0).reshape(2 * t, t)
        acc_ref[...] += _dot(_bf16(w), v)
        c_ref[...] = c + jnp.sum(lf, axis=1, keepdims=True)

    tile(qi, True)

    def cond(carry):
        j, cmax = carry
        return jnp.logical_and(j >= 0, cmax > SB_LOG_ZERO)

    def body(carry):
        j, _ = carry
        tile(j, False)
        return j - 1, jnp.max(c_ref[...])

    lax.while_loop(cond, body, (qi - 1, jnp.max(c_ref[...])))
    acc = acc_ref[...]
    o_ref[...] = _bf16(jnp.where(lane < SB_D, acc[:t], acc[t:]))


def _sb_attn(ub, B, S, t=256):
    nq = S // t
    c0 = DIFF_IN // 128
    return pl.pallas_call(
        functools.partial(_sb_attn_kernel, t=t),
        grid=(B, 2, nq),
        in_specs=[pl.BlockSpec((t, 128), lambda b, hp, i: (b * nq + i, c0 + hp)),
                  pl.BlockSpec((S, 128), lambda b, hp, i: (b, c0 + 2 + hp)),
                  pl.BlockSpec((S, 128), lambda b, hp, i: (b, c0 + 4 + hp))],
        out_specs=pl.BlockSpec((t, 128), lambda b, hp, i: (b * nq + i, hp)),
        out_shape=jax.ShapeDtypeStruct((B * S, SB_HEADS * SB_D), jnp.bfloat16),
        scratch_shapes=[pltpu.VMEM((2 * t, LANES), jnp.float32)] * 2,
        compiler_params=_cparams(("parallel", "parallel", "arbitrary")),
        name="sb_attn",
    )(ub, ub, ub)


def _layer_norm(x, g, b):
    mu = jnp.mean(x, -1, keepdims=True)
    xc = x - mu
    var = jnp.mean(xc * xc, -1, keepdims=True)
    return xc * lax.rsqrt(var + EPS) * g + b


def _out_proj_kernel(h_ref, ym_ref, yd_ref, ys_ref, wm_ref, wd_ref, ws_ref, g_ref, b_ref, o_ref, *, alpha):
    mix = _dot(ym_ref[...], wm_ref[...]) + _dot(yd_ref[...], wd_ref[...]) + _dot(ys_ref[...], ws_ref[...])
    o_ref[...] = _layer_norm(alpha * h_ref[...] + mix, g_ref[...], b_ref[...])


def _out_proj(h2, ym, yd, ys, wm, wd, ws, g, b, alpha, tm=512):
    T = h2.shape[0]
    row = lambda a: pl.BlockSpec((tm, a.shape[1]), lambda i: (i, 0))
    full = lambda a: pl.BlockSpec(a.shape, lambda i: (0, 0))
    return pl.pallas_call(
        functools.partial(_out_proj_kernel, alpha=alpha),
        grid=(T // tm,),
        in_specs=[row(h2), row(ym), row(yd), row(ys), full(wm), full(wd), full(ws), full(g), full(b)],
        out_specs=pl.BlockSpec((tm, D_MODEL), lambda i: (i, 0)),
        out_shape=jax.ShapeDtypeStruct((T, D_MODEL), jnp.float32),
        compiler_params=_cparams(("parallel",)),
        name="out_proj_ln",
    )(h2, ym, yd, ys, wm, wd, ws, g, b)


def _first_max(vals):
    m = vals[0]
    for v in vals[1:]:
        m = jnp.maximum(m, v)
    taken = jnp.zeros(m.shape, jnp.bool_)
    hot = []
    for v in vals:
        is_first = jnp.logical_and(v == m, jnp.logical_not(taken))
        hot.append(is_first)
        taken = jnp.logical_or(taken, is_first)
    return m, hot


def _router_kernel(h_ref, rw_ref, rb_ref, gate_ref, slot_ref, dest_ref, chosen_ref, first_ref):
    logits = jnp.dot(h_ref[...], rw_ref[...], preferred_element_type=jnp.float32,
                     precision=lax.Precision.HIGHEST).T[:N_EXPERTS]
    scores = jax.nn.sigmoid(logits)
    sel = scores + rb_ref[...]
    ninf = -jnp.inf
    group_score, first, second = [], [], []
    for g in range(N_GROUPS):
        vals = [sel[g * EXPERTS_PER_GROUP + k:g * EXPERTS_PER_GROUP + k + 1, :] for k in range(EXPERTS_PER_GROUP)]
        m1, hot1 = _first_max(vals)
        m2, hot2 = _first_max([jnp.where(hh, ninf, v) for hh, v in zip(hot1, vals)])
        group_score.append(m1 + m2)
        first.append(hot1)
        second.append(hot2)
    _, best = _first_max(group_score)
    w1 = jnp.zeros_like(group_score[0])
    w2 = jnp.zeros_like(group_score[0])
    for g in range(N_GROUPS):
        for k in range(EXPERTS_PER_GROUP):
            e = g * EXPERTS_PER_GROUP + k
            sc = scores[e:e + 1, :]
            w1 = jnp.where(jnp.logical_and(best[g], first[g][k]), sc, w1)
            w2 = jnp.where(jnp.logical_and(best[g], second[g][k]), sc, w2)
    tot = w1 + w2
    for g in range(N_GROUPS):
        for k in range(EXPERTS_PER_GROUP):
            e = g * EXPERTS_PER_GROUP + k
            is1 = jnp.logical_and(best[g], first[g][k])
            is2 = jnp.logical_and(best[g], second[g][k])
            gate_ref[e:e + 1, :] = jnp.where(is1, w1 / tot, 0.0) + jnp.where(is2, w2 / tot, 0.0)
            chosen_ref[e:e + 1, :] = jnp.where(jnp.logical_or(is1, is2), 1.0, 0.0)
            first_ref[e:e + 1, :] = jnp.where(is1, 1.0, 0.0)
    chosen = chosen_ref[...]
    tm = chosen.shape[1]
    before = (lax.broadcasted_iota(jnp.int32, (tm, tm), 0) < lax.broadcasted_iota(jnp.int32, (tm, tm), 1))
    rank = _dot(_bf16(chosen), jnp.where(before, 1.0, 0.0).astype(jnp.bfloat16)).astype(jnp.int32)
    slot_ref[...] = jnp.where(chosen > 0.0, rank, -1)
    expert = lax.broadcasted_iota(jnp.int32, chosen.shape, 0)
    parked = jnp.logical_and(chosen > 0.0, rank < MOE_SLOTS)
    code = jnp.where(parked, expert * MOE_SLOTS + rank + 1, 0)
    is_first = first_ref[...] > 0.0
    dest_ref[0:1, :] = jnp.sum(jnp.where(is_first, code, 0), axis=0, keepdims=True) - 1
    dest_ref[1:2, :] = jnp.sum(jnp.where(is_first, 0, code), axis=0, keepdims=True) - 1


def _router(h2, rw_pad, rb_col):
    T = h2.shape[0]
    tm = MOE_SUB
    tok = lambda rows: pl.BlockSpec((rows, tm), lambda i: (0, i))
    return pl.pallas_call(
        _router_kernel,
        grid=(T // tm,),
        in_specs=[pl.BlockSpec((tm, D_MODEL), lambda i: (i, 0)),
                  pl.BlockSpec(rw_pad.shape, lambda i: (0, 0)),
                  pl.BlockSpec(rb_col.shape, lambda i: (0, 0))],
        out_specs=[tok(N_EXPERTS), tok(N_EXPERTS), tok(2)],
        out_shape=[jax.ShapeDtypeStruct((N_EXPERTS, T), jnp.float32),
                   jax.ShapeDtypeStruct((N_EXPERTS, T), jnp.int32),
                   jax.ShapeDtypeStruct((2, T), jnp.int32)],
        scratch_shapes=[pltpu.VMEM((N_EXPERTS, tm), jnp.float32)] * 2,
        compiler_params=_cparams(("parallel",)),
        name="router",
    )(h2, rw_pad, rb_col)


def _moe_kernel(npass_ref, h_ref, slot_ref, slot_t_ref, dest_t_ref, gate_ref, p_ref, wg_ref, wu_ref, wd_ref,
                pg_ref, pp_ref, g_ref, b_ref, o_ref, over_ref, xb_ref, y_ref, *, alpha, n_sub):
    i = pl.program_id(0)
    e = pl.program_id(1)
    R, SUB = MOE_SLOTS, MOE_SUB

    @pl.when(e == 0)
    def _():
        xb_ref[...] = _bf16(h_ref[...])
        over_ref[...] = jnp.zeros(over_ref.shape, jnp.float32)

    slot_row = slot_ref[pl.ds(e, 1), :]
    gate_row = gate_ref[pl.ds(e, 1), :]

    def expert_pass(c):
        xs, gs = [], []
        for j in range(n_sub):
            sl = slot_row[:, j * SUB:(j + 1) * SUB] - c * R
            hit = lax.broadcasted_iota(jnp.int32, (R, SUB), 0) == sl
            onehot = jnp.where(hit, 1.0, 0.0).astype(jnp.bfloat16)
            xs.append(_bf16(_dot(onehot, xb_ref[j * SUB:(j + 1) * SUB, :])))
            gs.append(jnp.sum(jnp.where(hit, gate_row[:, j * SUB:(j + 1) * SUB], 0.0), axis=1, keepdims=True))
        xe = jnp.concatenate(xs, axis=0)
        hid = jax.nn.silu(_dot(xe, wg_ref[0])) * _dot(xe, wu_ref[0])
        y = _dot(_bf16(hid), wd_ref[0])
        return [_bf16(y[j * R:(j + 1) * R] * gs[j]) for j in range(n_sub)]

    ys = expert_pass(0)
    for j in range(n_sub):
        y_ref[j, pl.ds(pl.multiple_of(e * R, 16), R), :] = ys[j]

    def overflow_pass(c, carry):
        ys = expert_pass(c)
        lane16 = lax.broadcasted_iota(jnp.int32, (SUB, N_EXPERTS), 1)
        for j in range(n_sub):
            rows = slice(j * SUB, (j + 1) * SUB)
            slot_col = jnp.sum(jnp.where(lane16 == e, slot_t_ref[rows, :], 0), axis=1, keepdims=True)
            hit_t = lax.broadcasted_iota(jnp.int32, (SUB, R), 1) == slot_col - c * R
            over_ref[rows, :] += _dot(jnp.where(hit_t, 1.0, 0.0).astype(jnp.bfloat16), ys[j])
        return carry

    lax.fori_loop(1, npass_ref[i * N_EXPERTS + e], overflow_pass, 0)

    @pl.when(e == N_EXPERTS - 1)
    def _():
        lane = lax.broadcasted_iota(jnp.int32, (SUB, N_EXPERTS * R), 1)
        for j in range(n_sub):
            rows = slice(j * SUB, (j + 1) * SUB)
            dest = dest_t_ref[rows, :]
            hit = jnp.logical_or(lane == dest[:, 0:1], lane == dest[:, 1:2])
            scatter = jnp.where(hit, 1.0, 0.0).astype(jnp.bfloat16)
            ffn = _dot(scatter, y_ref[j]) + over_ref[rows, :]
            h = h_ref[rows, :]
            ple = jax.nn.sigmoid(_dot(xb_ref[rows, :], pg_ref[...])) * _dot(_bf16(p_ref[rows, :]), pp_ref[...])
            o_ref[rows, :] = _layer_norm(alpha * h + ffn + ple, g_ref[...], b_ref[...])


def _moe(h2, slot, dest, gate, p2, wg, wu, wd, pg, pp, g, b, alpha, tm=1024):
    T = h2.shape[0]
    n_sub = tm // MOE_SUB
    n_tiles = T // tm
    count = jnp.sum((slot >= 0).reshape(N_EXPERTS, n_tiles, n_sub, MOE_SUB), axis=3)
    npass = jnp.maximum(1, (jnp.max(count, axis=2) + MOE_SLOTS - 1) // MOE_SLOTS).T.reshape(-1).astype(jnp.int32)
    full = lambda a: pl.BlockSpec(a.shape, lambda i, e, *_: (0, 0))
    grid_spec = pltpu.PrefetchScalarGridSpec(
        num_scalar_prefetch=1,
        grid=(n_tiles, N_EXPERTS),
        in_specs=[pl.BlockSpec((tm, D_MODEL), lambda i, e, *_: (i, 0)),
                  pl.BlockSpec((N_EXPERTS, tm), lambda i, e, *_: (0, i)),
                  pl.BlockSpec((tm, N_EXPERTS), lambda i, e, *_: (i, 0)),
                  pl.BlockSpec((tm, 2), lambda i, e, *_: (i, 0)),
                  pl.BlockSpec((N_EXPERTS, tm), lambda i, e, *_: (0, i)),
                  pl.BlockSpec((tm, PLE_DIM), lambda i, e, *_: (i, 0)),
                  pl.BlockSpec((1, D_MODEL, D_EXPERT), lambda i, e, *_: (e, 0, 0)),
                  pl.BlockSpec((1, D_MODEL, D_EXPERT), lambda i, e, *_: (e, 0, 0)),
                  pl.BlockSpec((1, D_EXPERT, D_MODEL), lambda i, e, *_: (e, 0, 0)),
                  full(pg), full(pp), full(g), full(b)],
        out_specs=pl.BlockSpec((tm, D_MODEL), lambda i, e, *_: (i, 0)),
        scratch_shapes=[pltpu.VMEM((tm, D_MODEL), jnp.float32),
                        pltpu.VMEM((tm, D_MODEL), jnp.bfloat16),
                        pltpu.VMEM((n_sub, N_EXPERTS * MOE_SLOTS, D_MODEL), jnp.bfloat16)],
    )
    return pl.pallas_call(
        functools.partial(_moe_kernel, alpha=alpha, n_sub=n_sub),
        grid_spec=grid_spec,
        out_shape=jax.ShapeDtypeStruct((T, D_MODEL), jnp.float32),
        compiler_params=_cparams(("parallel", "arbitrary")),
        name="moe_ple_ln",
    )(npass, h2, slot, slot.T, dest.T, gate, p2, wg, wu, wd, pg, pp, g, b)


def _rotate_half_cols(w):
    half = w.shape[1] // 2
    return jnp.concatenate([-w[:, half:], w[:, :half]], axis=1)


def _in_proj_weights(w_in):
    z = lambda n: jnp.zeros((D_MODEL, n), w_in.dtype)
    cq = w_in[:, :MLA_Q_RANK]
    ckv = w_in[:, MLA_Q_RANK:MLA_Q_RANK + MLA_KV_RANK]
    kr = w_in[:, MLA_Q_RANK + MLA_KV_RANK:MLA_IN]
    n_dq = DIFF_HEADS * 2 * DIFF_QK
    sb0 = MLA_IN + DIFF_IN
    n_sq = SB_HEADS * SB_D
    cols = [ckv,
            z(MLA_NOPE), kr, z(128 - MLA_NOPE - MLA_ROPE),
            z(MLA_NOPE), _rotate_half_cols(kr), z(128 - MLA_NOPE - MLA_ROPE),
            cq, z(256 - MLA_Q_RANK),
            w_in[:, MLA_IN:MLA_IN + n_dq] * (DIFF_QK ** -0.5 * LOG2E),
            w_in[:, MLA_IN + n_dq:sb0],
            w_in[:, sb0:sb0 + n_sq] * (SB_D ** -0.5),
            w_in[:, sb0 + n_sq:]]
    return _bf16(jnp.concatenate(cols, axis=1))


def _mla_up_weights(w_uq, w_ukv):
    dq = MLA_NOPE + MLA_ROPE
    zq = lambda n: jnp.zeros((MLA_Q_RANK, n), w_uq.dtype)
    plain, rot = [], []
    for h in range(MLA_HEADS):
        wh = w_uq[:, h * dq:(h + 1) * dq]
        plain += [wh, zq(128 - dq)]
        rot += [zq(MLA_NOPE), _rotate_half_cols(wh[:, MLA_NOPE:]), zq(128 - dq)]
    wq = jnp.concatenate(plain + rot, axis=1)
    wq = jnp.concatenate([wq, jnp.zeros((256 - MLA_Q_RANK, wq.shape[1]), wq.dtype)], axis=0)
    dkv = MLA_NOPE + MLA_V
    zk = lambda n: jnp.zeros((MLA_KV_RANK, n), w_ukv.dtype)
    kcols, vcols = [], []
    for h in range(MLA_HEADS):
        wh = w_ukv[:, h * dkv:(h + 1) * dkv]
        kcols += [wh[:, :MLA_NOPE], zk(128 - MLA_NOPE)]
        vcols += [wh[:, MLA_NOPE:]]
    wkv = jnp.concatenate(kcols + vcols, axis=1)
    return _bf16(wq), _bf16(wkv)


def _attn_tiles(S):
    pick = lambda want: max(c for c in (128, 256, 512, 1024) if c <= want and S % c == 0)
    return pick(512), pick(512), pick(256)


def kernel(x, p, positions, w_in, mla_q_norm, mla_w_uq, mla_kv_norm, mla_w_ukv, diff_lambda_q1, diff_lambda_k1, diff_lambda_q2, diff_lambda_k2, diff_subln, rel_bias, w_o, ln1_g, ln1_b, router_w, router_b, w_gate, w_up, w_down, ple_proj, ple_gate, ln2_g, ln2_b):
    B, S, _ = x.shape
    depth = w_in.shape[0]
    T = B * S
    alpha = (2 * depth) ** 0.25
    t_mla, t_diff, t_sb = _attn_tiles(S)
    nq = S // t_diff

    pos_col = positions.reshape(T, 1)
    pos_tiles = positions.reshape(B * nq, 1, t_diff)
    tile_pos = positions.reshape(B * nq, t_diff)
    qmin = jnp.min(tile_pos, axis=1)
    kmax = jnp.max(tile_pos, axis=1)
    first = tile_pos[:, 0]
    consec = jnp.all(tile_pos[:, 1:] - tile_pos[:, :-1] == 1, axis=1).astype(jnp.int32)
    rb_flat = rel_bias.T.reshape(-1).astype(jnp.float32)

    half = MLA_ROPE // 2
    inv = ROPE_THETA ** (-jnp.arange(half, dtype=jnp.float32) / half)
    cos_t, sin_t = _rope_table(positions.reshape(1, T), jnp.concatenate([inv, inv]).reshape(MLA_ROPE, 1))
    pad_l = lambda v: jnp.full((T, MLA_NOPE), v, jnp.float32)
    pad_r = jnp.zeros((T, LANES - MLA_NOPE - MLA_ROPE), jnp.float32)
    cosf = jnp.concatenate([pad_l(1.0), cos_t.T, pad_r], axis=1)
    sinf = jnp.concatenate([pad_l(0.0), sin_t.T, pad_r], axis=1)

    rw_pad = jnp.pad(router_w.astype(jnp.float32), ((0, 0), (0, LANES - N_EXPERTS)))
    rb_col = router_b.reshape(N_EXPERTS, 1).astype(jnp.float32)
    row = lambda a: a.reshape(1, -1)

    h = x.reshape(T, D_MODEL)
    for i in range(depth):
        lam_init = 0.8 - 0.6 * math.exp(-0.3 * i)
        ua, ub = _in_proj(h, _in_proj_weights(w_in[i]))
        wq, wkv = _mla_up_weights(mla_w_uq[i], mla_w_ukv[i])
        gq = jnp.concatenate([mla_q_norm[i], jnp.zeros((256 - MLA_Q_RANK,), mla_q_norm.dtype)]).reshape(1, 256)
        q_m, k_m, v_m = _mla_prep(ua, cosf, sinf, gq, row(mla_kv_norm[i]), wq, wkv)
        y_mla = _mla_attn(q_m, k_m, v_m, B, S, t=t_mla)
        y_diff = _diff_attn(ub, pos_col, pos_tiles, qmin, kmax, first, consec, rb_flat,
                            row(diff_lambda_q1[i]), row(diff_lambda_k1[i]),
                            row(diff_lambda_q2[i]), row(diff_lambda_k2[i]), row(diff_subln[i]),
                            B, S, lam_init, t=t_diff)
        y_sb = _sb_attn(ub, B, S, t=t_sb)
        wo = _bf16(w_o[i])
        n_m, n_d = MLA_HEADS * MLA_V, DIFF_HEADS * DIFF_V
        h = _out_proj(h, y_mla, y_diff, y_sb, wo[:n_m], wo[n_m:n_m + n_d], wo[n_m + n_d:],
                      row(ln1_g[i]), row(ln1_b[i]), alpha)
        gate, slot, dest = _router(h, rw_pad, rb_col)
        h = _moe(h, slot, dest, gate, p[i].reshape(T, PLE_DIM), _bf16(w_gate[i]), _bf16(w_up[i]), _bf16(w_down[i]),
                 _bf16(ple_gate[i]), _bf16(ple_proj[i]), row(ln2_g[i]), row(ln2_b[i]), alpha)
    return h.reshape(B, S, D_MODEL)
```

```python
import functools
import math

import jax
import jax.numpy as jnp
from jax import lax
from jax.experimental import pallas as pl
from jax.experimental.pallas import tpu as pltpu

D_MODEL = 1024
CHUNK = 64
PLE_DIM = 256
MLA_HEADS = 4
MLA_NOPE = 64
MLA_ROPE = 32
MLA_V = 64
MLA_Q_RANK = 192
MLA_KV_RANK = 128
ROPE_THETA = 10000.0
DIFF_HEADS = 4
DIFF_QK = 64
DIFF_V = 2 * DIFF_QK
SB_HEADS = 4
SB_D = 64
REL_BUCKETS = 32
REL_MAX_DIST = 128
N_EXPERTS = 16
N_GROUPS = 4
EXPERTS_PER_GROUP = N_EXPERTS // N_GROUPS
D_EXPERT = 512
MLA_IN = MLA_Q_RANK + MLA_KV_RANK + MLA_ROPE
DIFF_IN = 2 * DIFF_HEADS * 2 * DIFF_QK + DIFF_HEADS * DIFF_V
SB_IN = 3 * SB_HEADS * SB_D
EPS = 1e-5
NEG_INF = -1e30

LANES = 128
VMEM_LIMIT = 56 * 1024 * 1024

UA_CKV = 0
UA_KR = 128
UA_KRS = 256
UA_CQ = 384
UA_W = 640
UB_W = DIFF_IN + SB_IN

T5_FAR = 91
SB_LOG_ZERO = -88.0
MOE_SUB = 512
MOE_SLOTS = 96
MOE_EXPERTS_PER_STEP = 2

_NT = (((1,), (1,)), ((), ()))
LOG2E = math.log2(math.e)


def _cparams(sem):
    return pltpu.CompilerParams(dimension_semantics=sem, vmem_limit_bytes=VMEM_LIMIT)


def _bf16(a):
    return a.astype(jnp.bfloat16)


def _dot(a, b):
    return jnp.dot(a, b, preferred_element_type=jnp.float32)


def _in_proj_kernel(x_ref, w_ref, ua_ref, ub_ref):
    x = _bf16(x_ref[...])
    step = 512
    for c in range(0, UA_W, step):
        e = min(c + step, UA_W)
        ua_ref[:, c:e] = _dot(x, w_ref[:, c:e])
    for c in range(0, UB_W, step):
        e = min(c + step, UB_W)
        ub_ref[:, c:e] = _bf16(_dot(x, w_ref[:, UA_W + c:UA_W + e]))


def _in_proj(h2, w_all, tm=512):
    T = h2.shape[0]
    return pl.pallas_call(
        _in_proj_kernel,
        grid=(T // tm,),
        in_specs=[pl.BlockSpec((tm, D_MODEL), lambda i: (i, 0)),
                  pl.BlockSpec((D_MODEL, UA_W + UB_W), lambda i: (0, 0))],
        out_specs=[pl.BlockSpec((tm, UA_W), lambda i: (i, 0)),
                   pl.BlockSpec((tm, UB_W), lambda i: (i, 0))],
        out_shape=[jax.ShapeDtypeStruct((T, UA_W), jnp.float32),
                   jax.ShapeDtypeStruct((T, UB_W), jnp.bfloat16)],
        compiler_params=_cparams(("parallel",)),
        name="in_proj",
    )(h2, w_all)


def _rope_table_kernel(pos_ref, inv_ref, cos_ref, sin_ref):
    ang = pos_ref[...].astype(jnp.float32) * inv_ref[...]
    cos_ref[...] = jnp.cos(ang)
    sin_ref[...] = jnp.sin(ang)


def _rope_table(pos_row, inv_col, tn=2048):
    T = pos_row.shape[1]
    n = inv_col.shape[0]
    tn = min(tn, T)
    return pl.pallas_call(
        _rope_table_kernel,
        grid=(T // tn,),
        in_specs=[pl.BlockSpec((1, tn), lambda i: (0, i)),
                  pl.BlockSpec((n, 1), lambda i: (0, 0))],
        out_specs=[pl.BlockSpec((n, tn), lambda i: (0, i))] * 2,
        out_shape=[jax.ShapeDtypeStruct((n, T), jnp.float32)] * 2,
        compiler_params=_cparams(("parallel",)),
        name="rope_table",
    )(pos_row, inv_col)


def _mla_prep_kernel(ua_ref, cos_ref, sin_ref, gq_ref, gkv_ref, wq_ref, wkv_ref,
                     q_ref, k_ref, v_ref):
    ckv = ua_ref[:, UA_CKV:UA_CKV + 128]
    kr = ua_ref[:, UA_KR:UA_KR + 128]
    krs = ua_ref[:, UA_KRS:UA_KRS + 128]
    cq = ua_ref[:, UA_CQ:UA_CQ + 256]
    cqn = cq * lax.rsqrt(jnp.sum(cq * cq, -1, keepdims=True) * (1.0 / MLA_Q_RANK) + EPS) * gq_ref[...]
    ckvn = ckv * lax.rsqrt(jnp.sum(ckv * ckv, -1, keepdims=True) * (1.0 / MLA_KV_RANK) + EPS) * gkv_ref[...]
    cosf = cos_ref[...]
    sinf = sin_ref[...]
    scale = (MLA_NOPE + MLA_ROPE) ** -0.5 * LOG2E
    cqb = _bf16(cqn)
    ckvb = _bf16(ckvn)
    k_rope = kr * cosf + krs * sinf
    for h in range(MLA_HEADS):
        a = _dot(cqb, wq_ref[:, h * 128:(h + 1) * 128])
        b = _dot(cqb, wq_ref[:, 512 + h * 128:512 + (h + 1) * 128])
        q_ref[:, h * 128:(h + 1) * 128] = _bf16((a * cosf + b * sinf) * scale)
        kn = _dot(ckvb, wkv_ref[:, h * 128:(h + 1) * 128])
        k_ref[:, h * 128:(h + 1) * 128] = _bf16(kn + k_rope)
    v_ref[...] = _bf16(_dot(ckvb, wkv_ref[:, 512:768]))


def _mla_prep(ua, cosf, sinf, gq, gkv, wq, wkv, tm=512):
    T = ua.shape[0]
    full = lambda a: pl.BlockSpec(a.shape, lambda i: (0, 0))
    return pl.pallas_call(
        _mla_prep_kernel,
        grid=(T // tm,),
        in_specs=[pl.BlockSpec((tm, UA_W), lambda i: (i, 0)),
                  pl.BlockSpec((tm, LANES), lambda i: (i, 0)),
                  pl.BlockSpec((tm, LANES), lambda i: (i, 0)),
                  full(gq), full(gkv), full(wq), full(wkv)],
        out_specs=[pl.BlockSpec((tm, 512), lambda i: (i, 0)),
                   pl.BlockSpec((tm, 512), lambda i: (i, 0)),
                   pl.BlockSpec((tm, 256), lambda i: (i, 0))],
        out_shape=[jax.ShapeDtypeStruct((T, 512), jnp.bfloat16),
                   jax.ShapeDtypeStruct((T, 512), jnp.bfloat16),
                   jax.ShapeDtypeStruct((T, 256), jnp.bfloat16)],
        compiler_params=_cparams(("parallel",)),
        name="mla_prep",
    )(ua, cosf, sinf, gq, gkv, wq, wkv)


def _chunk_mask(tq, tk):
    qc = lax.broadcasted_iota(jnp.int32, (tq, tk), 0) // CHUNK
    kc = lax.broadcasted_iota(jnp.int32, (tq, tk), 1) // CHUNK
    return kc <= qc


def _softmax_update(s, v, m_ref, l_ref, acc_ref, idx, shift=None):
    tk = s.shape[1]
    m_old = m_ref[idx]
    row_max = jnp.max(s, axis=1, keepdims=True)
    if shift is not None:
        row_max = row_max + shift
    m_new = jnp.maximum(m_old, row_max)
    alpha = jnp.exp2(m_old - m_new)
    m_sub = m_new if shift is None else m_new - shift
    p = jnp.exp2(s - jnp.concatenate([m_sub] * (tk // LANES), axis=1))
    psum = p[:, :LANES]
    for c in range(LANES, tk, LANES):
        psum = psum + p[:, c:c + LANES]
    l_ref[idx] = alpha * l_ref[idx] + psum
    acc_ref[idx] = alpha * acc_ref[idx] + _dot(_bf16(p), v)
    m_ref[idx] = m_new


def _softmax_init(m_ref, l_ref, acc_ref):
    m_ref[...] = jnp.full(m_ref.shape, NEG_INF, jnp.float32)
    l_ref[...] = jnp.zeros(l_ref.shape, jnp.float32)
    acc_ref[...] = jnp.zeros(acc_ref.shape, jnp.float32)


def _softmax_result(l_ref, acc_ref, idx):
    return acc_ref[idx] / jnp.sum(l_ref[idx], axis=1, keepdims=True)


def _mla_attn_kernel(q_ref, k_ref, v_ref, o_ref, m_ref, l_ref, acc_ref, *, t):
    qi = pl.program_id(2)
    _softmax_init(m_ref, l_ref, acc_ref)

    def tile(j, masked, width=1):
        ks = pl.multiple_of(j * t, t)
        v = v_ref[pl.ds(ks, width * t), :]
        for hh in range(2):
            q = q_ref[:, hh * 128:(hh + 1) * 128]
            k = k_ref[pl.ds(ks, width * t), hh * 128:(hh + 1) * 128]
            s = lax.dot_general(q, k, _NT, preferred_element_type=jnp.float32)
            if masked:
                s = jnp.where(_chunk_mask(t, t), s, NEG_INF)
            _softmax_update(s, v, m_ref, l_ref, acc_ref, hh)

    def body(i, carry):
        tile(2 * i, False, width=2)
        return carry

    lax.fori_loop(0, qi // 2, body, 0)

    @pl.when(qi % 2 == 1)
    def _():
        tile(qi - 1, False)

    tile(qi, True)
    lane = lax.broadcasted_iota(jnp.int32, (t, 128), 1)
    o_ref[...] = _bf16(jnp.where(lane < MLA_V, _softmax_result(l_ref, acc_ref, 0),
                                 _softmax_result(l_ref, acc_ref, 1)))


def _mla_attn(q, k, v, B, S, t=256):
    nq = S // t
    return pl.pallas_call(
        functools.partial(_mla_attn_kernel, t=t),
        grid=(B, 2, nq),
        in_specs=[pl.BlockSpec((t, 256), lambda b, hp, i: (b * nq + i, hp)),
                  pl.BlockSpec((S, 256), lambda b, hp, i: (b, hp)),
                  pl.BlockSpec((S, 128), lambda b, hp, i: (b, hp))],
        out_specs=pl.BlockSpec((t, 128), lambda b, hp, i: (b * nq + i, hp)),
        out_shape=jax.ShapeDtypeStruct((B * S, MLA_HEADS * MLA_V), jnp.bfloat16),
        scratch_shapes=[pltpu.VMEM((2, t, LANES), jnp.float32)] * 3,
        compiler_params=_cparams(("parallel", "parallel", "arbitrary")),
        name="mla_attn",
    )(q, k, v)


def _t5_bias(rel, rb_ref, h):
    nb = REL_BUCKETS // 2
    max_exact = nb // 2
    n = jnp.abs(rel)
    nf = jnp.maximum(n, 1).astype(jnp.float32)
    large = max_exact + (jnp.log(nf / max_exact) / math.log(REL_MAX_DIST / max_exact)
                         * (nb - max_exact)).astype(jnp.int32)
    large = jnp.minimum(large, nb - 1)
    low = jnp.where(n < max_exact, n, large)
    neg = jnp.zeros(rel.shape, jnp.float32)
    pos = jnp.zeros(rel.shape, jnp.float32)
    for j in range(nb):
        eq = low == j
        neg = jnp.where(eq, rb_ref[h * REL_BUCKETS + j], neg)
        pos = jnp.where(eq, rb_ref[h * REL_BUCKETS + nb + j], pos)
    return jnp.where(rel > 0, pos, neg)


def _diff_attn_kernel(qmin_ref, kmax_ref, first_ref, consec_ref,
                      rb_ref, q_ref, k_ref, v_ref, posq_ref, posk_ref,
                      lq1_ref, lk1_ref, lq2_ref, lk2_ref, sub_ref,
                      o_ref, m_ref, l_ref, acc_ref, *, t, nq, lam_init):
    b = pl.program_id(0)
    h = pl.program_id(1)
    qi = pl.program_id(2)
    _softmax_init(m_ref, l_ref, acc_ref)

    q = q_ref[...]
    lane = lax.broadcasted_iota(jnp.int32, q.shape, 1)
    zero = jnp.zeros_like(q)
    qs = jnp.concatenate([jnp.where(lane < DIFF_QK, q, zero), jnp.where(lane < DIFF_QK, zero, q)], axis=0)
    tq = b * nq + qi
    qmin = qmin_ref[tq]

    def toeplitz_bias(j):
        d0 = first_ref[b * nq + j] - first_ref[tq]
        x = lax.broadcasted_iota(jnp.int32, (1, 2 * t), 1)
        g = _t5_bias(d0 + jnp.where(x < t, x, x - 2 * t), rb_ref, h)
        g = pltpu.roll(jnp.broadcast_to(g, (t, 2 * t)), 0, 1, stride=1, stride_axis=0)
        return g[:, :t]

    def general_bias(j):
        rel = posk_ref[j] - posq_ref[...]
        return _t5_bias(rel, rb_ref, h)

    def tile(j, masked, bias_fn, width=1):
        ks = pl.multiple_of(j * t, t)
        k = k_ref[pl.ds(ks, width * t), :]
        v = v_ref[pl.ds(ks, width * t), :]
        s = lax.dot_general(qs, k, _NT, preferred_element_type=jnp.float32)
        if bias_fn is None:
            shift = rb_ref[h * REL_BUCKETS + REL_BUCKETS // 2 - 1] * LOG2E
        else:
            shift = None
            s = s.reshape(2, t, t) + (bias_fn(j) * LOG2E)[None]
            if masked:
                s = jnp.where(_chunk_mask(t, t)[None], s, NEG_INF)
            s = s.reshape(2 * t, t)
        _softmax_update(s, v, m_ref, l_ref, acc_ref, 0, shift)

    def near_tile(j, masked):
        consecutive = jnp.logical_and(consec_ref[tq] == 1, consec_ref[b * nq + j] == 1)

        @pl.when(consecutive)
        def _():
            tile(j, masked, toeplitz_bias)

        @pl.when(jnp.logical_not(consecutive))
        def _():
            tile(j, masked, general_bias)

    def is_far(j):
        return kmax_ref[b * nq + jnp.minimum(j, qi)] - qmin <= -T5_FAR

    def pair_cond(n):
        return jnp.logical_and(2 * n + 1 < qi, jnp.logical_and(is_far(2 * n), is_far(2 * n + 1)))

    def pair_body(n):
        tile(2 * n, False, None, width=2)
        return n + 1

    def body(j, carry):
        far = is_far(j)

        @pl.when(far)
        def _():
            tile(j, False, None)

        @pl.when(jnp.logical_not(far))
        def _():
            near_tile(j, False)

        return carry

    n_pairs = lax.while_loop(pair_cond, pair_body, 0)
    lax.fori_loop(2 * n_pairs, qi, body, 0)
    near_tile(qi, True)

    f32 = jnp.float32
    lam = (jnp.exp(jnp.sum(lq1_ref[...].astype(f32) * lk1_ref[...].astype(f32), keepdims=True))
           - jnp.exp(jnp.sum(lq2_ref[...].astype(f32) * lk2_ref[...].astype(f32), keepdims=True))
           + lam_init)
    a = _softmax_result(l_ref, acc_ref, 0)
    o = a[:t] - lam * a[t:]
    o = o * lax.rsqrt(jnp.mean(o * o, -1, keepdims=True) + EPS) * sub_ref[...] * (1.0 - lam_init)
    o_ref[...] = _bf16(o)


def _diff_attn(ub, pos_col, pos_tiles, qmin, kmax, first, consec, rb_flat, lq1, lk1, lq2, lk2, subln,
               B, S, lam_init, t=256):
    nq = S // t
    H = DIFF_HEADS
    small = lambda a: pl.BlockSpec(a.shape, lambda b, h, i, *_: (0, 0))
    grid_spec = pltpu.PrefetchScalarGridSpec(
        num_scalar_prefetch=4,
        grid=(B, H, nq),
        in_specs=[pl.BlockSpec(memory_space=pltpu.SMEM),
                  pl.BlockSpec((t, 128), lambda b, h, i, *_: (b * nq + i, h)),
                  pl.BlockSpec((S, 128), lambda b, h, i, *_: (b, H + h)),
                  pl.BlockSpec((S, 128), lambda b, h, i, *_: (b, 2 * H + h)),
                  pl.BlockSpec((t, 1), lambda b, h, i, *_: (b * nq + i, 0)),
                  pl.BlockSpec((nq, 1, t), lambda b, h, i, *_: (b, 0, 0)),
                  small(lq1), small(lk1), small(lq2), small(lk2), small(subln)],
        out_specs=pl.BlockSpec((t, 128), lambda b, h, i, *_: (b * nq + i, h)),
        scratch_shapes=[pltpu.VMEM((1, 2 * t, LANES), jnp.float32)] * 3,
    )
    return pl.pallas_call(
        functools.partial(_diff_attn_kernel, t=t, nq=nq, lam_init=lam_init),
        grid_spec=grid_spec,
        out_shape=jax.ShapeDtypeStruct((B * S, H * DIFF_V), jnp.bfloat16),
        compiler_params=_cparams(("parallel", "parallel", "arbitrary")),
        name="diff_attn",
    )(qmin, kmax, first, consec, rb_flat, ub, ub, ub, pos_col, pos_tiles, lq1, lk1, lq2, lk2, subln)


def _sb_attn_kernel(q_ref, k_ref, v_ref, o_ref, c_ref, acc_ref, *, t):
    qi = pl.program_id(2)
    row = lax.broadcasted_iota(jnp.int32, (t, t), 0)
    col = lax.broadcasted_iota(jnp.int32, (t, t), 1)
    tri = jnp.where(row > col, 1.0, 0.0).astype(jnp.bfloat16)
    strict = (col < row)[None]
    q = q_ref[...]
    lane = lax.broadcasted_iota(jnp.int32, q.shape, 1)
    zero = jnp.zeros_like(q)
    qs = jnp.concatenate([jnp.where(lane < SB_D, q, zero), jnp.where(lane < SB_D, zero, q)], axis=0)
    c_ref[...] = jnp.zeros(c_ref.shape, jnp.float32)
    acc_ref[...] = jnp.zeros(acc_ref.shape, jnp.float32)

    def tile(j, diag):
        ks = pl.multiple_of(j * t, t)
        k = k_ref[pl.ds(ks, t), :]
        v = v_ref[pl.ds(ks, t), :]
        z = lax.dot_general(qs, k, _NT, preferred_element_type=jnp.float32)
        lf = -(jnp.maximum(z, 0.0) + jnp.log(1.0 + jnp.exp(-jnp.abs(z))))
        if diag:
            lf = jnp.where(strict, lf.reshape(2, t, t), 0.0).reshape(2 * t, t)
        hi = _bf16(lf)
        lo = _bf16(lf - hi.astype(jnp.float32))
        c = c_ref[...]
        later = _dot(hi, tri) + _dot(lo, tri) + jnp.concatenate([c] * (t // LANES), axis=1)
        w = jnp.exp(lf + z + later)
        if diag:
            w = jnp.where(strict, w.reshape(2, t, t), 0.0).reshape(2 * t, t)
        acc_ref[...] += _dot(_bf16(w), v)
        c_ref[...] = c + jnp.sum(lf, axis=1, keepdims=True)

    tile(qi, True)

    def cond(carry):
        j, cmax = carry
        return jnp.logical_and(j >= 0, cmax > SB_LOG_ZERO)

    def body(carry):
        j, _ = carry
        tile(j, False)
        return j - 1, jnp.max(c_ref[...])

    lax.while_loop(cond, body, (qi - 1, jnp.max(c_ref[...])))
    acc = acc_ref[...]
    o_ref[...] = _bf16(jnp.where(lane < SB_D, acc[:t], acc[t:]))


def _sb_attn(ub, B, S, t=256):
    nq = S // t
    c0 = DIFF_IN // 128
    return pl.pallas_call(
        functools.partial(_sb_attn_kernel, t=t),
        grid=(B, 2, nq),
        in_specs=[pl.BlockSpec((t, 128), lambda b, hp, i: (b * nq + i, c0 + hp)),
                  pl.BlockSpec((S, 128), lambda b, hp, i: (b, c0 + 2 + hp)),
                  pl.BlockSpec((S, 128), lambda b, hp, i: (b, c0 + 4 + hp))],
        out_specs=pl.BlockSpec((t, 128), lambda b, hp, i: (b * nq + i, hp)),
        out_shape=jax.ShapeDtypeStruct((B * S, SB_HEADS * SB_D), jnp.bfloat16),
        scratch_shapes=[pltpu.VMEM((2 * t, LANES), jnp.float32)] * 2,
        compiler_params=_cparams(("parallel", "parallel", "arbitrary")),
        name="sb_attn",
    )(ub, ub, ub)


def _layer_norm(x, g, b):
    mu = jnp.mean(x, -1, keepdims=True)
    xc = x - mu
    var = jnp.mean(xc * xc, -1, keepdims=True)
    return xc * lax.rsqrt(var + EPS) * g + b


def _out_proj_kernel(h_ref, ym_ref, yd_ref, ys_ref, wm_ref, wd_ref, ws_ref, g_ref, b_ref, o_ref, *, alpha):
    mix = _dot(ym_ref[...], wm_ref[...]) + _dot(yd_ref[...], wd_ref[...]) + _dot(ys_ref[...], ws_ref[...])
    o_ref[...] = _layer_norm(alpha * h_ref[...] + mix, g_ref[...], b_ref[...])


def _out_proj(h2, ym, yd, ys, wm, wd, ws, g, b, alpha, tm=512):
    T = h2.shape[0]
    row = lambda a: pl.BlockSpec((tm, a.shape[1]), lambda i: (i, 0))
    full = lambda a: pl.BlockSpec(a.shape, lambda i: (0, 0))
    return pl.pallas_call(
        functools.partial(_out_proj_kernel, alpha=alpha),
        grid=(T // tm,),
        in_specs=[row(h2), row(ym), row(yd), row(ys), full(wm), full(wd), full(ws), full(g), full(b)],
        out_specs=pl.BlockSpec((tm, D_MODEL), lambda i: (i, 0)),
        out_shape=jax.ShapeDtypeStruct((T, D_MODEL), jnp.float32),
        compiler_params=_cparams(("parallel",)),
        name="out_proj_ln",
    )(h2, ym, yd, ys, wm, wd, ws, g, b)


def _first_max(vals):
    m = vals[0]
    for v in vals[1:]:
        m = jnp.maximum(m, v)
    taken = jnp.zeros(m.shape, jnp.bool_)
    hot = []
    for v in vals:
        is_first = jnp.logical_and(v == m, jnp.logical_not(taken))
        hot.append(is_first)
        taken = jnp.logical_or(taken, is_first)
    return m, hot


def _router_kernel(h_ref, rw_ref, rb_ref, gate_ref, slot_ref, dest_ref, chosen_ref, first_ref):
    logits = jnp.dot(h_ref[...], rw_ref[...], preferred_element_type=jnp.float32,
                     precision=lax.Precision.HIGHEST).T[:N_EXPERTS]
    scores = jax.nn.sigmoid(logits)
    sel = scores + rb_ref[...]
    ninf = -jnp.inf
    group_score, first, second = [], [], []
    for g in range(N_GROUPS):
        vals = [sel[g * EXPERTS_PER_GROUP + k:g * EXPERTS_PER_GROUP + k + 1, :] for k in range(EXPERTS_PER_GROUP)]
        m1, hot1 = _first_max(vals)
        m2, hot2 = _first_max([jnp.where(hh, ninf, v) for hh, v in zip(hot1, vals)])
        group_score.append(m1 + m2)
        first.append(hot1)
        second.append(hot2)
    _, best = _first_max(group_score)
    w1 = jnp.zeros_like(group_score[0])
    w2 = jnp.zeros_like(group_score[0])
    for g in range(N_GROUPS):
        for k in range(EXPERTS_PER_GROUP):
            e = g * EXPERTS_PER_GROUP + k
            sc = scores[e:e + 1, :]
            w1 = jnp.where(jnp.logical_and(best[g], first[g][k]), sc, w1)
            w2 = jnp.where(jnp.logical_and(best[g], second[g][k]), sc, w2)
    tot = w1 + w2
    for g in range(N_GROUPS):
        for k in range(EXPERTS_PER_GROUP):
            e = g * EXPERTS_PER_GROUP + k
            is1 = jnp.logical_and(best[g], first[g][k])
            is2 = jnp.logical_and(best[g], second[g][k])
            gate_ref[e:e + 1, :] = jnp.where(is1, w1 / tot, 0.0) + jnp.where(is2, w2 / tot, 0.0)
            chosen_ref[e:e + 1, :] = jnp.where(jnp.logical_or(is1, is2), 1.0, 0.0)
            first_ref[e:e + 1, :] = jnp.where(is1, 1.0, 0.0)
    chosen = chosen_ref[...]
    tm = chosen.shape[1]
    before = (lax.broadcasted_iota(jnp.int32, (tm, tm), 0) < lax.broadcasted_iota(jnp.int32, (tm, tm), 1))
    rank = _dot(_bf16(chosen), jnp.where(before, 1.0, 0.0).astype(jnp.bfloat16)).astype(jnp.int32)
    slot_ref[...] = jnp.where(chosen > 0.0, rank, -1)
    expert = lax.broadcasted_iota(jnp.int32, chosen.shape, 0)
    parked = jnp.logical_and(chosen > 0.0, rank < MOE_SLOTS)
    code = jnp.where(parked, expert * MOE_SLOTS + rank + 1, 0)
    is_first = first_ref[...] > 0.0
    dest_ref[0:1, :] = jnp.sum(jnp.where(is_first, code, 0), axis=0, keepdims=True) - 1
    dest_ref[1:2, :] = jnp.sum(jnp.where(is_first, 0, code), axis=0, keepdims=True) - 1


def _router(h2, rw_pad, rb_col):
    T = h2.shape[0]
    tm = MOE_SUB
    tok = lambda rows: pl.BlockSpec((rows, tm), lambda i: (0, i))
    return pl.pallas_call(
        _router_kernel,
        grid=(T // tm,),
        in_specs=[pl.BlockSpec((tm, D_MODEL), lambda i: (i, 0)),
                  pl.BlockSpec(rw_pad.shape, lambda i: (0, 0)),
                  pl.BlockSpec(rb_col.shape, lambda i: (0, 0))],
        out_specs=[tok(N_EXPERTS), tok(N_EXPERTS), tok(2)],
        out_shape=[jax.ShapeDtypeStruct((N_EXPERTS, T), jnp.float32),
                   jax.ShapeDtypeStruct((N_EXPERTS, T), jnp.int32),
                   jax.ShapeDtypeStruct((2, T), jnp.int32)],
        scratch_shapes=[pltpu.VMEM((N_EXPERTS, tm), jnp.float32)] * 2,
        compiler_params=_cparams(("parallel",)),
        name="router",
    )(h2, rw_pad, rb_col)


def _moe_kernel(npass_ref, h_ref, slot_ref, slot_t_ref, dest_t_ref, gate_ref, p_ref, wg_ref, wu_ref, wd_ref,
                pg_ref, pp_ref, g_ref, b_ref, o_ref, over_ref, xb_ref, y_ref, *, alpha, n_sub):
    i = pl.program_id(0)
    step = pl.program_id(1)
    R, SUB = MOE_SLOTS, MOE_SUB

    @pl.when(step == 0)
    def _():
        xb_ref[...] = _bf16(h_ref[...])
        over_ref[...] = jnp.zeros(over_ref.shape, jnp.float32)

    def expert_pass(e, k, c):
        slot_row = slot_ref[pl.ds(e, 1), :]
        gate_row = gate_ref[pl.ds(e, 1), :]
        xs, gs = [], []
        for j in range(n_sub):
            sl = slot_row[:, j * SUB:(j + 1) * SUB] - c * R
            hit = lax.broadcasted_iota(jnp.int32, (R, SUB), 0) == sl
            onehot = jnp.where(hit, 1.0, 0.0).astype(jnp.bfloat16)
            xs.append(_bf16(_dot(onehot, xb_ref[j * SUB:(j + 1) * SUB, :])))
            gs.append(jnp.sum(jnp.where(hit, gate_row[:, j * SUB:(j + 1) * SUB], 0.0), axis=1, keepdims=True))
        xe = jnp.concatenate(xs, axis=0)
        hid = jax.nn.silu(_dot(xe, wg_ref[k])) * _dot(xe, wu_ref[k])
        y = _dot(_bf16(hid), wd_ref[k])
        return [_bf16(y[j * R:(j + 1) * R] * gs[j]) for j in range(n_sub)]

    for k in range(MOE_EXPERTS_PER_STEP):
        e = step * MOE_EXPERTS_PER_STEP + k
        ys = expert_pass(e, k, 0)
        for j in range(n_sub):
            y_ref[j, pl.ds(pl.multiple_of(e * R, 16), R), :] = ys[j]

    for k in range(MOE_EXPERTS_PER_STEP):
        e = step * MOE_EXPERTS_PER_STEP + k

        def overflow_pass(c, carry, e=e, k=k):
            ys = expert_pass(e, k, c)
            lane16 = lax.broadcasted_iota(jnp.int32, (SUB, N_EXPERTS), 1)
            for j in range(n_sub):
                rows = slice(j * SUB, (j + 1) * SUB)
                slot_col = jnp.sum(jnp.where(lane16 == e, slot_t_ref[rows, :], 0), axis=1, keepdims=True)
                hit_t = lax.broadcasted_iota(jnp.int32, (SUB, R), 1) == slot_col - c * R
                over_ref[rows, :] += _dot(jnp.where(hit_t, 1.0, 0.0).astype(jnp.bfloat16), ys[j])
            return carry

        lax.fori_loop(1, npass_ref[i * N_EXPERTS + e], overflow_pass, 0)

    @pl.when(step == N_EXPERTS // MOE_EXPERTS_PER_STEP - 1)
    def _():
        lane = lax.broadcasted_iota(jnp.int32, (SUB, N_EXPERTS * R), 1)
        for j in range(n_sub):
            rows = slice(j * SUB, (j + 1) * SUB)
            dest = dest_t_ref[rows, :]
            hit = jnp.logical_or(lane == dest[:, 0:1], lane == dest[:, 1:2])
            scatter = jnp.where(hit, 1.0, 0.0).astype(jnp.bfloat16)
            ffn = _dot(scatter, y_ref[j]) + over_ref[rows, :]
            h = h_ref[rows, :]
            ple = jax.nn.sigmoid(_dot(xb_ref[rows, :], pg_ref[...])) * _dot(_bf16(p_ref[rows, :]), pp_ref[...])
            o_ref[rows, :] = _layer_norm(alpha * h + ffn + ple, g_ref[...], b_ref[...])


def _moe(h2, slot, dest, gate, p2, wg, wu, wd, layer, pg, pp, g, b, alpha, tm=1024):
    T = h2.shape[0]
    n_sub = tm // MOE_SUB
    n_tiles = T // tm
    per = MOE_EXPERTS_PER_STEP
    count = jnp.sum((slot >= 0).reshape(N_EXPERTS, n_tiles, n_sub, MOE_SUB), axis=3)
    npass = jnp.maximum(1, (jnp.max(count, axis=2) + MOE_SLOTS - 1) // MOE_SLOTS).T.reshape(-1).astype(jnp.int32)
    full = lambda a: pl.BlockSpec(a.shape, lambda i, e, *_: (0, 0))
    experts = lambda a: pl.BlockSpec((None, per) + a.shape[2:], lambda i, e, *_: (layer, e, 0, 0))
    grid_spec = pltpu.PrefetchScalarGridSpec(
        num_scalar_prefetch=1,
        grid=(n_tiles, N_EXPERTS // per),
        in_specs=[pl.BlockSpec((tm, D_MODEL), lambda i, e, *_: (i, 0)),
                  pl.BlockSpec((N_EXPERTS, tm), lambda i, e, *_: (0, i)),
                  pl.BlockSpec((tm, N_EXPERTS), lambda i, e, *_: (i, 0)),
                  pl.BlockSpec((tm, 2), lambda i, e, *_: (i, 0)),
                  pl.BlockSpec((N_EXPERTS, tm), lambda i, e, *_: (0, i)),
                  pl.BlockSpec((tm, PLE_DIM), lambda i, e, *_: (i, 0)),
                  experts(wg), experts(wu), experts(wd),
                  full(pg), full(pp), full(g), full(b)],
        out_specs=pl.BlockSpec((tm, D_MODEL), lambda i, e, *_: (i, 0)),
        scratch_shapes=[pltpu.VMEM((tm, D_MODEL), jnp.float32),
                        pltpu.VMEM((tm, D_MODEL), jnp.bfloat16),
                        pltpu.VMEM((n_sub, N_EXPERTS * MOE_SLOTS, D_MODEL), jnp.bfloat16)],
    )
    return pl.pallas_call(
        functools.partial(_moe_kernel, alpha=alpha, n_sub=n_sub),
        grid_spec=grid_spec,
        out_shape=jax.ShapeDtypeStruct((T, D_MODEL), jnp.float32),
        compiler_params=_cparams(("parallel", "arbitrary")),
        name="moe_ple_ln",
    )(npass, h2, slot, slot.T, dest.T, gate, p2, wg, wu, wd, pg, pp, g, b)


def _rotate_half_cols(w):
    half = w.shape[1] // 2
    return jnp.concatenate([-w[:, half:], w[:, :half]], axis=1)


def _in_proj_weights(w_in):
    z = lambda n: jnp.zeros((D_MODEL, n), w_in.dtype)
    cq = w_in[:, :MLA_Q_RANK]
    ckv = w_in[:, MLA_Q_RANK:MLA_Q_RANK + MLA_KV_RANK]
    kr = w_in[:, MLA_Q_RANK + MLA_KV_RANK:MLA_IN]
    n_dq = DIFF_HEADS * 2 * DIFF_QK
    sb0 = MLA_IN + DIFF_IN
    n_sq = SB_HEADS * SB_D
    cols = [ckv,
            z(MLA_NOPE), kr, z(128 - MLA_NOPE - MLA_ROPE),
            z(MLA_NOPE), _rotate_half_cols(kr), z(128 - MLA_NOPE - MLA_ROPE),
            cq, z(256 - MLA_Q_RANK),
            w_in[:, MLA_IN:MLA_IN + n_dq] * (DIFF_QK ** -0.5 * LOG2E),
            w_in[:, MLA_IN + n_dq:sb0],
            w_in[:, sb0:sb0 + n_sq] * (SB_D ** -0.5),
            w_in[:, sb0 + n_sq:]]
    return _bf16(jnp.concatenate(cols, axis=1))


def _mla_up_weights(w_uq, w_ukv):
    dq = MLA_NOPE + MLA_ROPE
    zq = lambda n: jnp.zeros((MLA_Q_RANK, n), w_uq.dtype)
    plain, rot = [], []
    for h in range(MLA_HEADS):
        wh = w_uq[:, h * dq:(h + 1) * dq]
        plain += [wh, zq(128 - dq)]
        rot += [zq(MLA_NOPE), _rotate_half_cols(wh[:, MLA_NOPE:]), zq(128 - dq)]
    wq = jnp.concatenate(plain + rot, axis=1)
    wq = jnp.concatenate([wq, jnp.zeros((256 - MLA_Q_RANK, wq.shape[1]), wq.dtype)], axis=0)
    dkv = MLA_NOPE + MLA_V
    zk = lambda n: jnp.zeros((MLA_KV_RANK, n), w_ukv.dtype)
    kcols, vcols = [], []
    for h in range(MLA_HEADS):
        wh = w_ukv[:, h * dkv:(h + 1) * dkv]
        kcols += [wh[:, :MLA_NOPE], zk(128 - MLA_NOPE)]
        vcols += [wh[:, MLA_NOPE:]]
    wkv = jnp.concatenate(kcols + vcols, axis=1)
    return _bf16(wq), _bf16(wkv)


def _attn_tiles(S):
    pick = lambda want: max(c for c in (128, 256, 512, 1024) if c <= want and S % c == 0)
    return pick(512), pick(512), pick(256)


def kernel(x, p, positions, w_in, mla_q_norm, mla_w_uq, mla_kv_norm, mla_w_ukv, diff_lambda_q1, diff_lambda_k1, diff_lambda_q2, diff_lambda_k2, diff_subln, rel_bias, w_o, ln1_g, ln1_b, router_w, router_b, w_gate, w_up, w_down, ple_proj, ple_gate, ln2_g, ln2_b):
    B, S, _ = x.shape
    depth = w_in.shape[0]
    T = B * S
    alpha = (2 * depth) ** 0.25
    t_mla, t_diff, t_sb = _attn_tiles(S)
    nq = S // t_diff

    pos_col = positions.reshape(T, 1)
    pos_tiles = positions.reshape(B * nq, 1, t_diff)
    tile_pos = positions.reshape(B * nq, t_diff)
    qmin = jnp.min(tile_pos, axis=1)
    kmax = jnp.max(tile_pos, axis=1)
    first = tile_pos[:, 0]
    consec = jnp.all(tile_pos[:, 1:] - tile_pos[:, :-1] == 1, axis=1).astype(jnp.int32)
    rb_flat = rel_bias.T.reshape(-1).astype(jnp.float32)

    half = MLA_ROPE // 2
    inv = ROPE_THETA ** (-jnp.arange(half, dtype=jnp.float32) / half)
    cos_t, sin_t = _rope_table(positions.reshape(1, T), jnp.concatenate([inv, inv]).reshape(MLA_ROPE, 1))
    pad_l = lambda v: jnp.full((T, MLA_NOPE), v, jnp.float32)
    pad_r = jnp.zeros((T, LANES - MLA_NOPE - MLA_ROPE), jnp.float32)
    cosf = jnp.concatenate([pad_l(1.0), cos_t.T, pad_r], axis=1)
    sinf = jnp.concatenate([pad_l(0.0), sin_t.T, pad_r], axis=1)

    rw_pad = jnp.pad(router_w.astype(jnp.float32), ((0, 0), (0, LANES - N_EXPERTS)))
    rb_col = router_b.reshape(N_EXPERTS, 1).astype(jnp.float32)
    row = lambda a: a.reshape(1, -1)

    wg_all, wu_all, wd_all = _bf16(w_gate), _bf16(w_up), _bf16(w_down)

    h = x.reshape(T, D_MODEL)
    for i in range(depth):
        lam_init = 0.8 - 0.6 * math.exp(-0.3 * i)
        ua, ub = _in_proj(h, _in_proj_weights(w_in[i]))
        wq, wkv = _mla_up_weights(mla_w_uq[i], mla_w_ukv[i])
        gq = jnp.concatenate([mla_q_norm[i], jnp.zeros((256 - MLA_Q_RANK,), mla_q_norm.dtype)]).reshape(1, 256)
        q_m, k_m, v_m = _mla_prep(ua, cosf, sinf, gq, row(mla_kv_norm[i]), wq, wkv)
        y_mla = _mla_attn(q_m, k_m, v_m, B, S, t=t_mla)
        y_diff = _diff_attn(ub, pos_col, pos_tiles, qmin, kmax, first, consec, rb_flat,
                            row(diff_lambda_q1[i]), row(diff_lambda_k1[i]),
                            row(diff_lambda_q2[i]), row(diff_lambda_k2[i]), row(diff_subln[i]),
                            B, S, lam_init, t=t_diff)
        y_sb = _sb_attn(ub, B, S, t=t_sb)
        wo = _bf16(w_o[i])
        n_m, n_d = MLA_HEADS * MLA_V, DIFF_HEADS * DIFF_V
        h = _out_proj(h, y_mla, y_diff, y_sb, wo[:n_m], wo[n_m:n_m + n_d], wo[n_m + n_d:],
                      row(ln1_g[i]), row(ln1_b[i]), alpha)
        gate, slot, dest = _router(h, rw_pad, rb_col)
        h = _moe(h, slot, dest, gate, p[i].reshape(T, PLE_DIM), wg_all, wu_all, wd_all, i,
                 _bf16(ple_gate[i]), _bf16(ple_proj[i]), row(ln2_g[i]), row(ln2_b[i]), alpha)
    return h.reshape(B, S, D_MODEL)
```

```python
import functools
import math

import jax
import jax.numpy as jnp
from jax import lax
from jax.experimental import pallas as pl
from jax.experimental.pallas import tpu as pltpu

D_MODEL = 1024
CHUNK = 64
PLE_DIM = 256
MLA_HEADS = 4
MLA_NOPE = 64
MLA_ROPE = 32
MLA_V = 64
MLA_Q_RANK = 192
MLA_KV_RANK = 128
ROPE_THETA = 10000.0
DIFF_HEADS = 4
DIFF_QK = 64
DIFF_V = 2 * DIFF_QK
SB_HEADS = 4
SB_D = 64
REL_BUCKETS = 32
REL_MAX_DIST = 128
N_EXPERTS = 16
N_GROUPS = 4
EXPERTS_PER_GROUP = N_EXPERTS // N_GROUPS
D_EXPERT = 512
MLA_IN = MLA_Q_RANK + MLA_KV_RANK + MLA_ROPE
DIFF_IN = 2 * DIFF_HEADS * 2 * DIFF_QK + DIFF_HEADS * DIFF_V
SB_IN = 3 * SB_HEADS * SB_D
EPS = 1e-5
NEG_INF = -1e30

LANES = 128
VMEM_LIMIT = 56 * 1024 * 1024

UA_CKV = 0
UA_KR = 128
UA_KRS = 256
UA_CQ = 384
UA_W = 640
UB_W = DIFF_IN + SB_IN

T5_FAR = 91
SB_LOG_ZERO = -88.0
DIFF_HEADS_PER_STEP = 2
FAR_TILE_WIDTHS = (4, 2)
MOE_SUB = 512
MOE_SLOTS = 96
MOE_EXPERTS_PER_STEP = 2

_NT = (((1,), (1,)), ((), ()))
LOG2E = math.log2(math.e)


def _cparams(sem):
    return pltpu.CompilerParams(dimension_semantics=sem, vmem_limit_bytes=VMEM_LIMIT)


def _bf16(a):
    return a.astype(jnp.bfloat16)


def _dot(a, b):
    return jnp.dot(a, b, preferred_element_type=jnp.float32)


def _in_proj_kernel(x_ref, w_ref, ua_ref, ub_ref):
    x = _bf16(x_ref[...])
    step = 512
    for c in range(0, UA_W, step):
        e = min(c + step, UA_W)
        ua_ref[:, c:e] = _dot(x, w_ref[:, c:e])
    for c in range(0, UB_W, step):
        e = min(c + step, UB_W)
        ub_ref[:, c:e] = _bf16(_dot(x, w_ref[:, UA_W + c:UA_W + e]))


def _in_proj(h2, w_all, tm=512):
    T = h2.shape[0]
    return pl.pallas_call(
        _in_proj_kernel,
        grid=(T // tm,),
        in_specs=[pl.BlockSpec((tm, D_MODEL), lambda i: (i, 0)),
                  pl.BlockSpec((D_MODEL, UA_W + UB_W), lambda i: (0, 0))],
        out_specs=[pl.BlockSpec((tm, UA_W), lambda i: (i, 0)),
                   pl.BlockSpec((tm, UB_W), lambda i: (i, 0))],
        out_shape=[jax.ShapeDtypeStruct((T, UA_W), jnp.float32),
                   jax.ShapeDtypeStruct((T, UB_W), jnp.bfloat16)],
        compiler_params=_cparams(("parallel",)),
        name="in_proj",
    )(h2, w_all)


def _rope_table_kernel(pos_ref, inv_ref, cos_ref, sin_ref):
    ang = pos_ref[...].astype(jnp.float32) * inv_ref[...]
    cos_ref[...] = jnp.cos(ang)
    sin_ref[...] = jnp.sin(ang)


def _rope_table(pos_row, inv_col, tn=2048):
    T = pos_row.shape[1]
    n = inv_col.shape[0]
    tn = min(tn, T)
    return pl.pallas_call(
        _rope_table_kernel,
        grid=(T // tn,),
        in_specs=[pl.BlockSpec((1, tn), lambda i: (0, i)),
                  pl.BlockSpec((n, 1), lambda i: (0, 0))],
        out_specs=[pl.BlockSpec((n, tn), lambda i: (0, i))] * 2,
        out_shape=[jax.ShapeDtypeStruct((n, T), jnp.float32)] * 2,
        compiler_params=_cparams(("parallel",)),
        name="rope_table",
    )(pos_row, inv_col)


def _mla_prep_kernel(ua_ref, cos_ref, sin_ref, gq_ref, gkv_ref, wq_ref, wkv_ref,
                     q_ref, k_ref, v_ref):
    ckv = ua_ref[:, UA_CKV:UA_CKV + 128]
    kr = ua_ref[:, UA_KR:UA_KR + 128]
    krs = ua_ref[:, UA_KRS:UA_KRS + 128]
    cq = ua_ref[:, UA_CQ:UA_CQ + 256]
    cqn = cq * lax.rsqrt(jnp.sum(cq * cq, -1, keepdims=True) * (1.0 / MLA_Q_RANK) + EPS) * gq_ref[...]
    ckvn = ckv * lax.rsqrt(jnp.sum(ckv * ckv, -1, keepdims=True) * (1.0 / MLA_KV_RANK) + EPS) * gkv_ref[...]
    cosf = cos_ref[...]
    sinf = sin_ref[...]
    scale = (MLA_NOPE + MLA_ROPE) ** -0.5 * LOG2E
    cqb = _bf16(cqn)
    ckvb = _bf16(ckvn)
    k_rope = kr * cosf + krs * sinf
    for h in range(MLA_HEADS):
        a = _dot(cqb, wq_ref[:, h * 128:(h + 1) * 128])
        b = _dot(cqb, wq_ref[:, 512 + h * 128:512 + (h + 1) * 128])
        q_ref[:, h * 128:(h + 1) * 128] = _bf16((a * cosf + b * sinf) * scale)
        kn = _dot(ckvb, wkv_ref[:, h * 128:(h + 1) * 128])
        k_ref[:, h * 128:(h + 1) * 128] = _bf16(kn + k_rope)
    v_ref[...] = _bf16(_dot(ckvb, wkv_ref[:, 512:768]))


def _mla_prep(ua, cosf, sinf, gq, gkv, wq, wkv, tm=512):
    T = ua.shape[0]
    full = lambda a: pl.BlockSpec(a.shape, lambda i: (0, 0))
    return pl.pallas_call(
        _mla_prep_kernel,
        grid=(T // tm,),
        in_specs=[pl.BlockSpec((tm, UA_W), lambda i: (i, 0)),
                  pl.BlockSpec((tm, LANES), lambda i: (i, 0)),
                  pl.BlockSpec((tm, LANES), lambda i: (i, 0)),
                  full(gq), full(gkv), full(wq), full(wkv)],
        out_specs=[pl.BlockSpec((tm, 512), lambda i: (i, 0)),
                   pl.BlockSpec((tm, 512), lambda i: (i, 0)),
                   pl.BlockSpec((tm, 256), lambda i: (i, 0))],
        out_shape=[jax.ShapeDtypeStruct((T, 512), jnp.bfloat16),
                   jax.ShapeDtypeStruct((T, 512), jnp.bfloat16),
                   jax.ShapeDtypeStruct((T, 256), jnp.bfloat16)],
        compiler_params=_cparams(("parallel",)),
        name="mla_prep",
    )(ua, cosf, sinf, gq, gkv, wq, wkv)


def _chunk_mask(tq, tk):
    qc = lax.broadcasted_iota(jnp.int32, (tq, tk), 0) // CHUNK
    kc = lax.broadcasted_iota(jnp.int32, (tq, tk), 1) // CHUNK
    return kc <= qc


def _softmax_update(s, v, m_ref, l_ref, acc_ref, idx, shift=None):
    tk = s.shape[1]
    m_old = m_ref[idx]
    row_max = jnp.max(s, axis=1, keepdims=True)
    if shift is not None:
        row_max = row_max + shift
    m_new = jnp.maximum(m_old, row_max)
    alpha = jnp.exp2(m_old - m_new)
    m_sub = m_new if shift is None else m_new - shift
    p = jnp.exp2(s - jnp.concatenate([m_sub] * (tk // LANES), axis=1))
    psum = p[:, :LANES]
    for c in range(LANES, tk, LANES):
        psum = psum + p[:, c:c + LANES]
    l_ref[idx] = alpha * l_ref[idx] + psum
    acc_ref[idx] = alpha * acc_ref[idx] + _dot(_bf16(p), v)
    m_ref[idx] = m_new


def _softmax_init(m_ref, l_ref, acc_ref):
    m_ref[...] = jnp.full(m_ref.shape, NEG_INF, jnp.float32)
    l_ref[...] = jnp.zeros(l_ref.shape, jnp.float32)
    acc_ref[...] = jnp.zeros(acc_ref.shape, jnp.float32)


def _softmax_result(l_ref, acc_ref, idx):
    return acc_ref[idx] / jnp.sum(l_ref[idx], axis=1, keepdims=True)


def _mla_attn_kernel(q_ref, k_ref, v_ref, o_ref, m_ref, l_ref, acc_ref, *, t):
    qi = pl.program_id(2)
    _softmax_init(m_ref, l_ref, acc_ref)

    def tile(j, masked, width=1):
        ks = pl.multiple_of(j * t, t)
        v = v_ref[pl.ds(ks, width * t), :]
        for hh in range(2):
            q = q_ref[:, hh * 128:(hh + 1) * 128]
            k = k_ref[pl.ds(ks, width * t), hh * 128:(hh + 1) * 128]
            s = lax.dot_general(q, k, _NT, preferred_element_type=jnp.float32)
            if masked:
                s = jnp.where(_chunk_mask(t, t), s, NEG_INF)
            _softmax_update(s, v, m_ref, l_ref, acc_ref, hh)

    done = 0
    for width in FAR_TILE_WIDTHS + (1,):
        trips = (qi - done) // width

        def body(i, carry, width=width, done=done):
            tile(done + i * width, False, width=width)
            return carry

        lax.fori_loop(0, trips, body, 0)
        done = done + trips * width

    tile(qi, True)
    lane = lax.broadcasted_iota(jnp.int32, (t, 128), 1)
    o_ref[...] = _bf16(jnp.where(lane < MLA_V, _softmax_result(l_ref, acc_ref, 0),
                                 _softmax_result(l_ref, acc_ref, 1)))


def _mla_attn(q, k, v, B, S, t=256):
    nq = S // t
    return pl.pallas_call(
        functools.partial(_mla_attn_kernel, t=t),
        grid=(B, 2, nq),
        in_specs=[pl.BlockSpec((t, 256), lambda b, hp, i: (b * nq + i, hp)),
                  pl.BlockSpec((S, 256), lambda b, hp, i: (b, hp)),
                  pl.BlockSpec((S, 128), lambda b, hp, i: (b, hp))],
        out_specs=pl.BlockSpec((t, 128), lambda b, hp, i: (b * nq + i, hp)),
        out_shape=jax.ShapeDtypeStruct((B * S, MLA_HEADS * MLA_V), jnp.bfloat16),
        scratch_shapes=[pltpu.VMEM((2, t, LANES), jnp.float32)] * 3,
        compiler_params=_cparams(("parallel", "parallel", "arbitrary")),
        name="mla_attn",
    )(q, k, v)


def _t5_bias(rel, rb_ref, h):
    nb = REL_BUCKETS // 2
    max_exact = nb // 2
    n = jnp.abs(rel)
    nf = jnp.maximum(n, 1).astype(jnp.float32)
    large = max_exact + (jnp.log(nf / max_exact) / math.log(REL_MAX_DIST / max_exact)
                         * (nb - max_exact)).astype(jnp.int32)
    large = jnp.minimum(large, nb - 1)
    low = jnp.where(n < max_exact, n, large)
    neg = jnp.zeros(rel.shape, jnp.float32)
    pos = jnp.zeros(rel.shape, jnp.float32)
    for j in range(nb):
        eq = low == j
        neg = jnp.where(eq, rb_ref[h * REL_BUCKETS + j], neg)
        pos = jnp.where(eq, rb_ref[h * REL_BUCKETS + nb + j], pos)
    return jnp.where(rel > 0, pos, neg)


def _diff_attn_kernel(qmin_ref, kmax_ref, first_ref, consec_ref,
                      rb_ref, q_ref, k_ref, v_ref, posq_ref, posk_ref,
                      lq1_ref, lk1_ref, lq2_ref, lk2_ref, sub_ref,
                      o_ref, m_ref, l_ref, acc_ref, *, t, nq, lam_init):
    b = pl.program_id(0)
    hp = pl.program_id(1)
    qi = pl.program_id(2)
    _softmax_init(m_ref, l_ref, acc_ref)

    lane = lax.broadcasted_iota(jnp.int32, (t, LANES), 1)
    qs = []
    for hh in range(DIFF_HEADS_PER_STEP):
        q = q_ref[:, hh * LANES:(hh + 1) * LANES]
        zero = jnp.zeros_like(q)
        qs.append(jnp.concatenate([jnp.where(lane < DIFF_QK, q, zero), jnp.where(lane < DIFF_QK, zero, q)], axis=0))
    tq = b * nq + qi
    qmin = qmin_ref[tq]

    def toeplitz_bias(j, h):
        d0 = first_ref[b * nq + j] - first_ref[tq]
        x = lax.broadcasted_iota(jnp.int32, (1, 2 * t), 1)
        g = _t5_bias(d0 + jnp.where(x < t, x, x - 2 * t), rb_ref, h)
        g = pltpu.roll(jnp.broadcast_to(g, (t, 2 * t)), 0, 1, stride=1, stride_axis=0)
        return g[:, :t]

    def general_bias(j, h):
        rel = posk_ref[j] - posq_ref[...]
        return _t5_bias(rel, rb_ref, h)

    def tile(j, masked, bias_fn, width=1):
        ks = pl.multiple_of(j * t, t)
        for hh in range(DIFF_HEADS_PER_STEP):
            h = hp * DIFF_HEADS_PER_STEP + hh
            k = k_ref[pl.ds(ks, width * t), hh * LANES:(hh + 1) * LANES]
            v = v_ref[pl.ds(ks, width * t), hh * LANES:(hh + 1) * LANES]
            s = lax.dot_general(qs[hh], k, _NT, preferred_element_type=jnp.float32)
            if bias_fn is None:
                shift = rb_ref[h * REL_BUCKETS + REL_BUCKETS // 2 - 1] * LOG2E
            else:
                shift = None
                s = s.reshape(2, t, t) + (bias_fn(j, h) * LOG2E)[None]
                if masked:
                    s = jnp.where(_chunk_mask(t, t)[None], s, NEG_INF)
                s = s.reshape(2 * t, t)
            _softmax_update(s, v, m_ref, l_ref, acc_ref, hh, shift)

    def near_tile(j, masked):
        consecutive = jnp.logical_and(consec_ref[tq] == 1, consec_ref[b * nq + j] == 1)

        @pl.when(consecutive)
        def _():
            tile(j, masked, toeplitz_bias)

        @pl.when(jnp.logical_not(consecutive))
        def _():
            tile(j, masked, general_bias)

    def is_far(j):
        return kmax_ref[b * nq + jnp.minimum(j, qi)] - qmin <= -T5_FAR

    def wide_far_loop(start, width):
        def cond(j):
            ok = j + width - 1 < qi
            for d in range(width):
                ok = jnp.logical_and(ok, is_far(j + d))
            return ok

        def step(j):
            tile(j, False, None, width=width)
            return j + width

        return lax.while_loop(cond, step, start)

    def body(j, carry):
        far = is_far(j)

        @pl.when(far)
        def _():
            tile(j, False, None)

        @pl.when(jnp.logical_not(far))
        def _():
            near_tile(j, False)

        return carry

    done = 0
    for width in FAR_TILE_WIDTHS:
        done = wide_far_loop(done, width)
    lax.fori_loop(done, qi, body, 0)
    near_tile(qi, True)

    f32 = jnp.float32
    lam = (jnp.exp(jnp.sum(lq1_ref[...].astype(f32) * lk1_ref[...].astype(f32), keepdims=True))
           - jnp.exp(jnp.sum(lq2_ref[...].astype(f32) * lk2_ref[...].astype(f32), keepdims=True))
           + lam_init)
    for hh in range(DIFF_HEADS_PER_STEP):
        a = _softmax_result(l_ref, acc_ref, hh)
        o = a[:t] - lam * a[t:]
        o = o * lax.rsqrt(jnp.mean(o * o, -1, keepdims=True) + EPS) * sub_ref[...] * (1.0 - lam_init)
        o_ref[:, hh * LANES:(hh + 1) * LANES] = _bf16(o)


def _diff_attn(ub, pos_col, pos_tiles, qmin, kmax, first, consec, rb_flat, lq1, lk1, lq2, lk2, subln,
               B, S, lam_init, t=256):
    nq = S // t
    per = DIFF_HEADS_PER_STEP
    G = DIFF_HEADS // per
    small = lambda a: pl.BlockSpec(a.shape, lambda b, h, i, *_: (0, 0))
    grid_spec = pltpu.PrefetchScalarGridSpec(
        num_scalar_prefetch=4,
        grid=(B, G, nq),
        in_specs=[pl.BlockSpec(memory_space=pltpu.SMEM),
                  pl.BlockSpec((t, per * 128), lambda b, h, i, *_: (b * nq + i, h)),
                  pl.BlockSpec((S, per * 128), lambda b, h, i, *_: (b, G + h)),
                  pl.BlockSpec((S, per * 128), lambda b, h, i, *_: (b, 2 * G + h)),
                  pl.BlockSpec((t, 1), lambda b, h, i, *_: (b * nq + i, 0)),
                  pl.BlockSpec((nq, 1, t), lambda b, h, i, *_: (b, 0, 0)),
                  small(lq1), small(lk1), small(lq2), small(lk2), small(subln)],
        out_specs=pl.BlockSpec((t, per * 128), lambda b, h, i, *_: (b * nq + i, h)),
        scratch_shapes=[pltpu.VMEM((per, 2 * t, LANES), jnp.float32)] * 3,
    )
    return pl.pallas_call(
        functools.partial(_diff_attn_kernel, t=t, nq=nq, lam_init=lam_init),
        grid_spec=grid_spec,
        out_shape=jax.ShapeDtypeStruct((B * S, DIFF_HEADS * DIFF_V), jnp.bfloat16),
        compiler_params=_cparams(("parallel", "parallel", "arbitrary")),
        name="diff_attn",
    )(qmin, kmax, first, consec, rb_flat, ub, ub, ub, pos_col, pos_tiles, lq1, lk1, lq2, lk2, subln)


def _sb_attn_kernel(q_ref, k_ref, v_ref, o_ref, c_ref, acc_ref, *, t):
    qi = pl.program_id(2)
    row = lax.broadcasted_iota(jnp.int32, (t, t), 0)
    col = lax.broadcasted_iota(jnp.int32, (t, t), 1)
    tri = jnp.where(row > col, 1.0, 0.0).astype(jnp.bfloat16)
    strict = (col < row)[None]
    q = q_ref[...]
    lane = lax.broadcasted_iota(jnp.int32, q.shape, 1)
    zero = jnp.zeros_like(q)
    qs = jnp.concatenate([jnp.where(lane < SB_D, q, zero), jnp.where(lane < SB_D, zero, q)], axis=0)
    c_ref[...] = jnp.zeros(c_ref.shape, jnp.float32)
    acc_ref[...] = jnp.zeros(acc_ref.shape, jnp.float32)

    def tile(j, diag):
        ks = pl.multiple_of(j * t, t)
        k = k_ref[pl.ds(ks, t), :]
        v = v_ref[pl.ds(ks, t), :]
        z = lax.dot_general(qs, k, _NT, preferred_element_type=jnp.float32)
        lf = -(jnp.maximum(z, 0.0) + jnp.log(1.0 + jnp.exp(-jnp.abs(z))))
        if diag:
            lf = jnp.where(strict, lf.reshape(2, t, t), 0.0).reshape(2 * t, t)
        hi = _bf16(lf)
        lo = _bf16(lf - hi.astype(jnp.float32))
        c = c_ref[...]
        later = _dot(hi, tri) + _dot(lo, tri) + jnp.concatenate([c] * (t // LANES), axis=1)
        w = jnp.exp(lf + z + later)
        if diag:
            w = jnp.where(strict, w.reshape(2, t, t), 0.0).reshape(2 * t, t)
        acc_ref[...] += _dot(_bf16(w), v)
        c_ref[...] = c + jnp.sum(lf, axis=1, keepdims=True)

    tile(qi, True)

    def cond(carry):
        j, cmax = carry
        return jnp.logical_and(j >= 0, cmax > SB_LOG_ZERO)

    def body(carry):
        j, _ = carry
        tile(j, False)
        return j - 1, jnp.max(c_ref[...])

    lax.while_loop(cond, body, (qi - 1, jnp.max(c_ref[...])))
    acc = acc_ref[...]
    o_ref[...] = _bf16(jnp.where(lane < SB_D, acc[:t], acc[t:]))


def _sb_attn(ub, B, S, t=256):
    nq = S // t
    c0 = DIFF_IN // 128
    return pl.pallas_call(
        functools.partial(_sb_attn_kernel, t=t),
        grid=(B, 2, nq),
        in_specs=[pl.BlockSpec((t, 128), lambda b, hp, i: (b * nq + i, c0 + hp)),
                  pl.BlockSpec((S, 128), lambda b, hp, i: (b, c0 + 2 + hp)),
                  pl.BlockSpec((S, 128), lambda b, hp, i: (b, c0 + 4 + hp))],
        out_specs=pl.BlockSpec((t, 128), lambda b, hp, i: (b * nq + i, hp)),
        out_shape=jax.ShapeDtypeStruct((B * S, SB_HEADS * SB_D), jnp.bfloat16),
        scratch_shapes=[pltpu.VMEM((2 * t, LANES), jnp.float32)] * 2,
        compiler_params=_cparams(("parallel", "parallel", "arbitrary")),
        name="sb_attn",
    )(ub, ub, ub)


def _layer_norm(x, g, b):
    mu = jnp.mean(x, -1, keepdims=True)
    xc = x - mu
    var = jnp.mean(xc * xc, -1, keepdims=True)
    return xc * lax.rsqrt(var + EPS) * g + b


def _out_proj_kernel(h_ref, ym_ref, yd_ref, ys_ref, wm_ref, wd_ref, ws_ref, g_ref, b_ref, o_ref, *, alpha):
    mix = _dot(ym_ref[...], wm_ref[...]) + _dot(yd_ref[...], wd_ref[...]) + _dot(ys_ref[...], ws_ref[...])
    o_ref[...] = _layer_norm(alpha * h_ref[...] + mix, g_ref[...], b_ref[...])


def _out_proj(h2, ym, yd, ys, wm, wd, ws, g, b, alpha, tm=512):
    T = h2.shape[0]
    row = lambda a: pl.BlockSpec((tm, a.shape[1]), lambda i: (i, 0))
    full = lambda a: pl.BlockSpec(a.shape, lambda i: (0, 0))
    return pl.pallas_call(
        functools.partial(_out_proj_kernel, alpha=alpha),
        grid=(T // tm,),
        in_specs=[row(h2), row(ym), row(yd), row(ys), full(wm), full(wd), full(ws), full(g), full(b)],
        out_specs=pl.BlockSpec((tm, D_MODEL), lambda i: (i, 0)),
        out_shape=jax.ShapeDtypeStruct((T, D_MODEL), jnp.float32),
        compiler_params=_cparams(("parallel",)),
        name="out_proj_ln",
    )(h2, ym, yd, ys, wm, wd, ws, g, b)


def _first_max(vals):
    m = vals[0]
    for v in vals[1:]:
        m = jnp.maximum(m, v)
    taken = jnp.zeros(m.shape, jnp.bool_)
    hot = []
    for v in vals:
        is_first = jnp.logical_and(v == m, jnp.logical_not(taken))
        hot.append(is_first)
        taken = jnp.logical_or(taken, is_first)
    return m, hot


def _router_kernel(h_ref, rw_ref, rb_ref, gate_ref, slot_ref, dest_ref, chosen_ref, first_ref):
    h = h_ref[...]
    rw = rw_ref[...]
    h_hi, rw_hi = _bf16(h), _bf16(rw)
    h_lo, rw_lo = _bf16(h - h_hi.astype(jnp.float32)), _bf16(rw - rw_hi.astype(jnp.float32))
    logits = (_dot(h_hi, rw_hi) + (_dot(h_hi, rw_lo) + _dot(h_lo, rw_hi))).T[:N_EXPERTS]
    scores = jax.nn.sigmoid(logits)
    sel = scores + rb_ref[...]
    ninf = -jnp.inf
    group_score, first, second = [], [], []
    for g in range(N_GROUPS):
        vals = [sel[g * EXPERTS_PER_GROUP + k:g * EXPERTS_PER_GROUP + k + 1, :] for k in range(EXPERTS_PER_GROUP)]
        m1, hot1 = _first_max(vals)
        m2, hot2 = _first_max([jnp.where(hh, ninf, v) for hh, v in zip(hot1, vals)])
        group_score.append(m1 + m2)
        first.append(hot1)
        second.append(hot2)
    _, best = _first_max(group_score)
    w1 = jnp.zeros_like(group_score[0])
    w2 = jnp.zeros_like(group_score[0])
    for g in range(N_GROUPS):
        for k in range(EXPERTS_PER_GROUP):
            e = g * EXPERTS_PER_GROUP + k
            sc = scores[e:e + 1, :]
            w1 = jnp.where(jnp.logical_and(best[g], first[g][k]), sc, w1)
            w2 = jnp.where(jnp.logical_and(best[g], second[g][k]), sc, w2)
    tot = w1 + w2
    for g in range(N_GROUPS):
        for k in range(EXPERTS_PER_GROUP):
            e = g * EXPERTS_PER_GROUP + k
            is1 = jnp.logical_and(best[g], first[g][k])
            is2 = jnp.logical_and(best[g], second[g][k])
            gate_ref[e:e + 1, :] = jnp.where(is1, w1 / tot, 0.0) + jnp.where(is2, w2 / tot, 0.0)
            chosen_ref[e:e + 1, :] = jnp.where(jnp.logical_or(is1, is2), 1.0, 0.0)
            first_ref[e:e + 1, :] = jnp.where(is1, 1.0, 0.0)
    chosen = chosen_ref[...]
    tm = chosen.shape[1]
    before = (lax.broadcasted_iota(jnp.int32, (tm, tm), 0) < lax.broadcasted_iota(jnp.int32, (tm, tm), 1))
    rank = _dot(_bf16(chosen), jnp.where(before, 1.0, 0.0).astype(jnp.bfloat16)).astype(jnp.int32)
    slot_ref[...] = jnp.where(chosen > 0.0, rank, -1)
    expert = lax.broadcasted_iota(jnp.int32, chosen.shape, 0)
    parked = jnp.logical_and(chosen > 0.0, rank < MOE_SLOTS)
    code = jnp.where(parked, expert * MOE_SLOTS + rank + 1, 0)
    is_first = first_ref[...] > 0.0
    dest_ref[0:1, :] = jnp.sum(jnp.where(is_first, code, 0), axis=0, keepdims=True) - 1
    dest_ref[1:2, :] = jnp.sum(jnp.where(is_first, 0, code), axis=0, keepdims=True) - 1


def _router(h2, rw_pad, rb_col):
    T = h2.shape[0]
    tm = MOE_SUB
    tok = lambda rows: pl.BlockSpec((rows, tm), lambda i: (0, i))
    return pl.pallas_call(
        _router_kernel,
        grid=(T // tm,),
        in_specs=[pl.BlockSpec((tm, D_MODEL), lambda i: (i, 0)),
                  pl.BlockSpec(rw_pad.shape, lambda i: (0, 0)),
                  pl.BlockSpec(rb_col.shape, lambda i: (0, 0))],
        out_specs=[tok(N_EXPERTS), tok(N_EXPERTS), tok(2)],
        out_shape=[jax.ShapeDtypeStruct((N_EXPERTS, T), jnp.float32),
                   jax.ShapeDtypeStruct((N_EXPERTS, T), jnp.int32),
                   jax.ShapeDtypeStruct((2, T), jnp.int32)],
        scratch_shapes=[pltpu.VMEM((N_EXPERTS, tm), jnp.float32)] * 2,
        compiler_params=_cparams(("parallel",)),
        name="router",
    )(h2, rw_pad, rb_col)


def _moe_kernel(npass_ref, h_ref, slot_ref, slot_t_ref, dest_t_ref, gate_ref, p_ref, wg_ref, wu_ref, wd_ref,
                pg_ref, pp_ref, g_ref, b_ref, o_ref, over_ref, xb_ref, y_ref, *, alpha, n_sub):
    i = pl.program_id(0)
    step = pl.program_id(1)
    R, SUB = MOE_SLOTS, MOE_SUB

    @pl.when(step == 0)
    def _():
        xb_ref[...] = _bf16(h_ref[...])
        over_ref[...] = jnp.zeros(over_ref.shape, jnp.float32)

    def expert_pass(e, k, c):
        slot_row = slot_ref[pl.ds(e, 1), :]
        gate_row = gate_ref[pl.ds(e, 1), :]
        xs, gs = [], []
        for j in range(n_sub):
            sl = slot_row[:, j * SUB:(j + 1) * SUB] - c * R
            hit = lax.broadcasted_iota(jnp.int32, (R, SUB), 0) == sl
            onehot = jnp.where(hit, 1.0, 0.0).astype(jnp.bfloat16)
            xs.append(_bf16(_dot(onehot, xb_ref[j * SUB:(j + 1) * SUB, :])))
            gs.append(jnp.sum(jnp.where(hit, gate_row[:, j * SUB:(j + 1) * SUB], 0.0), axis=1, keepdims=True))
        xe = jnp.concatenate(xs, axis=0)
        hid = jax.nn.silu(_dot(xe, wg_ref[k])) * _dot(xe, wu_ref[k])
        y = _dot(_bf16(hid), wd_ref[k])
        return [_bf16(y[j * R:(j + 1) * R] * gs[j]) for j in range(n_sub)]

    for k in range(MOE_EXPERTS_PER_STEP):
        e = step * MOE_EXPERTS_PER_STEP + k
        ys = expert_pass(e, k, 0)
        for j in range(n_sub):
            y_ref[j, pl.ds(pl.multiple_of(e * R, 16), R), :] = ys[j]

    for k in range(MOE_EXPERTS_PER_STEP):
        e = step * MOE_EXPERTS_PER_STEP + k

        def overflow_pass(c, carry, e=e, k=k):
            ys = expert_pass(e, k, c)
            lane16 = lax.broadcasted_iota(jnp.int32, (SUB, N_EXPERTS), 1)
            for j in range(n_sub):
                rows = slice(j * SUB, (j + 1) * SUB)
                slot_col = jnp.sum(jnp.where(lane16 == e, slot_t_ref[rows, :], 0), axis=1, keepdims=True)
                hit_t = lax.broadcasted_iota(jnp.int32, (SUB, R), 1) == slot_col - c * R
                over_ref[rows, :] += _dot(jnp.where(hit_t, 1.0, 0.0).astype(jnp.bfloat16), ys[j])
            return carry

        lax.fori_loop(1, npass_ref[i * N_EXPERTS + e], overflow_pass, 0)

    @pl.when(step == N_EXPERTS // MOE_EXPERTS_PER_STEP - 1)
    def _():
        lane = lax.broadcasted_iota(jnp.int32, (SUB, N_EXPERTS * R), 1)
        for j in range(n_sub):
            rows = slice(j * SUB, (j + 1) * SUB)
            dest = dest_t_ref[rows, :]
            hit = jnp.logical_or(lane == dest[:, 0:1], lane == dest[:, 1:2])
            scatter = jnp.where(hit, 1.0, 0.0).astype(jnp.bfloat16)
            ffn = _dot(scatter, y_ref[j]) + over_ref[rows, :]
            h = h_ref[rows, :]
            ple = jax.nn.sigmoid(_dot(xb_ref[rows, :], pg_ref[...])) * _dot(_bf16(p_ref[rows, :]), pp_ref[...])
            o_ref[rows, :] = _layer_norm(alpha * h + ffn + ple, g_ref[...], b_ref[...])


def _moe(h2, slot, dest, gate, p2, wg, wu, wd, layer, pg, pp, g, b, alpha, tm=1024):
    T = h2.shape[0]
    n_sub = tm // MOE_SUB
    n_tiles = T // tm
    per = MOE_EXPERTS_PER_STEP
    count = jnp.sum((slot >= 0).reshape(N_EXPERTS, n_tiles, n_sub, MOE_SUB), axis=3)
    npass = jnp.maximum(1, (jnp.max(count, axis=2) + MOE_SLOTS - 1) // MOE_SLOTS).T.reshape(-1).astype(jnp.int32)
    full = lambda a: pl.BlockSpec(a.shape, lambda i, e, *_: (0, 0))
    experts = lambda a: pl.BlockSpec((None, per) + a.shape[2:], lambda i, e, *_: (layer, e, 0, 0))
    grid_spec = pltpu.PrefetchScalarGridSpec(
        num_scalar_prefetch=1,
        grid=(n_tiles, N_EXPERTS // per),
        in_specs=[pl.BlockSpec((tm, D_MODEL), lambda i, e, *_: (i, 0)),
                  pl.BlockSpec((N_EXPERTS, tm), lambda i, e, *_: (0, i)),
                  pl.BlockSpec((tm, N_EXPERTS), lambda i, e, *_: (i, 0)),
                  pl.BlockSpec((tm, 2), lambda i, e, *_: (i, 0)),
                  pl.BlockSpec((N_EXPERTS, tm), lambda i, e, *_: (0, i)),
                  pl.BlockSpec((tm, PLE_DIM), lambda i, e, *_: (i, 0)),
                  experts(wg), experts(wu), experts(wd),
                  full(pg), full(pp), full(g), full(b)],
        out_specs=pl.BlockSpec((tm, D_MODEL), lambda i, e, *_: (i, 0)),
        scratch_shapes=[pltpu.VMEM((tm, D_MODEL), jnp.float32),
                        pltpu.VMEM((tm, D_MODEL), jnp.bfloat16),
                        pltpu.VMEM((n_sub, N_EXPERTS * MOE_SLOTS, D_MODEL), jnp.bfloat16)],
    )
    return pl.pallas_call(
        functools.partial(_moe_kernel, alpha=alpha, n_sub=n_sub),
        grid_spec=grid_spec,
        out_shape=jax.ShapeDtypeStruct((T, D_MODEL), jnp.float32),
        compiler_params=_cparams(("parallel", "arbitrary")),
        name="moe_ple_ln",
    )(npass, h2, slot, slot.T, dest.T, gate, p2, wg, wu, wd, pg, pp, g, b)


def _rotate_half_cols(w):
    half = w.shape[1] // 2
    return jnp.concatenate([-w[:, half:], w[:, :half]], axis=1)


def _in_proj_weights(w_in):
    z = lambda n: jnp.zeros((D_MODEL, n), w_in.dtype)
    cq = w_in[:, :MLA_Q_RANK]
    ckv = w_in[:, MLA_Q_RANK:MLA_Q_RANK + MLA_KV_RANK]
    kr = w_in[:, MLA_Q_RANK + MLA_KV_RANK:MLA_IN]
    n_dq = DIFF_HEADS * 2 * DIFF_QK
    sb0 = MLA_IN + DIFF_IN
    n_sq = SB_HEADS * SB_D
    cols = [ckv,
            z(MLA_NOPE), kr, z(128 - MLA_NOPE - MLA_ROPE),
            z(MLA_NOPE), _rotate_half_cols(kr), z(128 - MLA_NOPE - MLA_ROPE),
            cq, z(256 - MLA_Q_RANK),
            w_in[:, MLA_IN:MLA_IN + n_dq] * (DIFF_QK ** -0.5 * LOG2E),
            w_in[:, MLA_IN + n_dq:sb0],
            w_in[:, sb0:sb0 + n_sq] * (SB_D ** -0.5),
            w_in[:, sb0 + n_sq:]]
    return _bf16(jnp.concatenate(cols, axis=1))


def _mla_up_weights(w_uq, w_ukv):
    dq = MLA_NOPE + MLA_ROPE
    zq = lambda n: jnp.zeros((MLA_Q_RANK, n), w_uq.dtype)
    plain, rot = [], []
    for h in range(MLA_HEADS):
        wh = w_uq[:, h * dq:(h + 1) * dq]
        plain += [wh, zq(128 - dq)]
        rot += [zq(MLA_NOPE), _rotate_half_cols(wh[:, MLA_NOPE:]), zq(128 - dq)]
    wq = jnp.concatenate(plain + rot, axis=1)
    wq = jnp.concatenate([wq, jnp.zeros((256 - MLA_Q_RANK, wq.shape[1]), wq.dtype)], axis=0)
    dkv = MLA_NOPE + MLA_V
    zk = lambda n: jnp.zeros((MLA_KV_RANK, n), w_ukv.dtype)
    kcols, vcols = [], []
    for h in range(MLA_HEADS):
        wh = w_ukv[:, h * dkv:(h + 1) * dkv]
        kcols += [wh[:, :MLA_NOPE], zk(128 - MLA_NOPE)]
        vcols += [wh[:, MLA_NOPE:]]
    wkv = jnp.concatenate(kcols + vcols, axis=1)
    return _bf16(wq), _bf16(wkv)


def _attn_tiles(S):
    pick = lambda want: max(c for c in (128, 256, 512, 1024) if c <= want and S % c == 0)
    return pick(512), pick(512), pick(256)


def kernel(x, p, positions, w_in, mla_q_norm, mla_w_uq, mla_kv_norm, mla_w_ukv, diff_lambda_q1, diff_lambda_k1, diff_lambda_q2, diff_lambda_k2, diff_subln, rel_bias, w_o, ln1_g, ln1_b, router_w, router_b, w_gate, w_up, w_down, ple_proj, ple_gate, ln2_g, ln2_b):
    B, S, _ = x.shape
    depth = w_in.shape[0]
    T = B * S
    alpha = (2 * depth) ** 0.25
    t_mla, t_diff, t_sb = _attn_tiles(S)
    nq = S // t_diff

    pos_col = positions.reshape(T, 1)
    pos_tiles = positions.reshape(B * nq, 1, t_diff)
    tile_pos = positions.reshape(B * nq, t_diff)
    qmin = jnp.min(tile_pos, axis=1)
    kmax = jnp.max(tile_pos, axis=1)
    first = tile_pos[:, 0]
    consec = jnp.all(tile_pos[:, 1:] - tile_pos[:, :-1] == 1, axis=1).astype(jnp.int32)
    rb_flat = rel_bias.T.reshape(-1).astype(jnp.float32)

    half = MLA_ROPE // 2
    inv = ROPE_THETA ** (-jnp.arange(half, dtype=jnp.float32) / half)
    cos_t, sin_t = _rope_table(positions.reshape(1, T), jnp.concatenate([inv, inv]).reshape(MLA_ROPE, 1))
    pad_l = lambda v: jnp.full((T, MLA_NOPE), v, jnp.float32)
    pad_r = jnp.zeros((T, LANES - MLA_NOPE - MLA_ROPE), jnp.float32)
    cosf = jnp.concatenate([pad_l(1.0), cos_t.T, pad_r], axis=1)
    sinf = jnp.concatenate([pad_l(0.0), sin_t.T, pad_r], axis=1)

    rw_pad = jnp.pad(router_w.astype(jnp.float32), ((0, 0), (0, LANES - N_EXPERTS)))
    rb_col = router_b.reshape(N_EXPERTS, 1).astype(jnp.float32)
    row = lambda a: a.reshape(1, -1)

    wg_all, wu_all, wd_all = _bf16(w_gate), _bf16(w_up), _bf16(w_down)

    h = x.reshape(T, D_MODEL)
    for i in range(depth):
        lam_init = 0.8 - 0.6 * math.exp(-0.3 * i)
        ua, ub = _in_proj(h, _in_proj_weights(w_in[i]))
        wq, wkv = _mla_up_weights(mla_w_uq[i], mla_w_ukv[i])
        gq = jnp.concatenate([mla_q_norm[i], jnp.zeros((256 - MLA_Q_RANK,), mla_q_norm.dtype)]).reshape(1, 256)
        q_m, k_m, v_m = _mla_prep(ua, cosf, sinf, gq, row(mla_kv_norm[i]), wq, wkv)
        y_mla = _mla_attn(q_m, k_m, v_m, B, S, t=t_mla)
        y_diff = _diff_attn(ub, pos_col, pos_tiles, qmin, kmax, first, consec, rb_flat,
                            row(diff_lambda_q1[i]), row(diff_lambda_k1[i]),
                            row(diff_lambda_q2[i]), row(diff_lambda_k2[i]), row(diff_subln[i]),
                            B, S, lam_init, t=t_diff)
        y_sb = _sb_attn(ub, B, S, t=t_sb)
        wo = _bf16(w_o[i])
        n_m, n_d = MLA_HEADS * MLA_V, DIFF_HEADS * DIFF_V
        h = _out_proj(h, y_mla, y_diff, y_sb, wo[:n_m], wo[n_m:n_m + n_d], wo[n_m + n_d:],
                      row(ln1_g[i]), row(ln1_b[i]), alpha)
        gate, slot, dest = _router(h, rw_pad, rb_col)
        h = _moe(h, slot, dest, gate, p[i].reshape(T, PLE_DIM), wg_all, wu_all, wd_all, i,
                 _bf16(ple_gate[i]), _bf16(ple_proj[i]), row(ln2_g[i]), row(ln2_b[i]), alpha)
    return h.reshape(B, S, D_MODEL)
```

```python
import functools
import math

import jax
import jax.numpy as jnp
from jax import lax
from jax.experimental import pallas as pl
from jax.experimental.pallas import tpu as pltpu

D_MODEL = 1024
CHUNK = 64
PLE_DIM = 256
MLA_HEADS = 4
MLA_NOPE = 64
MLA_ROPE = 32
MLA_V = 64
MLA_Q_RANK = 192
MLA_KV_RANK = 128
ROPE_THETA = 10000.0
DIFF_HEADS = 4
DIFF_QK = 64
DIFF_V = 2 * DIFF_QK
SB_HEADS = 4
SB_D = 64
REL_BUCKETS = 32
REL_MAX_DIST = 128
N_EXPERTS = 16
N_GROUPS = 4
EXPERTS_PER_GROUP = N_EXPERTS // N_GROUPS
D_EXPERT = 512
MLA_IN = MLA_Q_RANK + MLA_KV_RANK + MLA_ROPE
DIFF_IN = 2 * DIFF_HEADS * 2 * DIFF_QK + DIFF_HEADS * DIFF_V
SB_IN = 3 * SB_HEADS * SB_D
EPS = 1e-5
NEG_INF = -1e30

LANES = 128
VMEM_LIMIT = 56 * 1024 * 1024

UA_CKV = 0
UA_KR = 128
UA_KRS = 256
UA_CQ = 384
UA_W = 640
UB_W = DIFF_IN + SB_IN

T5_FAR = 91
SB_LOG_ZERO = -88.0
DIFF_HEADS_PER_STEP = 2
FAR_TILE_WIDTHS = (4, 2)
MOE_SUB = 512
MOE_SLOTS = 96
MOE_EXPERTS_PER_STEP = 2

_NT = (((1,), (1,)), ((), ()))
LOG2E = math.log2(math.e)


def _cparams(sem):
    return pltpu.CompilerParams(dimension_semantics=sem, vmem_limit_bytes=VMEM_LIMIT)


def _bf16(a):
    return a.astype(jnp.bfloat16)


def _dot(a, b):
    return jnp.dot(a, b, preferred_element_type=jnp.float32)


def _in_proj_kernel(x_ref, w_ref, ua_ref, ub_ref):
    x = _bf16(x_ref[...])
    step = 512
    for c in range(0, UA_W, step):
        e = min(c + step, UA_W)
        ua_ref[:, c:e] = _dot(x, w_ref[:, c:e])
    for c in range(0, UB_W, step):
        e = min(c + step, UB_W)
        ub_ref[:, c:e] = _bf16(_dot(x, w_ref[:, UA_W + c:UA_W + e]))


def _in_proj(h2, w_all, tm=512):
    T = h2.shape[0]
    return pl.pallas_call(
        _in_proj_kernel,
        grid=(T // tm,),
        in_specs=[pl.BlockSpec((tm, D_MODEL), lambda i: (i, 0)),
                  pl.BlockSpec((D_MODEL, UA_W + UB_W), lambda i: (0, 0))],
        out_specs=[pl.BlockSpec((tm, UA_W), lambda i: (i, 0)),
                   pl.BlockSpec((tm, UB_W), lambda i: (i, 0))],
        out_shape=[jax.ShapeDtypeStruct((T, UA_W), jnp.float32),
                   jax.ShapeDtypeStruct((T, UB_W), jnp.bfloat16)],
        compiler_params=_cparams(("parallel",)),
        name="in_proj",
    )(h2, w_all)


def _rope_table_kernel(pos_ref, inv_ref, cos_ref, sin_ref):
    ang = pos_ref[...].astype(jnp.float32) * inv_ref[...]
    cos_ref[...] = jnp.cos(ang)
    sin_ref[...] = jnp.sin(ang)


def _rope_table(pos_row, inv_col, tn=2048):
    T = pos_row.shape[1]
    n = inv_col.shape[0]
    tn = min(tn, T)
    return pl.pallas_call(
        _rope_table_kernel,
        grid=(T // tn,),
        in_specs=[pl.BlockSpec((1, tn), lambda i: (0, i)),
                  pl.BlockSpec((n, 1), lambda i: (0, 0))],
        out_specs=[pl.BlockSpec((n, tn), lambda i: (0, i))] * 2,
        out_shape=[jax.ShapeDtypeStruct((n, T), jnp.float32)] * 2,
        compiler_params=_cparams(("parallel",)),
        name="rope_table",
    )(pos_row, inv_col)


def _mla_prep_kernel(ua_ref, cos_ref, sin_ref, gq_ref, gkv_ref, wq_ref, wkv_ref,
                     q_ref, k_ref, v_ref):
    ckv = ua_ref[:, UA_CKV:UA_CKV + 128]
    kr = ua_ref[:, UA_KR:UA_KR + 128]
    krs = ua_ref[:, UA_KRS:UA_KRS + 128]
    cq = ua_ref[:, UA_CQ:UA_CQ + 256]
    cqn = cq * lax.rsqrt(jnp.sum(cq * cq, -1, keepdims=True) * (1.0 / MLA_Q_RANK) + EPS) * gq_ref[...]
    ckvn = ckv * lax.rsqrt(jnp.sum(ckv * ckv, -1, keepdims=True) * (1.0 / MLA_KV_RANK) + EPS) * gkv_ref[...]
    cosf = cos_ref[...]
    sinf = sin_ref[...]
    scale = (MLA_NOPE + MLA_ROPE) ** -0.5 * LOG2E
    cqb = _bf16(cqn)
    ckvb = _bf16(ckvn)
    k_rope = kr * cosf + krs * sinf
    for h in range(MLA_HEADS):
        a = _dot(cqb, wq_ref[:, h * 128:(h + 1) * 128])
        b = _dot(cqb, wq_ref[:, 512 + h * 128:512 + (h + 1) * 128])
        q_ref[:, h * 128:(h + 1) * 128] = _bf16((a * cosf + b * sinf) * scale)
        kn = _dot(ckvb, wkv_ref[:, h * 128:(h + 1) * 128])
        k_ref[:, h * 128:(h + 1) * 128] = _bf16(kn + k_rope)
    v_ref[...] = _bf16(_dot(ckvb, wkv_ref[:, 512:768]))


def _mla_prep(ua, cosf, sinf, gq, gkv, wq, wkv, tm=512):
    T = ua.shape[0]
    full = lambda a: pl.BlockSpec(a.shape, lambda i: (0, 0))
    return pl.pallas_call(
        _mla_prep_kernel,
        grid=(T // tm,),
        in_specs=[pl.BlockSpec((tm, UA_W), lambda i: (i, 0)),
                  pl.BlockSpec((tm, LANES), lambda i: (i, 0)),
                  pl.BlockSpec((tm, LANES), lambda i: (i, 0)),
                  full(gq), full(gkv), full(wq), full(wkv)],
        out_specs=[pl.BlockSpec((tm, 512), lambda i: (i, 0)),
                   pl.BlockSpec((tm, 512), lambda i: (i, 0)),
                   pl.BlockSpec((tm, 256), lambda i: (i, 0))],
        out_shape=[jax.ShapeDtypeStruct((T, 512), jnp.bfloat16),
                   jax.ShapeDtypeStruct((T, 512), jnp.bfloat16),
                   jax.ShapeDtypeStruct((T, 256), jnp.bfloat16)],
        compiler_params=_cparams(("parallel",)),
        name="mla_prep",
    )(ua, cosf, sinf, gq, gkv, wq, wkv)


def _chunk_mask(tq, tk):
    qc = lax.broadcasted_iota(jnp.int32, (tq, tk), 0) // CHUNK
    kc = lax.broadcasted_iota(jnp.int32, (tq, tk), 1) // CHUNK
    return kc <= qc


def _softmax_update(s, v, m_ref, l_ref, acc_ref, idx, shift=None):
    tk = s.shape[1]
    m_old = m_ref[idx]
    row_max = jnp.max(s, axis=1, keepdims=True)
    if shift is not None:
        row_max = row_max + shift
    m_new = jnp.maximum(m_old, row_max)
    alpha = jnp.exp2(m_old - m_new)
    m_sub = m_new if shift is None else m_new - shift
    p = jnp.exp2(s - jnp.concatenate([m_sub] * (tk // LANES), axis=1))
    psum = p[:, :LANES]
    for c in range(LANES, tk, LANES):
        psum = psum + p[:, c:c + LANES]
    l_ref[idx] = alpha * l_ref[idx] + psum
    acc_ref[idx] = alpha * acc_ref[idx] + _dot(_bf16(p), v)
    m_ref[idx] = m_new


def _softmax_init(m_ref, l_ref, acc_ref):
    m_ref[...] = jnp.full(m_ref.shape, NEG_INF, jnp.float32)
    l_ref[...] = jnp.zeros(l_ref.shape, jnp.float32)
    acc_ref[...] = jnp.zeros(acc_ref.shape, jnp.float32)


def _softmax_result(l_ref, acc_ref, idx):
    return acc_ref[idx] / jnp.sum(l_ref[idx], axis=1, keepdims=True)


def _mla_attn_kernel(q_ref, k_ref, v_ref, o_ref, m_ref, l_ref, acc_ref, *, t):
    qi = pl.program_id(1)
    _softmax_init(m_ref, l_ref, acc_ref)

    def tile(j, masked, width=1):
        ks = pl.multiple_of(j * t, t)
        for hh in range(MLA_HEADS):
            q = q_ref[:, hh * 128:(hh + 1) * 128]
            k = k_ref[pl.ds(ks, width * t), hh * 128:(hh + 1) * 128]
            v = v_ref[pl.ds(ks, width * t), (hh // 2) * 128:(hh // 2 + 1) * 128]
            s = lax.dot_general(q, k, _NT, preferred_element_type=jnp.float32)
            if masked:
                s = jnp.where(_chunk_mask(t, t), s, NEG_INF)
            _softmax_update(s, v, m_ref, l_ref, acc_ref, hh)

    done = 0
    for width in FAR_TILE_WIDTHS + (1,):
        trips = (qi - done) // width

        def body(i, carry, width=width, done=done):
            tile(done + i * width, False, width=width)
            return carry

        lax.fori_loop(0, trips, body, 0)
        done = done + trips * width

    tile(qi, True)
    lane = lax.broadcasted_iota(jnp.int32, (t, 128), 1)
    for pair in range(MLA_HEADS // 2):
        o_ref[:, pair * 128:(pair + 1) * 128] = _bf16(
            jnp.where(lane < MLA_V, _softmax_result(l_ref, acc_ref, 2 * pair),
                      _softmax_result(l_ref, acc_ref, 2 * pair + 1)))


def _mla_attn(q, k, v, B, S, t=256):
    nq = S // t
    wq, wv = MLA_HEADS * 128, MLA_HEADS * MLA_V
    return pl.pallas_call(
        functools.partial(_mla_attn_kernel, t=t),
        grid=(B, nq),
        in_specs=[pl.BlockSpec((t, wq), lambda b, i: (b * nq + i, 0)),
                  pl.BlockSpec((S, wq), lambda b, i: (b, 0)),
                  pl.BlockSpec((S, wv), lambda b, i: (b, 0))],
        out_specs=pl.BlockSpec((t, wv), lambda b, i: (b * nq + i, 0)),
        out_shape=jax.ShapeDtypeStruct((B * S, wv), jnp.bfloat16),
        scratch_shapes=[pltpu.VMEM((MLA_HEADS, t, LANES), jnp.float32)] * 3,
        compiler_params=_cparams(("parallel", "arbitrary")),
        name="mla_attn",
    )(q, k, v)


def _t5_bias(rel, rb_ref, h):
    nb = REL_BUCKETS // 2
    max_exact = nb // 2
    n = jnp.abs(rel)
    nf = jnp.maximum(n, 1).astype(jnp.float32)
    large = max_exact + (jnp.log(nf / max_exact) / math.log(REL_MAX_DIST / max_exact)
                         * (nb - max_exact)).astype(jnp.int32)
    large = jnp.minimum(large, nb - 1)
    low = jnp.where(n < max_exact, n, large)
    neg = jnp.zeros(rel.shape, jnp.float32)
    pos = jnp.zeros(rel.shape, jnp.float32)
    for j in range(nb):
        eq = low == j
        neg = jnp.where(eq, rb_ref[h * REL_BUCKETS + j], neg)
        pos = jnp.where(eq, rb_ref[h * REL_BUCKETS + nb + j], pos)
    return jnp.where(rel > 0, pos, neg)


def _diff_attn_kernel(qmin_ref, kmax_ref, first_ref, consec_ref,
                      rb_ref, q_ref, k_ref, v_ref, posq_ref, posk_ref,
                      lq1_ref, lk1_ref, lq2_ref, lk2_ref, sub_ref,
                      o_ref, m_ref, l_ref, acc_ref, *, t, nq, lam_init):
    b = pl.program_id(0)
    hp = pl.program_id(1)
    qi = pl.program_id(2)
    _softmax_init(m_ref, l_ref, acc_ref)

    lane = lax.broadcasted_iota(jnp.int32, (t, LANES), 1)
    qs = []
    for hh in range(DIFF_HEADS_PER_STEP):
        q = q_ref[:, hh * LANES:(hh + 1) * LANES]
        zero = jnp.zeros_like(q)
        qs.append(jnp.concatenate([jnp.where(lane < DIFF_QK, q, zero), jnp.where(lane < DIFF_QK, zero, q)], axis=0))
    tq = b * nq + qi
    qmin = qmin_ref[tq]

    def toeplitz_bias(j, h):
        d0 = first_ref[b * nq + j] - first_ref[tq]
        x = lax.broadcasted_iota(jnp.int32, (1, 2 * t), 1)
        g = _t5_bias(d0 + jnp.where(x < t, x, x - 2 * t), rb_ref, h)
        g = pltpu.roll(jnp.broadcast_to(g, (t, 2 * t)), 0, 1, stride=1, stride_axis=0)
        return g[:, :t]

    def general_bias(j, h):
        rel = posk_ref[j] - posq_ref[...]
        return _t5_bias(rel, rb_ref, h)

    def tile(j, masked, bias_fn, width=1):
        ks = pl.multiple_of(j * t, t)
        for hh in range(DIFF_HEADS_PER_STEP):
            h = hp * DIFF_HEADS_PER_STEP + hh
            k = k_ref[pl.ds(ks, width * t), hh * LANES:(hh + 1) * LANES]
            v = v_ref[pl.ds(ks, width * t), hh * LANES:(hh + 1) * LANES]
            s = lax.dot_general(qs[hh], k, _NT, preferred_element_type=jnp.float32)
            if bias_fn is None:
                shift = rb_ref[h * REL_BUCKETS + REL_BUCKETS // 2 - 1] * LOG2E
            else:
                shift = None
                s = s.reshape(2, t, t) + (bias_fn(j, h) * LOG2E)[None]
                if masked:
                    s = jnp.where(_chunk_mask(t, t)[None], s, NEG_INF)
                s = s.reshape(2 * t, t)
            _softmax_update(s, v, m_ref, l_ref, acc_ref, hh, shift)

    def near_tile(j, masked):
        consecutive = jnp.logical_and(consec_ref[tq] == 1, consec_ref[b * nq + j] == 1)

        @pl.when(consecutive)
        def _():
            tile(j, masked, toeplitz_bias)

        @pl.when(jnp.logical_not(consecutive))
        def _():
            tile(j, masked, general_bias)

    def is_far(j):
        return kmax_ref[b * nq + jnp.minimum(j, qi)] - qmin <= -T5_FAR

    def wide_far_loop(start, width):
        def cond(j):
            ok = j + width - 1 < qi
            for d in range(width):
                ok = jnp.logical_and(ok, is_far(j + d))
            return ok

        def step(j):
            tile(j, False, None, width=width)
            return j + width

        return lax.while_loop(cond, step, start)

    def body(j, carry):
        far = is_far(j)

        @pl.when(far)
        def _():
            tile(j, False, None)

        @pl.when(jnp.logical_not(far))
        def _():
            near_tile(j, False)

        return carry

    done = 0
    for width in FAR_TILE_WIDTHS:
        done = wide_far_loop(done, width)
    lax.fori_loop(done, qi, body, 0)
    near_tile(qi, True)

    f32 = jnp.float32
    lam = (jnp.exp(jnp.sum(lq1_ref[...].astype(f32) * lk1_ref[...].astype(f32), keepdims=True))
           - jnp.exp(jnp.sum(lq2_ref[...].astype(f32) * lk2_ref[...].astype(f32), keepdims=True))
           + lam_init)
    for hh in range(DIFF_HEADS_PER_STEP):
        a = _softmax_result(l_ref, acc_ref, hh)
        o = a[:t] - lam * a[t:]
        o = o * lax.rsqrt(jnp.mean(o * o, -1, keepdims=True) + EPS) * sub_ref[...] * (1.0 - lam_init)
        o_ref[:, hh * LANES:(hh + 1) * LANES] = _bf16(o)


def _diff_attn(ub, pos_col, pos_tiles, qmin, kmax, first, consec, rb_flat, lq1, lk1, lq2, lk2, subln,
               B, S, lam_init, t=256):
    nq = S // t
    per = DIFF_HEADS_PER_STEP
    G = DIFF_HEADS // per
    small = lambda a: pl.BlockSpec(a.shape, lambda b, h, i, *_: (0, 0))
    grid_spec = pltpu.PrefetchScalarGridSpec(
        num_scalar_prefetch=4,
        grid=(B, G, nq),
        in_specs=[pl.BlockSpec(memory_space=pltpu.SMEM),
                  pl.BlockSpec((t, per * 128), lambda b, h, i, *_: (b * nq + i, h)),
                  pl.BlockSpec((S, per * 128), lambda b, h, i, *_: (b, G + h)),
                  pl.BlockSpec((S, per * 128), lambda b, h, i, *_: (b, 2 * G + h)),
                  pl.BlockSpec((t, 1), lambda b, h, i, *_: (b * nq + i, 0)),
                  pl.BlockSpec((nq, 1, t), lambda b, h, i, *_: (b, 0, 0)),
                  small(lq1), small(lk1), small(lq2), small(lk2), small(subln)],
        out_specs=pl.BlockSpec((t, per * 128), lambda b, h, i, *_: (b * nq + i, h)),
        scratch_shapes=[pltpu.VMEM((per, 2 * t, LANES), jnp.float32)] * 3,
    )
    return pl.pallas_call(
        functools.partial(_diff_attn_kernel, t=t, nq=nq, lam_init=lam_init),
        grid_spec=grid_spec,
        out_shape=jax.ShapeDtypeStruct((B * S, DIFF_HEADS * DIFF_V), jnp.bfloat16),
        compiler_params=_cparams(("parallel", "parallel", "arbitrary")),
        name="diff_attn",
    )(qmin, kmax, first, consec, rb_flat, ub, ub, ub, pos_col, pos_tiles, lq1, lk1, lq2, lk2, subln)


def _sb_attn_kernel(q_ref, k_ref, v_ref, o_ref, c_ref, acc_ref, *, t):
    qi = pl.program_id(1)
    n_pairs = SB_HEADS // 2
    row = lax.broadcasted_iota(jnp.int32, (t, t), 0)
    col = lax.broadcasted_iota(jnp.int32, (t, t), 1)
    tri = jnp.where(row > col, 1.0, 0.0).astype(jnp.bfloat16)
    strict = (col < row)[None]
    lane = lax.broadcasted_iota(jnp.int32, (t, LANES), 1)
    qs = []
    for hp in range(n_pairs):
        q = q_ref[:, hp * LANES:(hp + 1) * LANES]
        zero = jnp.zeros_like(q)
        qs.append(jnp.concatenate([jnp.where(lane < SB_D, q, zero), jnp.where(lane < SB_D, zero, q)], axis=0))
    c_ref[...] = jnp.zeros(c_ref.shape, jnp.float32)
    acc_ref[...] = jnp.zeros(acc_ref.shape, jnp.float32)

    def tile(j, diag):
        ks = pl.multiple_of(j * t, t)
        for hp in range(n_pairs):
            k = k_ref[pl.ds(ks, t), hp * LANES:(hp + 1) * LANES]
            v = v_ref[pl.ds(ks, t), hp * LANES:(hp + 1) * LANES]
            z = lax.dot_general(qs[hp], k, _NT, preferred_element_type=jnp.float32)
            lf = -(jnp.maximum(z, 0.0) + jnp.log(1.0 + jnp.exp(-jnp.abs(z))))
            if diag:
                lf = jnp.where(strict, lf.reshape(2, t, t), 0.0).reshape(2 * t, t)
            hi = _bf16(lf)
            lo = _bf16(lf - hi.astype(jnp.float32))
            c = c_ref[hp]
            later = _dot(hi, tri) + _dot(lo, tri) + jnp.concatenate([c] * (t // LANES), axis=1)
            w = jnp.exp(lf + z + later)
            if diag:
                w = jnp.where(strict, w.reshape(2, t, t), 0.0).reshape(2 * t, t)
            acc_ref[hp] += _dot(_bf16(w), v)
            c_ref[hp] = c + jnp.sum(lf, axis=1, keepdims=True)

    tile(qi, True)

    def cond(carry):
        j, cmax = carry
        return jnp.logical_and(j >= 0, cmax > SB_LOG_ZERO)

    def body(carry):
        j, _ = carry
        tile(j, False)
        return j - 1, jnp.max(c_ref[...])

    lax.while_loop(cond, body, (qi - 1, jnp.max(c_ref[...])))
    for hp in range(n_pairs):
        acc = acc_ref[hp]
        o_ref[:, hp * LANES:(hp + 1) * LANES] = _bf16(jnp.where(lane < SB_D, acc[:t], acc[t:]))


def _sb_attn(ub, B, S, t=256):
    nq = S // t
    w = SB_HEADS * SB_D
    c0 = DIFF_IN // w
    return pl.pallas_call(
        functools.partial(_sb_attn_kernel, t=t),
        grid=(B, nq),
        in_specs=[pl.BlockSpec((t, w), lambda b, i: (b * nq + i, c0)),
                  pl.BlockSpec((S, w), lambda b, i: (b, c0 + 1)),
                  pl.BlockSpec((S, w), lambda b, i: (b, c0 + 2))],
        out_specs=pl.BlockSpec((t, w), lambda b, i: (b * nq + i, 0)),
        out_shape=jax.ShapeDtypeStruct((B * S, w), jnp.bfloat16),
        scratch_shapes=[pltpu.VMEM((SB_HEADS // 2, 2 * t, LANES), jnp.float32)] * 2,
        compiler_params=_cparams(("parallel", "arbitrary")),
        name="sb_attn",
    )(ub, ub, ub)


def _layer_norm(x, g, b):
    mu = jnp.mean(x, -1, keepdims=True)
    xc = x - mu
    var = jnp.mean(xc * xc, -1, keepdims=True)
    return xc * lax.rsqrt(var + EPS) * g + b


def _out_proj_kernel(h_ref, ym_ref, yd_ref, ys_ref, wm_ref, wd_ref, ws_ref, g_ref, b_ref, o_ref, *, alpha):
    mix = _dot(ym_ref[...], wm_ref[...]) + _dot(yd_ref[...], wd_ref[...]) + _dot(ys_ref[...], ws_ref[...])
    o_ref[...] = _layer_norm(alpha * h_ref[...] + mix, g_ref[...], b_ref[...])


def _out_proj(h2, ym, yd, ys, wm, wd, ws, g, b, alpha, tm=512):
    T = h2.shape[0]
    row = lambda a: pl.BlockSpec((tm, a.shape[1]), lambda i: (i, 0))
    full = lambda a: pl.BlockSpec(a.shape, lambda i: (0, 0))
    return pl.pallas_call(
        functools.partial(_out_proj_kernel, alpha=alpha),
        grid=(T // tm,),
        in_specs=[row(h2), row(ym), row(yd), row(ys), full(wm), full(wd), full(ws), full(g), full(b)],
        out_specs=pl.BlockSpec((tm, D_MODEL), lambda i: (i, 0)),
        out_shape=jax.ShapeDtypeStruct((T, D_MODEL), jnp.float32),
        compiler_params=_cparams(("parallel",)),
        name="out_proj_ln",
    )(h2, ym, yd, ys, wm, wd, ws, g, b)


def _first_max(vals):
    m = vals[0]
    for v in vals[1:]:
        m = jnp.maximum(m, v)
    taken = jnp.zeros(m.shape, jnp.bool_)
    hot = []
    for v in vals:
        is_first = jnp.logical_and(v == m, jnp.logical_not(taken))
        hot.append(is_first)
        taken = jnp.logical_or(taken, is_first)
    return m, hot


def _router_kernel(h_ref, rw_ref, rb_ref, gate_ref, slot_ref, dest_ref, chosen_ref, first_ref):
    h = h_ref[...]
    rw = rw_ref[...]
    h_hi, rw_hi = _bf16(h), _bf16(rw)
    h_lo, rw_lo = _bf16(h - h_hi.astype(jnp.float32)), _bf16(rw - rw_hi.astype(jnp.float32))
    logits = (_dot(h_hi, rw_hi) + (_dot(h_hi, rw_lo) + _dot(h_lo, rw_hi))).T[:N_EXPERTS]
    scores = jax.nn.sigmoid(logits)
    sel = scores + rb_ref[...]
    ninf = -jnp.inf
    group_score, first, second = [], [], []
    for g in range(N_GROUPS):
        vals = [sel[g * EXPERTS_PER_GROUP + k:g * EXPERTS_PER_GROUP + k + 1, :] for k in range(EXPERTS_PER_GROUP)]
        m1, hot1 = _first_max(vals)
        m2, hot2 = _first_max([jnp.where(hh, ninf, v) for hh, v in zip(hot1, vals)])
        group_score.append(m1 + m2)
        first.append(hot1)
        second.append(hot2)
    _, best = _first_max(group_score)
    w1 = jnp.zeros_like(group_score[0])
    w2 = jnp.zeros_like(group_score[0])
    for g in range(N_GROUPS):
        for k in range(EXPERTS_PER_GROUP):
            e = g * EXPERTS_PER_GROUP + k
            sc = scores[e:e + 1, :]
            w1 = jnp.where(jnp.logical_and(best[g], first[g][k]), sc, w1)
            w2 = jnp.where(jnp.logical_and(best[g], second[g][k]), sc, w2)
    tot = w1 + w2
    for g in range(N_GROUPS):
        for k in range(EXPERTS_PER_GROUP):
            e = g * EXPERTS_PER_GROUP + k
            is1 = jnp.logical_and(best[g], first[g][k])
            is2 = jnp.logical_and(best[g], second[g][k])
            gate_ref[e:e + 1, :] = jnp.where(is1, w1 / tot, 0.0) + jnp.where(is2, w2 / tot, 0.0)
            chosen_ref[e:e + 1, :] = jnp.where(jnp.logical_or(is1, is2), 1.0, 0.0)
            first_ref[e:e + 1, :] = jnp.where(is1, 1.0, 0.0)
    chosen = chosen_ref[...]
    tm = chosen.shape[1]
    before = (lax.broadcasted_iota(jnp.int32, (tm, tm), 0) < lax.broadcasted_iota(jnp.int32, (tm, tm), 1))
    rank = _dot(_bf16(chosen), jnp.where(before, 1.0, 0.0).astype(jnp.bfloat16)).astype(jnp.int32)
    slot_ref[...] = jnp.where(chosen > 0.0, rank, -1)
    expert = lax.broadcasted_iota(jnp.int32, chosen.shape, 0)
    parked = jnp.logical_and(chosen > 0.0, rank < MOE_SLOTS)
    code = jnp.where(parked, expert * MOE_SLOTS + rank + 1, 0)
    is_first = first_ref[...] > 0.0
    dest_ref[0:1, :] = jnp.sum(jnp.where(is_first, code, 0), axis=0, keepdims=True) - 1
    dest_ref[1:2, :] = jnp.sum(jnp.where(is_first, 0, code), axis=0, keepdims=True) - 1


def _router(h2, rw_pad, rb_col):
    T = h2.shape[0]
    tm = MOE_SUB
    tok = lambda rows: pl.BlockSpec((rows, tm), lambda i: (0, i))
    return pl.pallas_call(
        _router_kernel,
        grid=(T // tm,),
        in_specs=[pl.BlockSpec((tm, D_MODEL), lambda i: (i, 0)),
                  pl.BlockSpec(rw_pad.shape, lambda i: (0, 0)),
                  pl.BlockSpec(rb_col.shape, lambda i: (0, 0))],
        out_specs=[tok(N_EXPERTS), tok(N_EXPERTS), tok(2)],
        out_shape=[jax.ShapeDtypeStruct((N_EXPERTS, T), jnp.float32),
                   jax.ShapeDtypeStruct((N_EXPERTS, T), jnp.int32),
                   jax.ShapeDtypeStruct((2, T), jnp.int32)],
        scratch_shapes=[pltpu.VMEM((N_EXPERTS, tm), jnp.float32)] * 2,
        compiler_params=_cparams(("parallel",)),
        name="router",
    )(h2, rw_pad, rb_col)


def _moe_kernel(npass_ref, h_ref, slot_ref, slot_t_ref, dest_t_ref, gate_ref, p_ref, wg_ref, wu_ref, wd_ref,
                pg_ref, pp_ref, g_ref, b_ref, o_ref, over_ref, xb_ref, y_ref, *, alpha, n_sub):
    i = pl.program_id(0)
    step = pl.program_id(1)
    R, SUB = MOE_SLOTS, MOE_SUB

    @pl.when(step == 0)
    def _():
        xb_ref[...] = _bf16(h_ref[...])
        over_ref[...] = jnp.zeros(over_ref.shape, jnp.float32)

    def expert_pass(e, k, c):
        slot_row = slot_ref[pl.ds(e, 1), :]
        gate_row = gate_ref[pl.ds(e, 1), :]
        xs, gs = [], []
        for j in range(n_sub):
            sl = slot_row[:, j * SUB:(j + 1) * SUB] - c * R
            hit = lax.broadcasted_iota(jnp.int32, (R, SUB), 0) == sl
            onehot = jnp.where(hit, 1.0, 0.0).astype(jnp.bfloat16)
            xs.append(_bf16(_dot(onehot, xb_ref[j * SUB:(j + 1) * SUB, :])))
            gs.append(jnp.sum(jnp.where(hit, gate_row[:, j * SUB:(j + 1) * SUB], 0.0), axis=1, keepdims=True))
        xe = jnp.concatenate(xs, axis=0)
        hid = jax.nn.silu(_dot(xe, wg_ref[k])) * _dot(xe, wu_ref[k])
        y = _dot(_bf16(hid), wd_ref[k])
        return [_bf16(y[j * R:(j + 1) * R] * gs[j]) for j in range(n_sub)]

    for k in range(MOE_EXPERTS_PER_STEP):
        e = step * MOE_EXPERTS_PER_STEP + k
        ys = expert_pass(e, k, 0)
        for j in range(n_sub):
            y_ref[j, pl.ds(pl.multiple_of(e * R, 16), R), :] = ys[j]

    for k in range(MOE_EXPERTS_PER_STEP):
        e = step * MOE_EXPERTS_PER_STEP + k

        def overflow_pass(c, carry, e=e, k=k):
            ys = expert_pass(e, k, c)
            lane16 = lax.broadcasted_iota(jnp.int32, (SUB, N_EXPERTS), 1)
            for j in range(n_sub):
                rows = slice(j * SUB, (j + 1) * SUB)
                slot_col = jnp.sum(jnp.where(lane16 == e, slot_t_ref[rows, :], 0), axis=1, keepdims=True)
                hit_t = lax.broadcasted_iota(jnp.int32, (SUB, R), 1) == slot_col - c * R
                over_ref[rows, :] += _dot(jnp.where(hit_t, 1.0, 0.0).astype(jnp.bfloat16), ys[j])
            return carry

        lax.fori_loop(1, npass_ref[i * N_EXPERTS + e], overflow_pass, 0)

    @pl.when(step == N_EXPERTS // MOE_EXPERTS_PER_STEP - 1)
    def _():
        lane = lax.broadcasted_iota(jnp.int32, (SUB, N_EXPERTS * R), 1)
        for j in range(n_sub):
            rows = slice(j * SUB, (j + 1) * SUB)
            dest = dest_t_ref[rows, :]
            hit = jnp.logical_or(lane == dest[:, 0:1], lane == dest[:, 1:2])
            scatter = jnp.where(hit, 1.0, 0.0).astype(jnp.bfloat16)
            ffn = _dot(scatter, y_ref[j]) + over_ref[rows, :]
            h = h_ref[rows, :]
            ple = jax.nn.sigmoid(_dot(xb_ref[rows, :], pg_ref[...])) * _dot(_bf16(p_ref[rows, :]), pp_ref[...])
            o_ref[rows, :] = _layer_norm(alpha * h + ffn + ple, g_ref[...], b_ref[...])


def _moe(h2, slot, dest, gate, p2, wg, wu, wd, layer, pg, pp, g, b, alpha, tm=1024):
    T = h2.shape[0]
    n_sub = tm // MOE_SUB
    n_tiles = T // tm
    per = MOE_EXPERTS_PER_STEP
    count = jnp.sum((slot >= 0).reshape(N_EXPERTS, n_tiles, n_sub, MOE_SUB), axis=3)
    npass = jnp.maximum(1, (jnp.max(count, axis=2) + MOE_SLOTS - 1) // MOE_SLOTS).T.reshape(-1).astype(jnp.int32)
    full = lambda a: pl.BlockSpec(a.shape, lambda i, e, *_: (0, 0))
    experts = lambda a: pl.BlockSpec((None, per) + a.shape[2:], lambda i, e, *_: (layer, e, 0, 0))
    grid_spec = pltpu.PrefetchScalarGridSpec(
        num_scalar_prefetch=1,
        grid=(n_tiles, N_EXPERTS // per),
        in_specs=[pl.BlockSpec((tm, D_MODEL), lambda i, e, *_: (i, 0)),
                  pl.BlockSpec((N_EXPERTS, tm), lambda i, e, *_: (0, i)),
                  pl.BlockSpec((tm, N_EXPERTS), lambda i, e, *_: (i, 0)),
                  pl.BlockSpec((tm, 2), lambda i, e, *_: (i, 0)),
                  pl.BlockSpec((N_EXPERTS, tm), lambda i, e, *_: (0, i)),
                  pl.BlockSpec((tm, PLE_DIM), lambda i, e, *_: (i, 0)),
                  experts(wg), experts(wu), experts(wd),
                  full(pg), full(pp), full(g), full(b)],
        out_specs=pl.BlockSpec((tm, D_MODEL), lambda i, e, *_: (i, 0)),
        scratch_shapes=[pltpu.VMEM((tm, D_MODEL), jnp.float32),
                        pltpu.VMEM((tm, D_MODEL), jnp.bfloat16),
                        pltpu.VMEM((n_sub, N_EXPERTS * MOE_SLOTS, D_MODEL), jnp.bfloat16)],
    )
    return pl.pallas_call(
        functools.partial(_moe_kernel, alpha=alpha, n_sub=n_sub),
        grid_spec=grid_spec,
        out_shape=jax.ShapeDtypeStruct((T, D_MODEL), jnp.float32),
        compiler_params=_cparams(("parallel", "arbitrary")),
        name="moe_ple_ln",
    )(npass, h2, slot, slot.T, dest.T, gate, p2, wg, wu, wd, pg, pp, g, b)


def _rotate_half_cols(w):
    half = w.shape[1] // 2
    return jnp.concatenate([-w[:, half:], w[:, :half]], axis=1)


def _in_proj_weights(w_in):
    z = lambda n: jnp.zeros((D_MODEL, n), w_in.dtype)
    cq = w_in[:, :MLA_Q_RANK]
    ckv = w_in[:, MLA_Q_RANK:MLA_Q_RANK + MLA_KV_RANK]
    kr = w_in[:, MLA_Q_RANK + MLA_KV_RANK:MLA_IN]
    n_dq = DIFF_HEADS * 2 * DIFF_QK
    sb0 = MLA_IN + DIFF_IN
    n_sq = SB_HEADS * SB_D
    cols = [ckv,
            z(MLA_NOPE), kr, z(128 - MLA_NOPE - MLA_ROPE),
            z(MLA_NOPE), _rotate_half_cols(kr), z(128 - MLA_NOPE - MLA_ROPE),
            cq, z(256 - MLA_Q_RANK),
            w_in[:, MLA_IN:MLA_IN + n_dq] * (DIFF_QK ** -0.5 * LOG2E),
            w_in[:, MLA_IN + n_dq:sb0],
            w_in[:, sb0:sb0 + n_sq] * (SB_D ** -0.5),
            w_in[:, sb0 + n_sq:]]
    return _bf16(jnp.concatenate(cols, axis=1))


def _mla_up_weights(w_uq, w_ukv):
    dq = MLA_NOPE + MLA_ROPE
    zq = lambda n: jnp.zeros((MLA_Q_RANK, n), w_uq.dtype)
    plain, rot = [], []
    for h in range(MLA_HEADS):
        wh = w_uq[:, h * dq:(h + 1) * dq]
        plain += [wh, zq(128 - dq)]
        rot += [zq(MLA_NOPE), _rotate_half_cols(wh[:, MLA_NOPE:]), zq(128 - dq)]
    wq = jnp.concatenate(plain + rot, axis=1)
    wq = jnp.concatenate([wq, jnp.zeros((256 - MLA_Q_RANK, wq.shape[1]), wq.dtype)], axis=0)
    dkv = MLA_NOPE + MLA_V
    zk = lambda n: jnp.zeros((MLA_KV_RANK, n), w_ukv.dtype)
    kcols, vcols = [], []
    for h in range(MLA_HEADS):
        wh = w_ukv[:, h * dkv:(h + 1) * dkv]
        kcols += [wh[:, :MLA_NOPE], zk(128 - MLA_NOPE)]
        vcols += [wh[:, MLA_NOPE:]]
    wkv = jnp.concatenate(kcols + vcols, axis=1)
    return _bf16(wq), _bf16(wkv)


def _attn_tiles(S):
    pick = lambda want: max(c for c in (128, 256, 512, 1024) if c <= want and S % c == 0)
    return pick(512), pick(512), pick(256)


def kernel(x, p, positions, w_in, mla_q_norm, mla_w_uq, mla_kv_norm, mla_w_ukv, diff_lambda_q1, diff_lambda_k1, diff_lambda_q2, diff_lambda_k2, diff_subln, rel_bias, w_o, ln1_g, ln1_b, router_w, router_b, w_gate, w_up, w_down, ple_proj, ple_gate, ln2_g, ln2_b):
    B, S, _ = x.shape
    depth = w_in.shape[0]
    T = B * S
    alpha = (2 * depth) ** 0.25
    t_mla, t_diff, t_sb = _attn_tiles(S)
    nq = S // t_diff

    pos_col = positions.reshape(T, 1)
    pos_tiles = positions.reshape(B * nq, 1, t_diff)
    tile_pos = positions.reshape(B * nq, t_diff)
    qmin = jnp.min(tile_pos, axis=1)
    kmax = jnp.max(tile_pos, axis=1)
    first = tile_pos[:, 0]
    consec = jnp.all(tile_pos[:, 1:] - tile_pos[:, :-1] == 1, axis=1).astype(jnp.int32)
    rb_flat = rel_bias.T.reshape(-1).astype(jnp.float32)

    half = MLA_ROPE // 2
    inv = ROPE_THETA ** (-jnp.arange(half, dtype=jnp.float32) / half)
    cos_t, sin_t = _rope_table(positions.reshape(1, T), jnp.concatenate([inv, inv]).reshape(MLA_ROPE, 1))
    pad_l = lambda v: jnp.full((T, MLA_NOPE), v, jnp.float32)
    pad_r = jnp.zeros((T, LANES - MLA_NOPE - MLA_ROPE), jnp.float32)
    cosf = jnp.concatenate([pad_l(1.0), cos_t.T, pad_r], axis=1)
    sinf = jnp.concatenate([pad_l(0.0), sin_t.T, pad_r], axis=1)

    rw_pad = jnp.pad(router_w.astype(jnp.float32), ((0, 0), (0, LANES - N_EXPERTS)))
    rb_col = router_b.reshape(N_EXPERTS, 1).astype(jnp.float32)
    row = lambda a: a.reshape(1, -1)

    wg_all, wu_all, wd_all = _bf16(w_gate), _bf16(w_up), _bf16(w_down)

    h = x.reshape(T, D_MODEL)
    for i in range(depth):
        lam_init = 0.8 - 0.6 * math.exp(-0.3 * i)
        ua, ub = _in_proj(h, _in_proj_weights(w_in[i]))
        wq, wkv = _mla_up_weights(mla_w_uq[i], mla_w_ukv[i])
        gq = jnp.concatenate([mla_q_norm[i], jnp.zeros((256 - MLA_Q_RANK,), mla_q_norm.dtype)]).reshape(1, 256)
        q_m, k_m, v_m = _mla_prep(ua, cosf, sinf, gq, row(mla_kv_norm[i]), wq, wkv)
        y_mla = _mla_attn(q_m, k_m, v_m, B, S, t=t_mla)
        y_diff = _diff_attn(ub, pos_col, pos_tiles, qmin, kmax, first, consec, rb_flat,
                            row(diff_lambda_q1[i]), row(diff_lambda_k1[i]),
                            row(diff_lambda_q2[i]), row(diff_lambda_k2[i]), row(diff_subln[i]),
                            B, S, lam_init, t=t_diff)
        y_sb = _sb_attn(ub, B, S, t=t_sb)
        wo = _bf16(w_o[i])
        n_m, n_d = MLA_HEADS * MLA_V, DIFF_HEADS * DIFF_V
        h = _out_proj(h, y_mla, y_diff, y_sb, wo[:n_m], wo[n_m:n_m + n_d], wo[n_m + n_d:],
                      row(ln1_g[i]), row(ln1_b[i]), alpha)
        gate, slot, dest = _router(h, rw_pad, rb_col)
        h = _moe(h, slot, dest, gate, p[i].reshape(T, PLE_DIM), wg_all, wu_all, wd_all, i,
                 _bf16(ple_gate[i]), _bf16(ple_proj[i]), row(ln2_g[i]), row(ln2_b[i]), alpha)
    return h.reshape(B, S, D_MODEL)
```

```python
import functools
import math

import jax
import jax.numpy as jnp
from jax import lax
from jax.experimental import pallas as pl
from jax.experimental.pallas import tpu as pltpu

D_MODEL = 1024
CHUNK = 64
PLE_DIM = 256
MLA_HEADS = 4
MLA_NOPE = 64
MLA_ROPE = 32
MLA_V = 64
MLA_Q_RANK = 192
MLA_KV_RANK = 128
ROPE_THETA = 10000.0
DIFF_HEADS = 4
DIFF_QK = 64
DIFF_V = 2 * DIFF_QK
SB_HEADS = 4
SB_D = 64
REL_BUCKETS = 32
REL_MAX_DIST = 128
N_EXPERTS = 16
N_GROUPS = 4
EXPERTS_PER_GROUP = N_EXPERTS // N_GROUPS
D_EXPERT = 512
MLA_IN = MLA_Q_RANK + MLA_KV_RANK + MLA_ROPE
DIFF_IN = 2 * DIFF_HEADS * 2 * DIFF_QK + DIFF_HEADS * DIFF_V
SB_IN = 3 * SB_HEADS * SB_D
EPS = 1e-5
NEG_INF = -1e30

LANES = 128
VMEM_LIMIT = 56 * 1024 * 1024

UA_CKV = 0
UA_KR = 128
UA_KRS = 256
UA_CQ = 384
UA_W = 640
UB_W = DIFF_IN + SB_IN

T5_FAR = 91
SB_LOG_ZERO = -88.0
DIFF_HEADS_PER_STEP = 2
FAR_TILE_WIDTHS = (4, 2)
MOE_SUB = 512
MOE_SLOTS = 96
MOE_EXPERTS_PER_STEP = 2

_NT = (((1,), (1,)), ((), ()))
LOG2E = math.log2(math.e)


def _cparams(sem):
    return pltpu.CompilerParams(dimension_semantics=sem, vmem_limit_bytes=VMEM_LIMIT)


def _bf16(a):
    return a.astype(jnp.bfloat16)


def _dot(a, b):
    return jnp.dot(a, b, preferred_element_type=jnp.float32)


def _in_proj_kernel(x_ref, w_ref, ua_ref, ub_ref):
    x = _bf16(x_ref[...])
    step = 512
    for c in range(0, UA_W, step):
        e = min(c + step, UA_W)
        ua_ref[:, c:e] = _dot(x, w_ref[:, c:e])
    for c in range(0, UB_W, step):
        e = min(c + step, UB_W)
        ub_ref[:, c:e] = _bf16(_dot(x, w_ref[:, UA_W + c:UA_W + e]))


def _in_proj(h2, w_all, tm=512):
    T = h2.shape[0]
    return pl.pallas_call(
        _in_proj_kernel,
        grid=(T // tm,),
        in_specs=[pl.BlockSpec((tm, D_MODEL), lambda i: (i, 0)),
                  pl.BlockSpec((D_MODEL, UA_W + UB_W), lambda i: (0, 0))],
        out_specs=[pl.BlockSpec((tm, UA_W), lambda i: (i, 0)),
                   pl.BlockSpec((tm, UB_W), lambda i: (i, 0))],
        out_shape=[jax.ShapeDtypeStruct((T, UA_W), jnp.float32),
                   jax.ShapeDtypeStruct((T, UB_W), jnp.bfloat16)],
        compiler_params=_cparams(("parallel",)),
        name="in_proj",
    )(h2, w_all)


def _rope_table_kernel(pos_ref, inv_ref, cos_ref, sin_ref):
    ang = pos_ref[...].astype(jnp.float32) * inv_ref[...]
    cos_ref[...] = jnp.cos(ang)
    sin_ref[...] = jnp.sin(ang)


def _rope_table(pos_row, inv_col, tn=2048):
    T = pos_row.shape[1]
    n = inv_col.shape[0]
    tn = min(tn, T)
    return pl.pallas_call(
        _rope_table_kernel,
        grid=(T // tn,),
        in_specs=[pl.BlockSpec((1, tn), lambda i: (0, i)),
                  pl.BlockSpec((n, 1), lambda i: (0, 0))],
        out_specs=[pl.BlockSpec((n, tn), lambda i: (0, i))] * 2,
        out_shape=[jax.ShapeDtypeStruct((n, T), jnp.float32)] * 2,
        compiler_params=_cparams(("parallel",)),
        name="rope_table",
    )(pos_row, inv_col)


def _mla_prep_kernel(ua_ref, cos_ref, sin_ref, gq_ref, gkv_ref, wq_ref, wkv_ref,
                     q_ref, k_ref, v_ref):
    ckv = ua_ref[:, UA_CKV:UA_CKV + 128]
    kr = ua_ref[:, UA_KR:UA_KR + 128]
    krs = ua_ref[:, UA_KRS:UA_KRS + 128]
    cq = ua_ref[:, UA_CQ:UA_CQ + 256]
    cqn = cq * lax.rsqrt(jnp.sum(cq * cq, -1, keepdims=True) * (1.0 / MLA_Q_RANK) + EPS) * gq_ref[...]
    ckvn = ckv * lax.rsqrt(jnp.sum(ckv * ckv, -1, keepdims=True) * (1.0 / MLA_KV_RANK) + EPS) * gkv_ref[...]
    cosf = cos_ref[...]
    sinf = sin_ref[...]
    scale = (MLA_NOPE + MLA_ROPE) ** -0.5 * LOG2E
    cqb = _bf16(cqn)
    ckvb = _bf16(ckvn)
    k_rope = kr * cosf + krs * sinf
    for h in range(MLA_HEADS):
        a = _dot(cqb, wq_ref[:, h * 128:(h + 1) * 128])
        b = _dot(cqb, wq_ref[:, 512 + h * 128:512 + (h + 1) * 128])
        q_ref[:, h * 128:(h + 1) * 128] = _bf16((a * cosf + b * sinf) * scale)
        kn = _dot(ckvb, wkv_ref[:, h * 128:(h + 1) * 128])
        k_ref[:, h * 128:(h + 1) * 128] = _bf16(kn + k_rope)
    v_ref[...] = _bf16(_dot(ckvb, wkv_ref[:, 512:768]))


def _mla_prep(ua, cosf, sinf, gq, gkv, wq, wkv, tm=512):
    T = ua.shape[0]
    full = lambda a: pl.BlockSpec(a.shape, lambda i: (0, 0))
    return pl.pallas_call(
        _mla_prep_kernel,
        grid=(T // tm,),
        in_specs=[pl.BlockSpec((tm, UA_W), lambda i: (i, 0)),
                  pl.BlockSpec((tm, LANES), lambda i: (i, 0)),
                  pl.BlockSpec((tm, LANES), lambda i: (i, 0)),
                  full(gq), full(gkv), full(wq), full(wkv)],
        out_specs=[pl.BlockSpec((tm, 512), lambda i: (i, 0)),
                   pl.BlockSpec((tm, 512), lambda i: (i, 0)),
                   pl.BlockSpec((tm, 256), lambda i: (i, 0))],
        out_shape=[jax.ShapeDtypeStruct((T, 512), jnp.bfloat16),
                   jax.ShapeDtypeStruct((T, 512), jnp.bfloat16),
                   jax.ShapeDtypeStruct((T, 256), jnp.bfloat16)],
        compiler_params=_cparams(("parallel",)),
        name="mla_prep",
    )(ua, cosf, sinf, gq, gkv, wq, wkv)


def _chunk_mask(tq, tk):
    qc = lax.broadcasted_iota(jnp.int32, (tq, tk), 0) // CHUNK
    kc = lax.broadcasted_iota(jnp.int32, (tq, tk), 1) // CHUNK
    return kc <= qc


def _softmax_update(s, v, m_ref, l_ref, acc_ref, idx, shift=None):
    tk = s.shape[1]
    m_old = m_ref[idx]
    row_max = jnp.max(s, axis=1, keepdims=True)
    if shift is not None:
        row_max = row_max + shift
    m_new = jnp.maximum(m_old, row_max)
    alpha = jnp.exp2(m_old - m_new)
    m_sub = m_new if shift is None else m_new - shift
    p = jnp.exp2(s - jnp.concatenate([m_sub] * (tk // LANES), axis=1))
    psum = p[:, :LANES]
    for c in range(LANES, tk, LANES):
        psum = psum + p[:, c:c + LANES]
    l_ref[idx] = alpha * l_ref[idx] + psum
    acc_ref[idx] = alpha * acc_ref[idx] + _dot(_bf16(p), v)
    m_ref[idx] = m_new


def _softmax_init(m_ref, l_ref, acc_ref):
    m_ref[...] = jnp.full(m_ref.shape, NEG_INF, jnp.float32)
    l_ref[...] = jnp.zeros(l_ref.shape, jnp.float32)
    acc_ref[...] = jnp.zeros(acc_ref.shape, jnp.float32)


def _softmax_result(l_ref, acc_ref, idx):
    return acc_ref[idx] / jnp.sum(l_ref[idx], axis=1, keepdims=True)


def _mla_attn_kernel(q_ref, k_ref, v_ref, o_ref, m_ref, l_ref, acc_ref, *, t):
    qi = pl.program_id(1)
    _softmax_init(m_ref, l_ref, acc_ref)

    def tile(j, masked, width=1):
        ks = pl.multiple_of(j * t, t)
        for hh in range(MLA_HEADS):
            q = q_ref[:, hh * 128:(hh + 1) * 128]
            k = k_ref[pl.ds(ks, width * t), hh * 128:(hh + 1) * 128]
            v = v_ref[pl.ds(ks, width * t), (hh // 2) * 128:(hh // 2 + 1) * 128]
            s = lax.dot_general(q, k, _NT, preferred_element_type=jnp.float32)
            if masked:
                s = jnp.where(_chunk_mask(t, t), s, NEG_INF)
            _softmax_update(s, v, m_ref, l_ref, acc_ref, hh)

    done = 0
    for width in FAR_TILE_WIDTHS + (1,):
        trips = (qi - done) // width

        def body(i, carry, width=width, done=done):
            tile(done + i * width, False, width=width)
            return carry

        lax.fori_loop(0, trips, body, 0)
        done = done + trips * width

    tile(qi, True)
    lane = lax.broadcasted_iota(jnp.int32, (t, 128), 1)
    for pair in range(MLA_HEADS // 2):
        o_ref[:, pair * 128:(pair + 1) * 128] = _bf16(
            jnp.where(lane < MLA_V, _softmax_result(l_ref, acc_ref, 2 * pair),
                      _softmax_result(l_ref, acc_ref, 2 * pair + 1)))


def _mla_attn(q, k, v, B, S, t=256):
    nq = S // t
    wq, wv = MLA_HEADS * 128, MLA_HEADS * MLA_V
    return pl.pallas_call(
        functools.partial(_mla_attn_kernel, t=t),
        grid=(B, nq),
        in_specs=[pl.BlockSpec((t, wq), lambda b, i: (b * nq + i, 0)),
                  pl.BlockSpec((S, wq), lambda b, i: (b, 0)),
                  pl.BlockSpec((S, wv), lambda b, i: (b, 0))],
        out_specs=pl.BlockSpec((t, wv), lambda b, i: (b * nq + i, 0)),
        out_shape=jax.ShapeDtypeStruct((B * S, wv), jnp.bfloat16),
        scratch_shapes=[pltpu.VMEM((MLA_HEADS, t, LANES), jnp.float32)] * 3,
        compiler_params=_cparams(("parallel", "arbitrary")),
        name="mla_attn",
    )(q, k, v)


def _t5_bias(rel, rb_ref, h):
    nb = REL_BUCKETS // 2
    max_exact = nb // 2
    n = jnp.abs(rel)
    nf = jnp.maximum(n, 1).astype(jnp.float32)
    large = max_exact + (jnp.log(nf / max_exact) / math.log(REL_MAX_DIST / max_exact)
                         * (nb - max_exact)).astype(jnp.int32)
    large = jnp.minimum(large, nb - 1)
    low = jnp.where(n < max_exact, n, large)
    neg = jnp.zeros(rel.shape, jnp.float32)
    pos = jnp.zeros(rel.shape, jnp.float32)
    for j in range(nb):
        eq = low == j
        neg = jnp.where(eq, rb_ref[h * REL_BUCKETS + j], neg)
        pos = jnp.where(eq, rb_ref[h * REL_BUCKETS + nb + j], pos)
    return jnp.where(rel > 0, pos, neg)


def _diff_attn_kernel(qmin_ref, kmax_ref, first_ref, consec_ref,
                      rb_ref, q_ref, k_ref, v_ref, posq_ref, posk_ref,
                      lq1_ref, lk1_ref, lq2_ref, lk2_ref, sub_ref,
                      o_ref, m_ref, l_ref, acc_ref, bias_ref, cache_ref, *, t, nq, lam_init):
    b = pl.program_id(0)
    hp = pl.program_id(1)
    qi = pl.program_id(2)
    _softmax_init(m_ref, l_ref, acc_ref)

    @pl.when(qi == 0)
    def _():
        for n in range(4):
            cache_ref[n] = 0

    lane = lax.broadcasted_iota(jnp.int32, (t, LANES), 1)
    qs = []
    for hh in range(DIFF_HEADS_PER_STEP):
        q = q_ref[:, hh * LANES:(hh + 1) * LANES]
        zero = jnp.zeros_like(q)
        qs.append(jnp.concatenate([jnp.where(lane < DIFF_QK, q, zero), jnp.where(lane < DIFF_QK, zero, q)], axis=0))
    tq = b * nq + qi
    qmin = qmin_ref[tq]

    def toeplitz_bias(j, h):
        d0 = first_ref[b * nq + j] - first_ref[tq]
        x = lax.broadcasted_iota(jnp.int32, (1, 2 * t), 1)
        g = _t5_bias(d0 + jnp.where(x < t, x, x - 2 * t), rb_ref, h)
        g = pltpu.roll(jnp.broadcast_to(g, (t, 2 * t)), 0, 1, stride=1, stride_axis=0)
        return g[:, :t]

    def general_bias(j, h):
        rel = posk_ref[j] - posq_ref[...]
        return _t5_bias(rel, rb_ref, h)

    def masked_bias(bias, masked):
        bias = bias * LOG2E
        return jnp.where(_chunk_mask(t, t), bias, NEG_INF) if masked else bias

    def tile(j, masked, bias_fn, width=1):
        ks = pl.multiple_of(j * t, t)
        for hh in range(DIFF_HEADS_PER_STEP):
            h = hp * DIFF_HEADS_PER_STEP + hh
            k = k_ref[pl.ds(ks, width * t), hh * LANES:(hh + 1) * LANES]
            v = v_ref[pl.ds(ks, width * t), hh * LANES:(hh + 1) * LANES]
            s = lax.dot_general(qs[hh], k, _NT, preferred_element_type=jnp.float32)
            if bias_fn is None:
                shift = rb_ref[h * REL_BUCKETS + REL_BUCKETS // 2 - 1] * LOG2E
            else:
                shift = None
                s = (s.reshape(2, t, t) + bias_fn(j, h, hh)[None]).reshape(2 * t, t)
            _softmax_update(s, v, m_ref, l_ref, acc_ref, hh, shift)

    def near_tile(j, masked):
        consecutive = jnp.logical_and(consec_ref[tq] == 1, consec_ref[b * nq + j] == 1)
        slot = 1 if masked else 0

        @pl.when(consecutive)
        def _():
            d0 = first_ref[b * nq + j] - first_ref[tq]
            stale = jnp.logical_or(cache_ref[2 * slot + 1] != 1, cache_ref[2 * slot] != d0)

            @pl.when(stale)
            def _():
                for hh in range(DIFF_HEADS_PER_STEP):
                    h = hp * DIFF_HEADS_PER_STEP + hh
                    bias_ref[slot, hh] = masked_bias(toeplitz_bias(j, h), masked)
                cache_ref[2 * slot] = d0
                cache_ref[2 * slot + 1] = 1

            tile(j, masked, lambda j, h, hh: bias_ref[slot, hh])

        @pl.when(jnp.logical_not(consecutive))
        def _():
            tile(j, masked, lambda j, h, hh: masked_bias(general_bias(j, h), masked))

    def is_far(j):
        return kmax_ref[b * nq + jnp.minimum(j, qi)] - qmin <= -T5_FAR

    def wide_far_loop(start, width):
        def cond(j):
            ok = j + width - 1 < qi
            for d in range(width):
                ok = jnp.logical_and(ok, is_far(j + d))
            return ok

        def step(j):
            tile(j, False, None, width=width)
            return j + width

        return lax.while_loop(cond, step, start)

    def body(j, carry):
        far = is_far(j)

        @pl.when(far)
        def _():
            tile(j, False, None)

        @pl.when(jnp.logical_not(far))
        def _():
            near_tile(j, False)

        return carry

    done = 0
    for width in FAR_TILE_WIDTHS:
        done = wide_far_loop(done, width)
    lax.fori_loop(done, qi, body, 0)
    near_tile(qi, True)

    f32 = jnp.float32
    lam = (jnp.exp(jnp.sum(lq1_ref[...].astype(f32) * lk1_ref[...].astype(f32), keepdims=True))
           - jnp.exp(jnp.sum(lq2_ref[...].astype(f32) * lk2_ref[...].astype(f32), keepdims=True))
           + lam_init)
    for hh in range(DIFF_HEADS_PER_STEP):
        a = _softmax_result(l_ref, acc_ref, hh)
        o = a[:t] - lam * a[t:]
        o = o * lax.rsqrt(jnp.mean(o * o, -1, keepdims=True) + EPS) * sub_ref[...] * (1.0 - lam_init)
        o_ref[:, hh * LANES:(hh + 1) * LANES] = _bf16(o)


def _diff_attn(ub, pos_col, pos_tiles, qmin, kmax, first, consec, rb_flat, lq1, lk1, lq2, lk2, subln,
               B, S, lam_init, t=256):
    nq = S // t
    per = DIFF_HEADS_PER_STEP
    G = DIFF_HEADS // per
    small = lambda a: pl.BlockSpec(a.shape, lambda b, h, i, *_: (0, 0))
    grid_spec = pltpu.PrefetchScalarGridSpec(
        num_scalar_prefetch=4,
        grid=(B, G, nq),
        in_specs=[pl.BlockSpec(memory_space=pltpu.SMEM),
                  pl.BlockSpec((t, per * 128), lambda b, h, i, *_: (b * nq + i, h)),
                  pl.BlockSpec((S, per * 128), lambda b, h, i, *_: (b, G + h)),
                  pl.BlockSpec((S, per * 128), lambda b, h, i, *_: (b, 2 * G + h)),
                  pl.BlockSpec((t, 1), lambda b, h, i, *_: (b * nq + i, 0)),
                  pl.BlockSpec((nq, 1, t), lambda b, h, i, *_: (b, 0, 0)),
                  small(lq1), small(lk1), small(lq2), small(lk2), small(subln)],
        out_specs=pl.BlockSpec((t, per * 128), lambda b, h, i, *_: (b * nq + i, h)),
        scratch_shapes=[pltpu.VMEM((per, 2 * t, LANES), jnp.float32)] * 3
                       + [pltpu.VMEM((2, per, t, t), jnp.float32), pltpu.SMEM((4,), jnp.int32)],
    )
    return pl.pallas_call(
        functools.partial(_diff_attn_kernel, t=t, nq=nq, lam_init=lam_init),
        grid_spec=grid_spec,
        out_shape=jax.ShapeDtypeStruct((B * S, DIFF_HEADS * DIFF_V), jnp.bfloat16),
        compiler_params=_cparams(("parallel", "parallel", "arbitrary")),
        name="diff_attn",
    )(qmin, kmax, first, consec, rb_flat, ub, ub, ub, pos_col, pos_tiles, lq1, lk1, lq2, lk2, subln)


def _sb_attn_kernel(q_ref, k_ref, v_ref, o_ref, c_ref, acc_ref, *, t):
    qi = pl.program_id(1)
    n_pairs = SB_HEADS // 2
    row = lax.broadcasted_iota(jnp.int32, (t, t), 0)
    col = lax.broadcasted_iota(jnp.int32, (t, t), 1)
    tri = jnp.where(row > col, 1.0, 0.0).astype(jnp.bfloat16)
    strict = (col < row)[None]
    lane = lax.broadcasted_iota(jnp.int32, (t, LANES), 1)
    qs = []
    for hp in range(n_pairs):
        q = q_ref[:, hp * LANES:(hp + 1) * LANES]
        zero = jnp.zeros_like(q)
        qs.append(jnp.concatenate([jnp.where(lane < SB_D, q, zero), jnp.where(lane < SB_D, zero, q)], axis=0))
    c_ref[...] = jnp.zeros(c_ref.shape, jnp.float32)
    acc_ref[...] = jnp.zeros(acc_ref.shape, jnp.float32)

    def tile(j, diag):
        ks = pl.multiple_of(j * t, t)
        for hp in range(n_pairs):
            k = k_ref[pl.ds(ks, t), hp * LANES:(hp + 1) * LANES]
            v = v_ref[pl.ds(ks, t), hp * LANES:(hp + 1) * LANES]
            z = lax.dot_general(qs[hp], k, _NT, preferred_element_type=jnp.float32)
            lf = -(jnp.maximum(z, 0.0) + jnp.log(1.0 + jnp.exp(-jnp.abs(z))))
            if diag:
                lf = jnp.where(strict, lf.reshape(2, t, t), 0.0).reshape(2 * t, t)
            hi = _bf16(lf)
            lo = _bf16(lf - hi.astype(jnp.float32))
            c = c_ref[hp]
            later = _dot(hi, tri) + _dot(lo, tri) + jnp.concatenate([c] * (t // LANES), axis=1)
            w = jnp.exp(lf + z + later)
            if diag:
                w = jnp.where(strict, w.reshape(2, t, t), 0.0).reshape(2 * t, t)
            acc_ref[hp] += _dot(_bf16(w), v)
            c_ref[hp] = c + jnp.sum(lf, axis=1, keepdims=True)

    tile(qi, True)

    def cond(carry):
        j, cmax = carry
        return jnp.logical_and(j >= 0, cmax > SB_LOG_ZERO)

    def body(carry):
        j, _ = carry
        tile(j, False)
        return j - 1, jnp.max(c_ref[...])

    lax.while_loop(cond, body, (qi - 1, jnp.max(c_ref[...])))
    for hp in range(n_pairs):
        acc = acc_ref[hp]
        o_ref[:, hp * LANES:(hp + 1) * LANES] = _bf16(jnp.where(lane < SB_D, acc[:t], acc[t:]))


def _sb_attn(ub, B, S, t=256):
    nq = S // t
    w = SB_HEADS * SB_D
    c0 = DIFF_IN // w
    return pl.pallas_call(
        functools.partial(_sb_attn_kernel, t=t),
        grid=(B, nq),
        in_specs=[pl.BlockSpec((t, w), lambda b, i: (b * nq + i, c0)),
                  pl.BlockSpec((S, w), lambda b, i: (b, c0 + 1)),
                  pl.BlockSpec((S, w), lambda b, i: (b, c0 + 2))],
        out_specs=pl.BlockSpec((t, w), lambda b, i: (b * nq + i, 0)),
        out_shape=jax.ShapeDtypeStruct((B * S, w), jnp.bfloat16),
        scratch_shapes=[pltpu.VMEM((SB_HEADS // 2, 2 * t, LANES), jnp.float32)] * 2,
        compiler_params=_cparams(("parallel", "arbitrary")),
        name="sb_attn",
    )(ub, ub, ub)


def _layer_norm(x, g, b):
    mu = jnp.mean(x, -1, keepdims=True)
    xc = x - mu
    var = jnp.mean(xc * xc, -1, keepdims=True)
    return xc * lax.rsqrt(var + EPS) * g + b


def _out_proj_kernel(h_ref, ym_ref, yd_ref, ys_ref, wm_ref, wd_ref, ws_ref, g_ref, b_ref, o_ref, *, alpha):
    mix = _dot(ym_ref[...], wm_ref[...]) + _dot(yd_ref[...], wd_ref[...]) + _dot(ys_ref[...], ws_ref[...])
    o_ref[...] = _layer_norm(alpha * h_ref[...] + mix, g_ref[...], b_ref[...])


def _out_proj(h2, ym, yd, ys, wm, wd, ws, g, b, alpha, tm=512):
    T = h2.shape[0]
    row = lambda a: pl.BlockSpec((tm, a.shape[1]), lambda i: (i, 0))
    full = lambda a: pl.BlockSpec(a.shape, lambda i: (0, 0))
    return pl.pallas_call(
        functools.partial(_out_proj_kernel, alpha=alpha),
        grid=(T // tm,),
        in_specs=[row(h2), row(ym), row(yd), row(ys), full(wm), full(wd), full(ws), full(g), full(b)],
        out_specs=pl.BlockSpec((tm, D_MODEL), lambda i: (i, 0)),
        out_shape=jax.ShapeDtypeStruct((T, D_MODEL), jnp.float32),
        compiler_params=_cparams(("parallel",)),
        name="out_proj_ln",
    )(h2, ym, yd, ys, wm, wd, ws, g, b)


def _first_max(vals):
    m = vals[0]
    for v in vals[1:]:
        m = jnp.maximum(m, v)
    taken = jnp.zeros(m.shape, jnp.bool_)
    hot = []
    for v in vals:
        is_first = jnp.logical_and(v == m, jnp.logical_not(taken))
        hot.append(is_first)
        taken = jnp.logical_or(taken, is_first)
    return m, hot


def _router_kernel(h_ref, rw_ref, rb_ref, gate_ref, slot_ref, dest_ref, chosen_ref, first_ref):
    h = h_ref[...]
    rw = rw_ref[...]
    h_hi, rw_hi = _bf16(h), _bf16(rw)
    h_lo, rw_lo = _bf16(h - h_hi.astype(jnp.float32)), _bf16(rw - rw_hi.astype(jnp.float32))
    logits = (_dot(h_hi, rw_hi) + (_dot(h_hi, rw_lo) + _dot(h_lo, rw_hi))).T[:N_EXPERTS]
    scores = jax.nn.sigmoid(logits)
    sel = scores + rb_ref[...]
    ninf = -jnp.inf
    group_score, first, second = [], [], []
    for g in range(N_GROUPS):
        vals = [sel[g * EXPERTS_PER_GROUP + k:g * EXPERTS_PER_GROUP + k + 1, :] for k in range(EXPERTS_PER_GROUP)]
        m1, hot1 = _first_max(vals)
        m2, hot2 = _first_max([jnp.where(hh, ninf, v) for hh, v in zip(hot1, vals)])
        group_score.append(m1 + m2)
        first.append(hot1)
        second.append(hot2)
    _, best = _first_max(group_score)
    w1 = jnp.zeros_like(group_score[0])
    w2 = jnp.zeros_like(group_score[0])
    for g in range(N_GROUPS):
        for k in range(EXPERTS_PER_GROUP):
            e = g * EXPERTS_PER_GROUP + k
            sc = scores[e:e + 1, :]
            w1 = jnp.where(jnp.logical_and(best[g], first[g][k]), sc, w1)
            w2 = jnp.where(jnp.logical_and(best[g], second[g][k]), sc, w2)
    tot = w1 + w2
    for g in range(N_GROUPS):
        for k in range(EXPERTS_PER_GROUP):
            e = g * EXPERTS_PER_GROUP + k
            is1 = jnp.logical_and(best[g], first[g][k])
            is2 = jnp.logical_and(best[g], second[g][k])
            gate_ref[e:e + 1, :] = jnp.where(is1, w1 / tot, 0.0) + jnp.where(is2, w2 / tot, 0.0)
            chosen_ref[e:e + 1, :] = jnp.where(jnp.logical_or(is1, is2), 1.0, 0.0)
            first_ref[e:e + 1, :] = jnp.where(is1, 1.0, 0.0)
    chosen = chosen_ref[...]
    tm = chosen.shape[1]
    before = (lax.broadcasted_iota(jnp.int32, (tm, tm), 0) < lax.broadcasted_iota(jnp.int32, (tm, tm), 1))
    rank = _dot(_bf16(chosen), jnp.where(before, 1.0, 0.0).astype(jnp.bfloat16)).astype(jnp.int32)
    slot_ref[...] = jnp.where(chosen > 0.0, rank, -1)
    expert = lax.broadcasted_iota(jnp.int32, chosen.shape, 0)
    parked = jnp.logical_and(chosen > 0.0, rank < MOE_SLOTS)
    code = jnp.where(parked, expert * MOE_SLOTS + rank + 1, 0)
    is_first = first_ref[...] > 0.0
    dest_ref[0:1, :] = jnp.sum(jnp.where(is_first, code, 0), axis=0, keepdims=True) - 1
    dest_ref[1:2, :] = jnp.sum(jnp.where(is_first, 0, code), axis=0, keepdims=True) - 1


def _router(h2, rw_pad, rb_col):
    T = h2.shape[0]
    tm = MOE_SUB
    tok = lambda rows: pl.BlockSpec((rows, tm), lambda i: (0, i))
    return pl.pallas_call(
        _router_kernel,
        grid=(T // tm,),
        in_specs=[pl.BlockSpec((tm, D_MODEL), lambda i: (i, 0)),
                  pl.BlockSpec(rw_pad.shape, lambda i: (0, 0)),
                  pl.BlockSpec(rb_col.shape, lambda i: (0, 0))],
        out_specs=[tok(N_EXPERTS), tok(N_EXPERTS), tok(2)],
        out_shape=[jax.ShapeDtypeStruct((N_EXPERTS, T), jnp.float32),
                   jax.ShapeDtypeStruct((N_EXPERTS, T), jnp.int32),
                   jax.ShapeDtypeStruct((2, T), jnp.int32)],
        scratch_shapes=[pltpu.VMEM((N_EXPERTS, tm), jnp.float32)] * 2,
        compiler_params=_cparams(("parallel",)),
        name="router",
    )(h2, rw_pad, rb_col)


def _moe_kernel(npass_ref, h_ref, slot_ref, slot_t_ref, dest_t_ref, gate_ref, p_ref, wg_ref, wu_ref, wd_ref,
                pg_ref, pp_ref, g_ref, b_ref, o_ref, over_ref, xb_ref, y_ref, *, alpha, n_sub):
    i = pl.program_id(0)
    step = pl.program_id(1)
    R, SUB = MOE_SLOTS, MOE_SUB

    @pl.when(step == 0)
    def _():
        xb_ref[...] = _bf16(h_ref[...])
        over_ref[...] = jnp.zeros(over_ref.shape, jnp.float32)

    def expert_pass(e, k, c):
        slot_row = slot_ref[pl.ds(e, 1), :]
        gate_row = gate_ref[pl.ds(e, 1), :]
        xs, gs = [], []
        for j in range(n_sub):
            sl = slot_row[:, j * SUB:(j + 1) * SUB] - c * R
            hit = lax.broadcasted_iota(jnp.int32, (R, SUB), 0) == sl
            onehot = jnp.where(hit, 1.0, 0.0).astype(jnp.bfloat16)
            xs.append(_bf16(_dot(onehot, xb_ref[j * SUB:(j + 1) * SUB, :])))
            gs.append(jnp.sum(jnp.where(hit, gate_row[:, j * SUB:(j + 1) * SUB], 0.0), axis=1, keepdims=True))
        xe = jnp.concatenate(xs, axis=0)
        hid = jax.nn.silu(_dot(xe, wg_ref[k])) * _dot(xe, wu_ref[k])
        y = _dot(_bf16(hid), wd_ref[k])
        return [_bf16(y[j * R:(j + 1) * R] * gs[j]) for j in range(n_sub)]

    for k in range(MOE_EXPERTS_PER_STEP):
        e = step * MOE_EXPERTS_PER_STEP + k
        ys = expert_pass(e, k, 0)
        for j in range(n_sub):
            y_ref[j, pl.ds(pl.multiple_of(e * R, 16), R), :] = ys[j]

    for k in range(MOE_EXPERTS_PER_STEP):
        e = step * MOE_EXPERTS_PER_STEP + k

        def overflow_pass(c, carry, e=e, k=k):
            ys = expert_pass(e, k, c)
            lane16 = lax.broadcasted_iota(jnp.int32, (SUB, N_EXPERTS), 1)
            for j in range(n_sub):
                rows = slice(j * SUB, (j + 1) * SUB)
                slot_col = jnp.sum(jnp.where(lane16 == e, slot_t_ref[rows, :], 0), axis=1, keepdims=True)
                hit_t = lax.broadcasted_iota(jnp.int32, (SUB, R), 1) == slot_col - c * R
                over_ref[rows, :] += _dot(jnp.where(hit_t, 1.0, 0.0).astype(jnp.bfloat16), ys[j])
            return carry

        lax.fori_loop(1, npass_ref[i * N_EXPERTS + e], overflow_pass, 0)

    @pl.when(step == N_EXPERTS // MOE_EXPERTS_PER_STEP - 1)
    def _():
        lane = lax.broadcasted_iota(jnp.int32, (SUB, N_EXPERTS * R), 1)
        for j in range(n_sub):
            rows = slice(j * SUB, (j + 1) * SUB)
            dest = dest_t_ref[rows, :]
            hit = jnp.logical_or(lane == dest[:, 0:1], lane == dest[:, 1:2])
            scatter = jnp.where(hit, 1.0, 0.0).astype(jnp.bfloat16)
            ffn = _dot(scatter, y_ref[j]) + over_ref[rows, :]
            h = h_ref[rows, :]
            ple = jax.nn.sigmoid(_dot(xb_ref[rows, :], pg_ref[...])) * _dot(_bf16(p_ref[rows, :]), pp_ref[...])
            o_ref[rows, :] = _layer_norm(alpha * h + ffn + ple, g_ref[...], b_ref[...])


def _moe(h2, slot, dest, gate, p2, wg, wu, wd, layer, pg, pp, g, b, alpha, tm=1024):
    T = h2.shape[0]
    n_sub = tm // MOE_SUB
    n_tiles = T // tm
    per = MOE_EXPERTS_PER_STEP
    count = jnp.sum((slot >= 0).reshape(N_EXPERTS, n_tiles, n_sub, MOE_SUB), axis=3)
    npass = jnp.maximum(1, (jnp.max(count, axis=2) + MOE_SLOTS - 1) // MOE_SLOTS).T.reshape(-1).astype(jnp.int32)
    full = lambda a: pl.BlockSpec(a.shape, lambda i, e, *_: (0, 0))
    experts = lambda a: pl.BlockSpec((None, per) + a.shape[2:], lambda i, e, *_: (layer, e, 0, 0))
    grid_spec = pltpu.PrefetchScalarGridSpec(
        num_scalar_prefetch=1,
        grid=(n_tiles, N_EXPERTS // per),
        in_specs=[pl.BlockSpec((tm, D_MODEL), lambda i, e, *_: (i, 0)),
                  pl.BlockSpec((N_EXPERTS, tm), lambda i, e, *_: (0, i)),
                  pl.BlockSpec((tm, N_EXPERTS), lambda i, e, *_: (i, 0)),
                  pl.BlockSpec((tm, 2), lambda i, e, *_: (i, 0)),
                  pl.BlockSpec((N_EXPERTS, tm), lambda i, e, *_: (0, i)),
                  pl.BlockSpec((tm, PLE_DIM), lambda i, e, *_: (i, 0)),
                  experts(wg), experts(wu), experts(wd),
                  full(pg), full(pp), full(g), full(b)],
        out_specs=pl.BlockSpec((tm, D_MODEL), lambda i, e, *_: (i, 0)),
        scratch_shapes=[pltpu.VMEM((tm, D_MODEL), jnp.float32),
                        pltpu.VMEM((tm, D_MODEL), jnp.bfloat16),
                        pltpu.VMEM((n_sub, N_EXPERTS * MOE_SLOTS, D_MODEL), jnp.bfloat16)],
    )
    return pl.pallas_call(
        functools.partial(_moe_kernel, alpha=alpha, n_sub=n_sub),
        grid_spec=grid_spec,
        out_shape=jax.ShapeDtypeStruct((T, D_MODEL), jnp.float32),
        compiler_params=_cparams(("parallel", "arbitrary")),
        name="moe_ple_ln",
    )(npass, h2, slot, slot.T, dest.T, gate, p2, wg, wu, wd, pg, pp, g, b)


def _rotate_half_cols(w):
    half = w.shape[1] // 2
    return jnp.concatenate([-w[:, half:], w[:, :half]], axis=1)


def _in_proj_weights(w_in):
    z = lambda n: jnp.zeros((D_MODEL, n), w_in.dtype)
    cq = w_in[:, :MLA_Q_RANK]
    ckv = w_in[:, MLA_Q_RANK:MLA_Q_RANK + MLA_KV_RANK]
    kr = w_in[:, MLA_Q_RANK + MLA_KV_RANK:MLA_IN]
    n_dq = DIFF_HEADS * 2 * DIFF_QK
    sb0 = MLA_IN + DIFF_IN
    n_sq = SB_HEADS * SB_D
    cols = [ckv,
            z(MLA_NOPE), kr, z(128 - MLA_NOPE - MLA_ROPE),
            z(MLA_NOPE), _rotate_half_cols(kr), z(128 - MLA_NOPE - MLA_ROPE),
            cq, z(256 - MLA_Q_RANK),
            w_in[:, MLA_IN:MLA_IN + n_dq] * (DIFF_QK ** -0.5 * LOG2E),
            w_in[:, MLA_IN + n_dq:sb0],
            w_in[:, sb0:sb0 + n_sq] * (SB_D ** -0.5),
            w_in[:, sb0 + n_sq:]]
    return _bf16(jnp.concatenate(cols, axis=1))


def _mla_up_weights(w_uq, w_ukv):
    dq = MLA_NOPE + MLA_ROPE
    zq = lambda n: jnp.zeros((MLA_Q_RANK, n), w_uq.dtype)
    plain, rot = [], []
    for h in range(MLA_HEADS):
        wh = w_uq[:, h * dq:(h + 1) * dq]
        plain += [wh, zq(128 - dq)]
        rot += [zq(MLA_NOPE), _rotate_half_cols(wh[:, MLA_NOPE:]), zq(128 - dq)]
    wq = jnp.concatenate(plain + rot, axis=1)
    wq = jnp.concatenate([wq, jnp.zeros((256 - MLA_Q_RANK, wq.shape[1]), wq.dtype)], axis=0)
    dkv = MLA_NOPE + MLA_V
    zk = lambda n: jnp.zeros((MLA_KV_RANK, n), w_ukv.dtype)
    kcols, vcols = [], []
    for h in range(MLA_HEADS):
        wh = w_ukv[:, h * dkv:(h + 1) * dkv]
        kcols += [wh[:, :MLA_NOPE], zk(128 - MLA_NOPE)]
        vcols += [wh[:, MLA_NOPE:]]
    wkv = jnp.concatenate(kcols + vcols, axis=1)
    return _bf16(wq), _bf16(wkv)


def _attn_tiles(S):
    pick = lambda want: max(c for c in (128, 256, 512, 1024) if c <= want and S % c == 0)
    return pick(512), pick(512), pick(256)


def kernel(x, p, positions, w_in, mla_q_norm, mla_w_uq, mla_kv_norm, mla_w_ukv, diff_lambda_q1, diff_lambda_k1, diff_lambda_q2, diff_lambda_k2, diff_subln, rel_bias, w_o, ln1_g, ln1_b, router_w, router_b, w_gate, w_up, w_down, ple_proj, ple_gate, ln2_g, ln2_b):
    B, S, _ = x.shape
    depth = w_in.shape[0]
    T = B * S
    alpha = (2 * depth) ** 0.25
    t_mla, t_diff, t_sb = _attn_tiles(S)
    nq = S // t_diff

    pos_col = positions.reshape(T, 1)
    pos_tiles = positions.reshape(B * nq, 1, t_diff)
    tile_pos = positions.reshape(B * nq, t_diff)
    qmin = jnp.min(tile_pos, axis=1)
    kmax = jnp.max(tile_pos, axis=1)
    first = tile_pos[:, 0]
    consec = jnp.all(tile_pos[:, 1:] - tile_pos[:, :-1] == 1, axis=1).astype(jnp.int32)
    rb_flat = rel_bias.T.reshape(-1).astype(jnp.float32)

    half = MLA_ROPE // 2
    inv = ROPE_THETA ** (-jnp.arange(half, dtype=jnp.float32) / half)
    cos_t, sin_t = _rope_table(positions.reshape(1, T), jnp.concatenate([inv, inv]).reshape(MLA_ROPE, 1))
    pad_l = lambda v: jnp.full((T, MLA_NOPE), v, jnp.float32)
    pad_r = jnp.zeros((T, LANES - MLA_NOPE - MLA_ROPE), jnp.float32)
    cosf = jnp.concatenate([pad_l(1.0), cos_t.T, pad_r], axis=1)
    sinf = jnp.concatenate([pad_l(0.0), sin_t.T, pad_r], axis=1)

    rw_pad = jnp.pad(router_w.astype(jnp.float32), ((0, 0), (0, LANES - N_EXPERTS)))
    rb_col = router_b.reshape(N_EXPERTS, 1).astype(jnp.float32)
    row = lambda a: a.reshape(1, -1)

    wg_all, wu_all, wd_all = _bf16(w_gate), _bf16(w_up), _bf16(w_down)

    h = x.reshape(T, D_MODEL)
    for i in range(depth):
        lam_init = 0.8 - 0.6 * math.exp(-0.3 * i)
        ua, ub = _in_proj(h, _in_proj_weights(w_in[i]))
        wq, wkv = _mla_up_weights(mla_w_uq[i], mla_w_ukv[i])
        gq = jnp.concatenate([mla_q_norm[i], jnp.zeros((256 - MLA_Q_RANK,), mla_q_norm.dtype)]).reshape(1, 256)
        q_m, k_m, v_m = _mla_prep(ua, cosf, sinf, gq, row(mla_kv_norm[i]), wq, wkv)
        y_mla = _mla_attn(q_m, k_m, v_m, B, S, t=t_mla)
        y_diff = _diff_attn(ub, pos_col, pos_tiles, qmin, kmax, first, consec, rb_flat,
                            row(diff_lambda_q1[i]), row(diff_lambda_k1[i]),
                            row(diff_lambda_q2[i]), row(diff_lambda_k2[i]), row(diff_subln[i]),
                            B, S, lam_init, t=t_diff)
        y_sb = _sb_attn(ub, B, S, t=t_sb)
        wo = _bf16(w_o[i])
        n_m, n_d = MLA_HEADS * MLA_V, DIFF_HEADS * DIFF_V
        h = _out_proj(h, y_mla, y_diff, y_sb, wo[:n_m], wo[n_m:n_m + n_d], wo[n_m + n_d:],
                      row(ln1_g[i]), row(ln1_b[i]), alpha)
        gate, slot, dest = _router(h, rw_pad, rb_col)
        h = _moe(h, slot, dest, gate, p[i].reshape(T, PLE_DIM), wg_all, wu_all, wd_all, i,
                 _bf16(ple_gate[i]), _bf16(ple_proj[i]), row(ln2_g[i]), row(ln2_b[i]), alpha)
    return h.reshape(B, S, D_MODEL)
```

```python
import functools
import math

import jax
import jax.numpy as jnp
from jax import lax
from jax.experimental import pallas as pl
from jax.experimental.pallas import tpu as pltpu

D_MODEL = 1024
CHUNK = 64
PLE_DIM = 256
MLA_HEADS = 4
MLA_NOPE = 64
MLA_ROPE = 32
MLA_V = 64
MLA_Q_RANK = 192
MLA_KV_RANK = 128
ROPE_THETA = 10000.0
DIFF_HEADS = 4
DIFF_QK = 64
DIFF_V = 2 * DIFF_QK
SB_HEADS = 4
SB_D = 64
REL_BUCKETS = 32
REL_MAX_DIST = 128
N_EXPERTS = 16
N_GROUPS = 4
EXPERTS_PER_GROUP = N_EXPERTS // N_GROUPS
D_EXPERT = 512
MLA_IN = MLA_Q_RANK + MLA_KV_RANK + MLA_ROPE
DIFF_IN = 2 * DIFF_HEADS * 2 * DIFF_QK + DIFF_HEADS * DIFF_V
SB_IN = 3 * SB_HEADS * SB_D
EPS = 1e-5
NEG_INF = -1e30

LANES = 128
VMEM_LIMIT = 56 * 1024 * 1024

UA_CKV = 0
UA_KR = 128
UA_KRS = 256
UA_CQ = 384
UA_W = 640
UB_W = DIFF_IN + SB_IN

T5_FAR = 91
SB_LOG_ZERO = -88.0
DIFF_HEADS_PER_STEP = 2
FAR_TILE_WIDTHS = (4, 2)
MOE_SUB = 512
MOE_SLOTS = 96
MOE_EXPERTS_PER_STEP = 2

_NT = (((1,), (1,)), ((), ()))
LOG2E = math.log2(math.e)


def _cparams(sem):
    return pltpu.CompilerParams(dimension_semantics=sem, vmem_limit_bytes=VMEM_LIMIT)


def _bf16(a):
    return a.astype(jnp.bfloat16)


def _dot(a, b):
    return jnp.dot(a, b, preferred_element_type=jnp.float32)


def _in_proj_kernel(x_ref, w_ref, cos_ref, sin_ref, gq_ref, gkv_ref, wq_ref, wkv_ref,
                    ub_ref, q_ref, k_ref, v_ref, ua_ref):
    x = _bf16(x_ref[...])
    step = 512
    for c in range(0, UA_W, step):
        e = min(c + step, UA_W)
        ua_ref[:, c:e] = _dot(x, w_ref[:, c:e])
    for c in range(0, UB_W, step):
        e = min(c + step, UB_W)
        ub_ref[:, c:e] = _bf16(_dot(x, w_ref[:, UA_W + c:UA_W + e]))
    _mla_prep_kernel(ua_ref, cos_ref, sin_ref, gq_ref, gkv_ref, wq_ref, wkv_ref, q_ref, k_ref, v_ref)


def _in_proj(h2, w_all, cosf, sinf, gq, gkv, wq, wkv, tm=512):
    T = h2.shape[0]
    full = lambda a: pl.BlockSpec(a.shape, lambda i: (0, 0))
    row = lambda w: pl.BlockSpec((tm, w), lambda i: (i, 0))
    return pl.pallas_call(
        _in_proj_kernel,
        grid=(T // tm,),
        in_specs=[row(D_MODEL), full(w_all), row(LANES), row(LANES), full(gq), full(gkv), full(wq), full(wkv)],
        out_specs=[row(UB_W), row(512), row(512), row(256)],
        out_shape=[jax.ShapeDtypeStruct((T, UB_W), jnp.bfloat16),
                   jax.ShapeDtypeStruct((T, 512), jnp.bfloat16),
                   jax.ShapeDtypeStruct((T, 512), jnp.bfloat16),
                   jax.ShapeDtypeStruct((T, 256), jnp.bfloat16)],
        scratch_shapes=[pltpu.VMEM((tm, UA_W), jnp.float32)],
        compiler_params=_cparams(("parallel",)),
        name="in_proj_mla_prep",
    )(h2, w_all, cosf, sinf, gq, gkv, wq, wkv)


def _rope_table_kernel(pos_ref, inv_ref, cos_ref, sin_ref):
    ang = pos_ref[...].astype(jnp.float32) * inv_ref[...]
    cos_ref[...] = jnp.cos(ang)
    sin_ref[...] = jnp.sin(ang)


def _rope_table(pos_row, inv_col, tn=2048):
    T = pos_row.shape[1]
    n = inv_col.shape[0]
    tn = min(tn, T)
    return pl.pallas_call(
        _rope_table_kernel,
        grid=(T // tn,),
        in_specs=[pl.BlockSpec((1, tn), lambda i: (0, i)),
                  pl.BlockSpec((n, 1), lambda i: (0, 0))],
        out_specs=[pl.BlockSpec((n, tn), lambda i: (0, i))] * 2,
        out_shape=[jax.ShapeDtypeStruct((n, T), jnp.float32)] * 2,
        compiler_params=_cparams(("parallel",)),
        name="rope_table",
    )(pos_row, inv_col)


def _mla_prep_kernel(ua_ref, cos_ref, sin_ref, gq_ref, gkv_ref, wq_ref, wkv_ref,
                     q_ref, k_ref, v_ref):
    ckv = ua_ref[:, UA_CKV:UA_CKV + 128]
    kr = ua_ref[:, UA_KR:UA_KR + 128]
    krs = ua_ref[:, UA_KRS:UA_KRS + 128]
    cq = ua_ref[:, UA_CQ:UA_CQ + 256]
    cqn = cq * lax.rsqrt(jnp.sum(cq * cq, -1, keepdims=True) * (1.0 / MLA_Q_RANK) + EPS) * gq_ref[...]
    ckvn = ckv * lax.rsqrt(jnp.sum(ckv * ckv, -1, keepdims=True) * (1.0 / MLA_KV_RANK) + EPS) * gkv_ref[...]
    cosf = cos_ref[...]
    sinf = sin_ref[...]
    scale = (MLA_NOPE + MLA_ROPE) ** -0.5 * LOG2E
    cqb = _bf16(cqn)
    ckvb = _bf16(ckvn)
    k_rope = kr * cosf + krs * sinf
    for h in range(MLA_HEADS):
        a = _dot(cqb, wq_ref[:, h * 128:(h + 1) * 128])
        b = _dot(cqb, wq_ref[:, 512 + h * 128:512 + (h + 1) * 128])
        q_ref[:, h * 128:(h + 1) * 128] = _bf16((a * cosf + b * sinf) * scale)
        kn = _dot(ckvb, wkv_ref[:, h * 128:(h + 1) * 128])
        k_ref[:, h * 128:(h + 1) * 128] = _bf16(kn + k_rope)
    v_ref[...] = _bf16(_dot(ckvb, wkv_ref[:, 512:768]))


def _chunk_mask(tq, tk):
    qc = lax.broadcasted_iota(jnp.int32, (tq, tk), 0) // CHUNK
    kc = lax.broadcasted_iota(jnp.int32, (tq, tk), 1) // CHUNK
    return kc <= qc


def _softmax_update(s, v, m_ref, l_ref, acc_ref, idx, shift=None):
    tk = s.shape[1]
    m_old = m_ref[idx]
    row_max = jnp.max(s, axis=1, keepdims=True)
    if shift is not None:
        row_max = row_max + shift
    m_new = jnp.maximum(m_old, row_max)
    alpha = jnp.exp2(m_old - m_new)
    m_sub = m_new if shift is None else m_new - shift
    p = jnp.exp2(s - jnp.concatenate([m_sub] * (tk // LANES), axis=1))
    psum = p[:, :LANES]
    for c in range(LANES, tk, LANES):
        psum = psum + p[:, c:c + LANES]
    l_ref[idx] = alpha * l_ref[idx] + psum
    acc_ref[idx] = alpha * acc_ref[idx] + _dot(_bf16(p), v)
    m_ref[idx] = m_new


def _softmax_init(m_ref, l_ref, acc_ref):
    m_ref[...] = jnp.full(m_ref.shape, NEG_INF, jnp.float32)
    l_ref[...] = jnp.zeros(l_ref.shape, jnp.float32)
    acc_ref[...] = jnp.zeros(acc_ref.shape, jnp.float32)


def _softmax_result(l_ref, acc_ref, idx):
    return acc_ref[idx] / jnp.sum(l_ref[idx], axis=1, keepdims=True)


def _mla_attn_kernel(q_ref, k_ref, v_ref, o_ref, m_ref, l_ref, acc_ref, *, t):
    qi = pl.program_id(1)
    _softmax_init(m_ref, l_ref, acc_ref)

    def tile(j, masked, width=1):
        ks = pl.multiple_of(j * t, t)
        for hh in range(MLA_HEADS):
            q = q_ref[:, hh * 128:(hh + 1) * 128]
            k = k_ref[pl.ds(ks, width * t), hh * 128:(hh + 1) * 128]
            v = v_ref[pl.ds(ks, width * t), (hh // 2) * 128:(hh // 2 + 1) * 128]
            s = lax.dot_general(q, k, _NT, preferred_element_type=jnp.float32)
            if masked:
                s = jnp.where(_chunk_mask(t, t), s, NEG_INF)
            _softmax_update(s, v, m_ref, l_ref, acc_ref, hh)

    done = 0
    for width in FAR_TILE_WIDTHS + (1,):
        trips = (qi - done) // width

        def body(i, carry, width=width, done=done):
            tile(done + i * width, False, width=width)
            return carry

        lax.fori_loop(0, trips, body, 0)
        done = done + trips * width

    tile(qi, True)
    lane = lax.broadcasted_iota(jnp.int32, (t, 128), 1)
    for pair in range(MLA_HEADS // 2):
        o_ref[:, pair * 128:(pair + 1) * 128] = _bf16(
            jnp.where(lane < MLA_V, _softmax_result(l_ref, acc_ref, 2 * pair),
                      _softmax_result(l_ref, acc_ref, 2 * pair + 1)))


def _mla_attn(q, k, v, B, S, t=256):
    nq = S // t
    wq, wv = MLA_HEADS * 128, MLA_HEADS * MLA_V
    return pl.pallas_call(
        functools.partial(_mla_attn_kernel, t=t),
        grid=(B, nq),
        in_specs=[pl.BlockSpec((t, wq), lambda b, i: (b * nq + i, 0)),
                  pl.BlockSpec((S, wq), lambda b, i: (b, 0)),
                  pl.BlockSpec((S, wv), lambda b, i: (b, 0))],
        out_specs=pl.BlockSpec((t, wv), lambda b, i: (b * nq + i, 0)),
        out_shape=jax.ShapeDtypeStruct((B * S, wv), jnp.bfloat16),
        scratch_shapes=[pltpu.VMEM((MLA_HEADS, t, LANES), jnp.float32)] * 3,
        compiler_params=_cparams(("parallel", "arbitrary")),
        name="mla_attn",
    )(q, k, v)


def _t5_bias(rel, rb_ref, h):
    nb = REL_BUCKETS // 2
    max_exact = nb // 2
    n = jnp.abs(rel)
    nf = jnp.maximum(n, 1).astype(jnp.float32)
    large = max_exact + (jnp.log(nf / max_exact) / math.log(REL_MAX_DIST / max_exact)
                         * (nb - max_exact)).astype(jnp.int32)
    large = jnp.minimum(large, nb - 1)
    low = jnp.where(n < max_exact, n, large)
    neg = jnp.zeros(rel.shape, jnp.float32)
    pos = jnp.zeros(rel.shape, jnp.float32)
    for j in range(nb):
        eq = low == j
        neg = jnp.where(eq, rb_ref[h * REL_BUCKETS + j], neg)
        pos = jnp.where(eq, rb_ref[h * REL_BUCKETS + nb + j], pos)
    return jnp.where(rel > 0, pos, neg)


def _diff_attn_kernel(qmin_ref, kmax_ref, first_ref, consec_ref,
                      rb_ref, q_ref, k_ref, v_ref, posq_ref, posk_ref,
                      lq1_ref, lk1_ref, lq2_ref, lk2_ref, sub_ref,
                      o_ref, m_ref, l_ref, acc_ref, bias_ref, cache_ref, *, t, nq, lam_init):
    b = pl.program_id(0)
    hp = pl.program_id(1)
    qi = pl.program_id(2)
    _softmax_init(m_ref, l_ref, acc_ref)

    @pl.when(qi == 0)
    def _():
        for n in range(4):
            cache_ref[n] = 0

    lane = lax.broadcasted_iota(jnp.int32, (t, LANES), 1)
    qs = []
    for hh in range(DIFF_HEADS_PER_STEP):
        q = q_ref[:, hh * LANES:(hh + 1) * LANES]
        zero = jnp.zeros_like(q)
        qs.append(jnp.concatenate([jnp.where(lane < DIFF_QK, q, zero), jnp.where(lane < DIFF_QK, zero, q)], axis=0))
    tq = b * nq + qi
    qmin = qmin_ref[tq]

    def toeplitz_bias(j, h):
        d0 = first_ref[b * nq + j] - first_ref[tq]
        x = lax.broadcasted_iota(jnp.int32, (1, 2 * t), 1)
        g = _t5_bias(d0 + jnp.where(x < t, x, x - 2 * t), rb_ref, h)
        g = pltpu.roll(jnp.broadcast_to(g, (t, 2 * t)), 0, 1, stride=1, stride_axis=0)
        return g[:, :t]

    def general_bias(j, h):
        rel = posk_ref[j] - posq_ref[...]
        return _t5_bias(rel, rb_ref, h)

    def masked_bias(bias, masked):
        bias = bias * LOG2E
        return jnp.where(_chunk_mask(t, t), bias, NEG_INF) if masked else bias

    def tile(j, masked, bias_fn, width=1):
        ks = pl.multiple_of(j * t, t)
        for hh in range(DIFF_HEADS_PER_STEP):
            h = hp * DIFF_HEADS_PER_STEP + hh
            k = k_ref[pl.ds(ks, width * t), hh * LANES:(hh + 1) * LANES]
            v = v_ref[pl.ds(ks, width * t), hh * LANES:(hh + 1) * LANES]
            s = lax.dot_general(qs[hh], k, _NT, preferred_element_type=jnp.float32)
            if bias_fn is None:
                shift = rb_ref[h * REL_BUCKETS + REL_BUCKETS // 2 - 1] * LOG2E
            else:
                shift = None
                s = (s.reshape(2, t, t) + bias_fn(j, h, hh)[None]).reshape(2 * t, t)
            _softmax_update(s, v, m_ref, l_ref, acc_ref, hh, shift)

    def near_tile(j, masked):
        consecutive = jnp.logical_and(consec_ref[tq] == 1, consec_ref[b * nq + j] == 1)
        slot = 1 if masked else 0

        @pl.when(consecutive)
        def _():
            d0 = first_ref[b * nq + j] - first_ref[tq]
            stale = jnp.logical_or(cache_ref[2 * slot + 1] != 1, cache_ref[2 * slot] != d0)

            @pl.when(stale)
            def _():
                for hh in range(DIFF_HEADS_PER_STEP):
                    h = hp * DIFF_HEADS_PER_STEP + hh
                    bias_ref[slot, hh] = masked_bias(toeplitz_bias(j, h), masked)
                cache_ref[2 * slot] = d0
                cache_ref[2 * slot + 1] = 1

            tile(j, masked, lambda j, h, hh: bias_ref[slot, hh])

        @pl.when(jnp.logical_not(consecutive))
        def _():
            tile(j, masked, lambda j, h, hh: masked_bias(general_bias(j, h), masked))

    def is_far(j):
        return kmax_ref[b * nq + jnp.minimum(j, qi)] - qmin <= -T5_FAR

    def wide_far_loop(start, width):
        def cond(j):
            ok = j + width - 1 < qi
            for d in range(width):
                ok = jnp.logical_and(ok, is_far(j + d))
            return ok

        def step(j):
            tile(j, False, None, width=width)
            return j + width

        return lax.while_loop(cond, step, start)

    def body(j, carry):
        far = is_far(j)

        @pl.when(far)
        def _():
            tile(j, False, None)

        @pl.when(jnp.logical_not(far))
        def _():
            near_tile(j, False)

        return carry

    done = 0
    for width in FAR_TILE_WIDTHS:
        done = wide_far_loop(done, width)
    lax.fori_loop(done, qi, body, 0)
    near_tile(qi, True)

    f32 = jnp.float32
    lam = (jnp.exp(jnp.sum(lq1_ref[...].astype(f32) * lk1_ref[...].astype(f32), keepdims=True))
           - jnp.exp(jnp.sum(lq2_ref[...].astype(f32) * lk2_ref[...].astype(f32), keepdims=True))
           + lam_init)
    for hh in range(DIFF_HEADS_PER_STEP):
        a = _softmax_result(l_ref, acc_ref, hh)
        o = a[:t] - lam * a[t:]
        o = o * lax.rsqrt(jnp.mean(o * o, -1, keepdims=True) + EPS) * sub_ref[...] * (1.0 - lam_init)
        o_ref[:, hh * LANES:(hh + 1) * LANES] = _bf16(o)


def _diff_attn(ub, pos_col, pos_tiles, qmin, kmax, first, consec, rb_flat, lq1, lk1, lq2, lk2, subln,
               B, S, lam_init, t=256):
    nq = S // t
    per = DIFF_HEADS_PER_STEP
    G = DIFF_HEADS // per
    small = lambda a: pl.BlockSpec(a.shape, lambda b, h, i, *_: (0, 0))
    grid_spec = pltpu.PrefetchScalarGridSpec(
        num_scalar_prefetch=4,
        grid=(B, G, nq),
        in_specs=[pl.BlockSpec(memory_space=pltpu.SMEM),
                  pl.BlockSpec((t, per * 128), lambda b, h, i, *_: (b * nq + i, h)),
                  pl.BlockSpec((S, per * 128), lambda b, h, i, *_: (b, G + h)),
                  pl.BlockSpec((S, per * 128), lambda b, h, i, *_: (b, 2 * G + h)),
                  pl.BlockSpec((t, 1), lambda b, h, i, *_: (b * nq + i, 0)),
                  pl.BlockSpec((nq, 1, t), lambda b, h, i, *_: (b, 0, 0)),
                  small(lq1), small(lk1), small(lq2), small(lk2), small(subln)],
        out_specs=pl.BlockSpec((t, per * 128), lambda b, h, i, *_: (b * nq + i, h)),
        scratch_shapes=[pltpu.VMEM((per, 2 * t, LANES), jnp.float32)] * 3
                       + [pltpu.VMEM((2, per, t, t), jnp.float32), pltpu.SMEM((4,), jnp.int32)],
    )
    return pl.pallas_call(
        functools.partial(_diff_attn_kernel, t=t, nq=nq, lam_init=lam_init),
        grid_spec=grid_spec,
        out_shape=jax.ShapeDtypeStruct((B * S, DIFF_HEADS * DIFF_V), jnp.bfloat16),
        compiler_params=_cparams(("parallel", "parallel", "arbitrary")),
        name="diff_attn",
    )(qmin, kmax, first, consec, rb_flat, ub, ub, ub, pos_col, pos_tiles, lq1, lk1, lq2, lk2, subln)


def _sb_attn_kernel(q_ref, k_ref, v_ref, o_ref, c_ref, acc_ref, *, t):
    qi = pl.program_id(1)
    n_pairs = SB_HEADS // 2
    row = lax.broadcasted_iota(jnp.int32, (t, t), 0)
    col = lax.broadcasted_iota(jnp.int32, (t, t), 1)
    tri = jnp.where(row > col, 1.0, 0.0).astype(jnp.bfloat16)
    strict = (col < row)[None]
    lane = lax.broadcasted_iota(jnp.int32, (t, LANES), 1)
    qs = []
    for hp in range(n_pairs):
        q = q_ref[:, hp * LANES:(hp + 1) * LANES]
        zero = jnp.zeros_like(q)
        qs.append(jnp.concatenate([jnp.where(lane < SB_D, q, zero), jnp.where(lane < SB_D, zero, q)], axis=0))
    c_ref[...] = jnp.zeros(c_ref.shape, jnp.float32)
    acc_ref[...] = jnp.zeros(acc_ref.shape, jnp.float32)

    def tile(j, diag):
        ks = pl.multiple_of(j * t, t)
        for hp in range(n_pairs):
            k = k_ref[pl.ds(ks, t), hp * LANES:(hp + 1) * LANES]
            v = v_ref[pl.ds(ks, t), hp * LANES:(hp + 1) * LANES]
            z = lax.dot_general(qs[hp], k, _NT, preferred_element_type=jnp.float32)
            lf = -(jnp.maximum(z, 0.0) + jnp.log(1.0 + jnp.exp(-jnp.abs(z))))
            if diag:
                lf = jnp.where(strict, lf.reshape(2, t, t), 0.0).reshape(2 * t, t)
            hi = _bf16(lf)
            lo = _bf16(lf - hi.astype(jnp.float32))
            c = c_ref[hp]
            later = _dot(hi, tri) + _dot(lo, tri) + jnp.concatenate([c] * (t // LANES), axis=1)
            w = jnp.exp(lf + z + later)
            if diag:
                w = jnp.where(strict, w.reshape(2, t, t), 0.0).reshape(2 * t, t)
            acc_ref[hp] += _dot(_bf16(w), v)
            c_ref[hp] = c + jnp.sum(lf, axis=1, keepdims=True)

    tile(qi, True)

    def cond(carry):
        j, cmax = carry
        return jnp.logical_and(j >= 0, cmax > SB_LOG_ZERO)

    def body(carry):
        j, _ = carry
        tile(j, False)
        return j - 1, jnp.max(c_ref[...])

    lax.while_loop(cond, body, (qi - 1, jnp.max(c_ref[...])))
    for hp in range(n_pairs):
        acc = acc_ref[hp]
        o_ref[:, hp * LANES:(hp + 1) * LANES] = _bf16(jnp.where(lane < SB_D, acc[:t], acc[t:]))


def _sb_attn(ub, B, S, t=256):
    nq = S // t
    w = SB_HEADS * SB_D
    c0 = DIFF_IN // w
    return pl.pallas_call(
        functools.partial(_sb_attn_kernel, t=t),
        grid=(B, nq),
        in_specs=[pl.BlockSpec((t, w), lambda b, i: (b * nq + i, c0)),
                  pl.BlockSpec((S, w), lambda b, i: (b, c0 + 1)),
                  pl.BlockSpec((S, w), lambda b, i: (b, c0 + 2))],
        out_specs=pl.BlockSpec((t, w), lambda b, i: (b * nq + i, 0)),
        out_shape=jax.ShapeDtypeStruct((B * S, w), jnp.bfloat16),
        scratch_shapes=[pltpu.VMEM((SB_HEADS // 2, 2 * t, LANES), jnp.float32)] * 2,
        compiler_params=_cparams(("parallel", "arbitrary")),
        name="sb_attn",
    )(ub, ub, ub)


def _layer_norm(x, g, b):
    mu = jnp.mean(x, -1, keepdims=True)
    xc = x - mu
    var = jnp.mean(xc * xc, -1, keepdims=True)
    return xc * lax.rsqrt(var + EPS) * g + b


def _out_proj_kernel(h_ref, ym_ref, yd_ref, ys_ref, wm_ref, wd_ref, ws_ref, g_ref, b_ref, o_ref, *, alpha):
    mix = _dot(ym_ref[...], wm_ref[...]) + _dot(yd_ref[...], wd_ref[...]) + _dot(ys_ref[...], ws_ref[...])
    o_ref[...] = _layer_norm(alpha * h_ref[...] + mix, g_ref[...], b_ref[...])


def _out_proj(h2, ym, yd, ys, wm, wd, ws, g, b, alpha, tm=512):
    T = h2.shape[0]
    row = lambda a: pl.BlockSpec((tm, a.shape[1]), lambda i: (i, 0))
    full = lambda a: pl.BlockSpec(a.shape, lambda i: (0, 0))
    return pl.pallas_call(
        functools.partial(_out_proj_kernel, alpha=alpha),
        grid=(T // tm,),
        in_specs=[row(h2), row(ym), row(yd), row(ys), full(wm), full(wd), full(ws), full(g), full(b)],
        out_specs=pl.BlockSpec((tm, D_MODEL), lambda i: (i, 0)),
        out_shape=jax.ShapeDtypeStruct((T, D_MODEL), jnp.float32),
        compiler_params=_cparams(("parallel",)),
        name="out_proj_ln",
    )(h2, ym, yd, ys, wm, wd, ws, g, b)


def _first_max(vals):
    m = vals[0]
    for v in vals[1:]:
        m = jnp.maximum(m, v)
    taken = jnp.zeros(m.shape, jnp.bool_)
    hot = []
    for v in vals:
        is_first = jnp.logical_and(v == m, jnp.logical_not(taken))
        hot.append(is_first)
        taken = jnp.logical_or(taken, is_first)
    return m, hot


def _router_kernel(h_ref, rw_ref, rb_ref, gate_ref, slot_ref, dest_ref, chosen_ref, first_ref):
    h = h_ref[...]
    rw = rw_ref[...]
    h_hi, rw_hi = _bf16(h), _bf16(rw)
    h_lo, rw_lo = _bf16(h - h_hi.astype(jnp.float32)), _bf16(rw - rw_hi.astype(jnp.float32))
    logits = (_dot(h_hi, rw_hi) + (_dot(h_hi, rw_lo) + _dot(h_lo, rw_hi))).T[:N_EXPERTS]
    scores = jax.nn.sigmoid(logits)
    sel = scores + rb_ref[...]
    ninf = -jnp.inf
    group_score, first, second = [], [], []
    for g in range(N_GROUPS):
        vals = [sel[g * EXPERTS_PER_GROUP + k:g * EXPERTS_PER_GROUP + k + 1, :] for k in range(EXPERTS_PER_GROUP)]
        m1, hot1 = _first_max(vals)
        m2, hot2 = _first_max([jnp.where(hh, ninf, v) for hh, v in zip(hot1, vals)])
        group_score.append(m1 + m2)
        first.append(hot1)
        second.append(hot2)
    _, best = _first_max(group_score)
    w1 = jnp.zeros_like(group_score[0])
    w2 = jnp.zeros_like(group_score[0])
    for g in range(N_GROUPS):
        for k in range(EXPERTS_PER_GROUP):
            e = g * EXPERTS_PER_GROUP + k
            sc = scores[e:e + 1, :]
            w1 = jnp.where(jnp.logical_and(best[g], first[g][k]), sc, w1)
            w2 = jnp.where(jnp.logical_and(best[g], second[g][k]), sc, w2)
    tot = w1 + w2
    for g in range(N_GROUPS):
        for k in range(EXPERTS_PER_GROUP):
            e = g * EXPERTS_PER_GROUP + k
            is1 = jnp.logical_and(best[g], first[g][k])
            is2 = jnp.logical_and(best[g], second[g][k])
            gate_ref[e:e + 1, :] = jnp.where(is1, w1 / tot, 0.0) + jnp.where(is2, w2 / tot, 0.0)
            chosen_ref[e:e + 1, :] = jnp.where(jnp.logical_or(is1, is2), 1.0, 0.0)
            first_ref[e:e + 1, :] = jnp.where(is1, 1.0, 0.0)
    chosen = chosen_ref[...]
    tm = chosen.shape[1]
    before = (lax.broadcasted_iota(jnp.int32, (tm, tm), 0) < lax.broadcasted_iota(jnp.int32, (tm, tm), 1))
    rank = _dot(_bf16(chosen), jnp.where(before, 1.0, 0.0).astype(jnp.bfloat16)).astype(jnp.int32)
    slot_ref[...] = jnp.where(chosen > 0.0, rank, -1)
    expert = lax.broadcasted_iota(jnp.int32, chosen.shape, 0)
    parked = jnp.logical_and(chosen > 0.0, rank < MOE_SLOTS)
    code = jnp.where(parked, expert * MOE_SLOTS + rank + 1, 0)
    is_first = first_ref[...] > 0.0
    dest_ref[0:1, :] = jnp.sum(jnp.where(is_first, code, 0), axis=0, keepdims=True) - 1
    dest_ref[1:2, :] = jnp.sum(jnp.where(is_first, 0, code), axis=0, keepdims=True) - 1


def _router(h2, rw_pad, rb_col):
    T = h2.shape[0]
    tm = MOE_SUB
    tok = lambda rows: pl.BlockSpec((rows, tm), lambda i: (0, i))
    return pl.pallas_call(
        _router_kernel,
        grid=(T // tm,),
        in_specs=[pl.BlockSpec((tm, D_MODEL), lambda i: (i, 0)),
                  pl.BlockSpec(rw_pad.shape, lambda i: (0, 0)),
                  pl.BlockSpec(rb_col.shape, lambda i: (0, 0))],
        out_specs=[tok(N_EXPERTS), tok(N_EXPERTS), tok(2)],
        out_shape=[jax.ShapeDtypeStruct((N_EXPERTS, T), jnp.float32),
                   jax.ShapeDtypeStruct((N_EXPERTS, T), jnp.int32),
                   jax.ShapeDtypeStruct((2, T), jnp.int32)],
        scratch_shapes=[pltpu.VMEM((N_EXPERTS, tm), jnp.float32)] * 2,
        compiler_params=_cparams(("parallel",)),
        name="router",
    )(h2, rw_pad, rb_col)


def _moe_kernel(npass_ref, h_ref, slot_ref, slot_t_ref, dest_t_ref, gate_ref, p_ref, wg_ref, wu_ref, wd_ref,
                pg_ref, pp_ref, g_ref, b_ref, o_ref, over_ref, xb_ref, y_ref, *, alpha, n_sub):
    i = pl.program_id(0)
    step = pl.program_id(1)
    R, SUB = MOE_SLOTS, MOE_SUB

    @pl.when(step == 0)
    def _():
        xb_ref[...] = _bf16(h_ref[...])
        over_ref[...] = jnp.zeros(over_ref.shape, jnp.float32)

    def expert_pass(e, k, c):
        slot_row = slot_ref[pl.ds(e, 1), :]
        gate_row = gate_ref[pl.ds(e, 1), :]
        xs, gs = [], []
        for j in range(n_sub):
            sl = slot_row[:, j * SUB:(j + 1) * SUB] - c * R
            hit = lax.broadcasted_iota(jnp.int32, (R, SUB), 0) == sl
            onehot = jnp.where(hit, 1.0, 0.0).astype(jnp.bfloat16)
            xs.append(_bf16(_dot(onehot, xb_ref[j * SUB:(j + 1) * SUB, :])))
            gs.append(jnp.sum(jnp.where(hit, gate_row[:, j * SUB:(j + 1) * SUB], 0.0), axis=1, keepdims=True))
        xe = jnp.concatenate(xs, axis=0)
        hid = jax.nn.silu(_dot(xe, wg_ref[k])) * _dot(xe, wu_ref[k])
        y = _dot(_bf16(hid), wd_ref[k])
        return [_bf16(y[j * R:(j + 1) * R] * gs[j]) for j in range(n_sub)]

    for k in range(MOE_EXPERTS_PER_STEP):
        e = step * MOE_EXPERTS_PER_STEP + k
        ys = expert_pass(e, k, 0)
        for j in range(n_sub):
            y_ref[j, pl.ds(pl.multiple_of(e * R, 16), R), :] = ys[j]

    for k in range(MOE_EXPERTS_PER_STEP):
        e = step * MOE_EXPERTS_PER_STEP + k

        def overflow_pass(c, carry, e=e, k=k):
            ys = expert_pass(e, k, c)
            lane16 = lax.broadcasted_iota(jnp.int32, (SUB, N_EXPERTS), 1)
            for j in range(n_sub):
                rows = slice(j * SUB, (j + 1) * SUB)
                slot_col = jnp.sum(jnp.where(lane16 == e, slot_t_ref[rows, :], 0), axis=1, keepdims=True)
                hit_t = lax.broadcasted_iota(jnp.int32, (SUB, R), 1) == slot_col - c * R
                over_ref[rows, :] += _dot(jnp.where(hit_t, 1.0, 0.0).astype(jnp.bfloat16), ys[j])
            return carry

        lax.fori_loop(1, npass_ref[i * N_EXPERTS + e], overflow_pass, 0)

    @pl.when(step == N_EXPERTS // MOE_EXPERTS_PER_STEP - 1)
    def _():
        lane = lax.broadcasted_iota(jnp.int32, (SUB, N_EXPERTS * R), 1)
        for j in range(n_sub):
            rows = slice(j * SUB, (j + 1) * SUB)
            dest = dest_t_ref[rows, :]
            hit = jnp.logical_or(lane == dest[:, 0:1], lane == dest[:, 1:2])
            scatter = jnp.where(hit, 1.0, 0.0).astype(jnp.bfloat16)
            ffn = _dot(scatter, y_ref[j]) + over_ref[rows, :]
            h = h_ref[rows, :]
            ple = jax.nn.sigmoid(_dot(xb_ref[rows, :], pg_ref[...])) * _dot(_bf16(p_ref[rows, :]), pp_ref[...])
            o_ref[rows, :] = _layer_norm(alpha * h + ffn + ple, g_ref[...], b_ref[...])


def _moe(h2, slot, dest, gate, p2, wg, wu, wd, layer, pg, pp, g, b, alpha, tm=1024):
    T = h2.shape[0]
    n_sub = tm // MOE_SUB
    n_tiles = T // tm
    per = MOE_EXPERTS_PER_STEP
    count = jnp.sum((slot >= 0).reshape(N_EXPERTS, n_tiles, n_sub, MOE_SUB), axis=3)
    npass = jnp.maximum(1, (jnp.max(count, axis=2) + MOE_SLOTS - 1) // MOE_SLOTS).T.reshape(-1).astype(jnp.int32)
    full = lambda a: pl.BlockSpec(a.shape, lambda i, e, *_: (0, 0))
    experts = lambda a: pl.BlockSpec((None, per) + a.shape[2:], lambda i, e, *_: (layer, e, 0, 0))
    grid_spec = pltpu.PrefetchScalarGridSpec(
        num_scalar_prefetch=1,
        grid=(n_tiles, N_EXPERTS // per),
        in_specs=[pl.BlockSpec((tm, D_MODEL), lambda i, e, *_: (i, 0)),
                  pl.BlockSpec((N_EXPERTS, tm), lambda i, e, *_: (0, i)),
                  pl.BlockSpec((tm, N_EXPERTS), lambda i, e, *_: (i, 0)),
                  pl.BlockSpec((tm, 2), lambda i, e, *_: (i, 0)),
                  pl.BlockSpec((N_EXPERTS, tm), lambda i, e, *_: (0, i)),
                  pl.BlockSpec((tm, PLE_DIM), lambda i, e, *_: (i, 0)),
                  experts(wg), experts(wu), experts(wd),
                  full(pg), full(pp), full(g), full(b)],
        out_specs=pl.BlockSpec((tm, D_MODEL), lambda i, e, *_: (i, 0)),
        scratch_shapes=[pltpu.VMEM((tm, D_MODEL), jnp.float32),
                        pltpu.VMEM((tm, D_MODEL), jnp.bfloat16),
                        pltpu.VMEM((n_sub, N_EXPERTS * MOE_SLOTS, D_MODEL), jnp.bfloat16)],
    )
    return pl.pallas_call(
        functools.partial(_moe_kernel, alpha=alpha, n_sub=n_sub),
        grid_spec=grid_spec,
        out_shape=jax.ShapeDtypeStruct((T, D_MODEL), jnp.float32),
        compiler_params=_cparams(("parallel", "arbitrary")),
        name="moe_ple_ln",
    )(npass, h2, slot, slot.T, dest.T, gate, p2, wg, wu, wd, pg, pp, g, b)


def _rotate_half_cols(w):
    half = w.shape[1] // 2
    return jnp.concatenate([-w[:, half:], w[:, :half]], axis=1)


def _in_proj_weights(w_in):
    z = lambda n: jnp.zeros((D_MODEL, n), w_in.dtype)
    cq = w_in[:, :MLA_Q_RANK]
    ckv = w_in[:, MLA_Q_RANK:MLA_Q_RANK + MLA_KV_RANK]
    kr = w_in[:, MLA_Q_RANK + MLA_KV_RANK:MLA_IN]
    n_dq = DIFF_HEADS * 2 * DIFF_QK
    sb0 = MLA_IN + DIFF_IN
    n_sq = SB_HEADS * SB_D
    cols = [ckv,
            z(MLA_NOPE), kr, z(128 - MLA_NOPE - MLA_ROPE),
            z(MLA_NOPE), _rotate_half_cols(kr), z(128 - MLA_NOPE - MLA_ROPE),
            cq, z(256 - MLA_Q_RANK),
            w_in[:, MLA_IN:MLA_IN + n_dq] * (DIFF_QK ** -0.5 * LOG2E),
            w_in[:, MLA_IN + n_dq:sb0],
            w_in[:, sb0:sb0 + n_sq] * (SB_D ** -0.5),
            w_in[:, sb0 + n_sq:]]
    return _bf16(jnp.concatenate(cols, axis=1))


def _mla_up_weights(w_uq, w_ukv):
    dq = MLA_NOPE + MLA_ROPE
    zq = lambda n: jnp.zeros((MLA_Q_RANK, n), w_uq.dtype)
    plain, rot = [], []
    for h in range(MLA_HEADS):
        wh = w_uq[:, h * dq:(h + 1) * dq]
        plain += [wh, zq(128 - dq)]
        rot += [zq(MLA_NOPE), _rotate_half_cols(wh[:, MLA_NOPE:]), zq(128 - dq)]
    wq = jnp.concatenate(plain + rot, axis=1)
    wq = jnp.concatenate([wq, jnp.zeros((256 - MLA_Q_RANK, wq.shape[1]), wq.dtype)], axis=0)
    dkv = MLA_NOPE + MLA_V
    zk = lambda n: jnp.zeros((MLA_KV_RANK, n), w_ukv.dtype)
    kcols, vcols = [], []
    for h in range(MLA_HEADS):
        wh = w_ukv[:, h * dkv:(h + 1) * dkv]
        kcols += [wh[:, :MLA_NOPE], zk(128 - MLA_NOPE)]
        vcols += [wh[:, MLA_NOPE:]]
    wkv = jnp.concatenate(kcols + vcols, axis=1)
    return _bf16(wq), _bf16(wkv)


def _attn_tiles(S):
    pick = lambda want: max(c for c in (128, 256, 512, 1024) if c <= want and S % c == 0)
    return pick(512), pick(512), pick(256)


def kernel(x, p, positions, w_in, mla_q_norm, mla_w_uq, mla_kv_norm, mla_w_ukv, diff_lambda_q1, diff_lambda_k1, diff_lambda_q2, diff_lambda_k2, diff_subln, rel_bias, w_o, ln1_g, ln1_b, router_w, router_b, w_gate, w_up, w_down, ple_proj, ple_gate, ln2_g, ln2_b):
    B, S, _ = x.shape
    depth = w_in.shape[0]
    T = B * S
    alpha = (2 * depth) ** 0.25
    t_mla, t_diff, t_sb = _attn_tiles(S)
    nq = S // t_diff

    pos_col = positions.reshape(T, 1)
    pos_tiles = positions.reshape(B * nq, 1, t_diff)
    tile_pos = positions.reshape(B * nq, t_diff)
    qmin = jnp.min(tile_pos, axis=1)
    kmax = jnp.max(tile_pos, axis=1)
    first = tile_pos[:, 0]
    consec = jnp.all(tile_pos[:, 1:] - tile_pos[:, :-1] == 1, axis=1).astype(jnp.int32)
    rb_flat = rel_bias.T.reshape(-1).astype(jnp.float32)

    half = MLA_ROPE // 2
    inv = ROPE_THETA ** (-jnp.arange(half, dtype=jnp.float32) / half)
    cos_t, sin_t = _rope_table(positions.reshape(1, T), jnp.concatenate([inv, inv]).reshape(MLA_ROPE, 1))
    pad_l = lambda v: jnp.full((T, MLA_NOPE), v, jnp.float32)
    pad_r = jnp.zeros((T, LANES - MLA_NOPE - MLA_ROPE), jnp.float32)
    cosf = jnp.concatenate([pad_l(1.0), cos_t.T, pad_r], axis=1)
    sinf = jnp.concatenate([pad_l(0.0), sin_t.T, pad_r], axis=1)

    rw_pad = jnp.pad(router_w.astype(jnp.float32), ((0, 0), (0, LANES - N_EXPERTS)))
    rb_col = router_b.reshape(N_EXPERTS, 1).astype(jnp.float32)
    row = lambda a: a.reshape(1, -1)

    wg_all, wu_all, wd_all = _bf16(w_gate), _bf16(w_up), _bf16(w_down)

    h = x.reshape(T, D_MODEL)
    for i in range(depth):
        lam_init = 0.8 - 0.6 * math.exp(-0.3 * i)
        wq, wkv = _mla_up_weights(mla_w_uq[i], mla_w_ukv[i])
        gq = jnp.concatenate([mla_q_norm[i], jnp.zeros((256 - MLA_Q_RANK,), mla_q_norm.dtype)]).reshape(1, 256)
        ub, q_m, k_m, v_m = _in_proj(h, _in_proj_weights(w_in[i]), cosf, sinf, gq, row(mla_kv_norm[i]), wq, wkv)
        y_mla = _mla_attn(q_m, k_m, v_m, B, S, t=t_mla)
        y_diff = _diff_attn(ub, pos_col, pos_tiles, qmin, kmax, first, consec, rb_flat,
                            row(diff_lambda_q1[i]), row(diff_lambda_k1[i]),
                            row(diff_lambda_q2[i]), row(diff_lambda_k2[i]), row(diff_subln[i]),
                            B, S, lam_init, t=t_diff)
        y_sb = _sb_attn(ub, B, S, t=t_sb)
        wo = _bf16(w_o[i])
        n_m, n_d = MLA_HEADS * MLA_V, DIFF_HEADS * DIFF_V
        h = _out_proj(h, y_mla, y_diff, y_sb, wo[:n_m], wo[n_m:n_m + n_d], wo[n_m + n_d:],
                      row(ln1_g[i]), row(ln1_b[i]), alpha)
        gate, slot, dest = _router(h, rw_pad, rb_col)
        h = _moe(h, slot, dest, gate, p[i].reshape(T, PLE_DIM), wg_all, wu_all, wd_all, i,
                 _bf16(ple_gate[i]), _bf16(ple_proj[i]), row(ln2_g[i]), row(ln2_b[i]), alpha)
    return h.reshape(B, S, D_MODEL)
```

```python
import functools
import math

import jax
import jax.numpy as jnp
from jax import lax
from jax.experimental import pallas as pl
from jax.experimental.pallas import tpu as pltpu

D_MODEL = 1024
CHUNK = 64
PLE_DIM = 256
MLA_HEADS = 4
MLA_NOPE = 64
MLA_ROPE = 32
MLA_V = 64
MLA_Q_RANK = 192
MLA_KV_RANK = 128
ROPE_THETA = 10000.0
DIFF_HEADS = 4
DIFF_QK = 64
DIFF_V = 2 * DIFF_QK
SB_HEADS = 4
SB_D = 64
REL_BUCKETS = 32
REL_MAX_DIST = 128
N_EXPERTS = 16
N_GROUPS = 4
EXPERTS_PER_GROUP = N_EXPERTS // N_GROUPS
D_EXPERT = 512
MLA_IN = MLA_Q_RANK + MLA_KV_RANK + MLA_ROPE
DIFF_IN = 2 * DIFF_HEADS * 2 * DIFF_QK + DIFF_HEADS * DIFF_V
SB_IN = 3 * SB_HEADS * SB_D
EPS = 1e-5
NEG_INF = -1e30

LANES = 128
VMEM_LIMIT = 56 * 1024 * 1024

UA_CKV = 0
UA_KR = UA_CKV + MLA_KV_RANK
UA_KRS = UA_KR + LANES
UA_CQ = UA_KRS + LANES
UA_CQ_W = 2 * LANES
UA_W = UA_CQ + UA_CQ_W
UB_W = DIFF_IN + SB_IN
MLA_QK_W = MLA_HEADS * LANES
MLA_V_W = MLA_HEADS * MLA_V

T5_FAR = 91
SB_LOG_ZERO = -88.0
TOKEN_TILE = 512
PROJ_COLS = 512
MOE_TILE = 1024
DIFF_HEADS_PER_STEP = 2
FAR_TILE_WIDTHS = (4, 2)
MOE_SUB = 512
MOE_SLOTS = 96
MOE_EXPERTS_PER_STEP = 2

_NT = (((1,), (1,)), ((), ()))
LOG2E = math.log2(math.e)


def _cparams(sem):
    return pltpu.CompilerParams(dimension_semantics=sem, vmem_limit_bytes=VMEM_LIMIT)


def _bf16(a):
    return a.astype(jnp.bfloat16)


def _dot(a, b):
    return jnp.dot(a, b, preferred_element_type=jnp.float32)


def _in_proj_kernel(x_ref, w_ref, cos_ref, sin_ref, gq_ref, gkv_ref, wq_ref, wkv_ref,
                    ub_ref, q_ref, k_ref, v_ref, ua_ref):
    x = _bf16(x_ref[...])
    step = PROJ_COLS
    for c in range(0, UA_W, step):
        e = min(c + step, UA_W)
        ua_ref[:, c:e] = _dot(x, w_ref[:, c:e])
    for c in range(0, UB_W, step):
        e = min(c + step, UB_W)
        ub_ref[:, c:e] = _bf16(_dot(x, w_ref[:, UA_W + c:UA_W + e]))
    _mla_prep_kernel(ua_ref, cos_ref, sin_ref, gq_ref, gkv_ref, wq_ref, wkv_ref, q_ref, k_ref, v_ref)


def _in_proj(h2, w_all, cosf, sinf, gq, gkv, wq, wkv, tm=TOKEN_TILE):
    T = h2.shape[0]
    full = lambda a: pl.BlockSpec(a.shape, lambda i: (0, 0))
    row = lambda w: pl.BlockSpec((tm, w), lambda i: (i, 0))
    return pl.pallas_call(
        _in_proj_kernel,
        grid=(T // tm,),
        in_specs=[row(D_MODEL), full(w_all), row(LANES), row(LANES), full(gq), full(gkv), full(wq), full(wkv)],
        out_specs=[row(UB_W), row(MLA_QK_W), row(MLA_QK_W), row(MLA_V_W)],
        out_shape=[jax.ShapeDtypeStruct((T, UB_W), jnp.bfloat16),
                   jax.ShapeDtypeStruct((T, MLA_QK_W), jnp.bfloat16),
                   jax.ShapeDtypeStruct((T, MLA_QK_W), jnp.bfloat16),
                   jax.ShapeDtypeStruct((T, MLA_V_W), jnp.bfloat16)],
        scratch_shapes=[pltpu.VMEM((tm, UA_W), jnp.float32)],
        compiler_params=_cparams(("parallel",)),
        name="in_proj_mla_prep",
    )(h2, w_all, cosf, sinf, gq, gkv, wq, wkv)


def _rope_table_kernel(pos_ref, inv_ref, cos_ref, sin_ref):
    ang = pos_ref[...].astype(jnp.float32) * inv_ref[...]
    cos_ref[...] = jnp.cos(ang)
    sin_ref[...] = jnp.sin(ang)


def _rope_table(pos_row, inv_col, tn=2048):
    T = pos_row.shape[1]
    n = inv_col.shape[0]
    tn = min(tn, T)
    return pl.pallas_call(
        _rope_table_kernel,
        grid=(T // tn,),
        in_specs=[pl.BlockSpec((1, tn), lambda i: (0, i)),
                  pl.BlockSpec((n, 1), lambda i: (0, 0))],
        out_specs=[pl.BlockSpec((n, tn), lambda i: (0, i))] * 2,
        out_shape=[jax.ShapeDtypeStruct((n, T), jnp.float32)] * 2,
        compiler_params=_cparams(("parallel",)),
        name="rope_table",
    )(pos_row, inv_col)


def _mla_prep_kernel(ua_ref, cos_ref, sin_ref, gq_ref, gkv_ref, wq_ref, wkv_ref,
                     q_ref, k_ref, v_ref):
    ckv = ua_ref[:, UA_CKV:UA_CKV + MLA_KV_RANK]
    kr = ua_ref[:, UA_KR:UA_KR + LANES]
    krs = ua_ref[:, UA_KRS:UA_KRS + LANES]
    cq = ua_ref[:, UA_CQ:UA_CQ + UA_CQ_W]
    cqn = cq * lax.rsqrt(jnp.sum(cq * cq, -1, keepdims=True) * (1.0 / MLA_Q_RANK) + EPS) * gq_ref[...]
    ckvn = ckv * lax.rsqrt(jnp.sum(ckv * ckv, -1, keepdims=True) * (1.0 / MLA_KV_RANK) + EPS) * gkv_ref[...]
    cosf = cos_ref[...]
    sinf = sin_ref[...]
    scale = (MLA_NOPE + MLA_ROPE) ** -0.5 * LOG2E
    cqb = _bf16(cqn)
    ckvb = _bf16(ckvn)
    k_rope = kr * cosf + krs * sinf
    for h in range(MLA_HEADS):
        head = slice(h * LANES, (h + 1) * LANES)
        rot = slice(MLA_QK_W + h * LANES, MLA_QK_W + (h + 1) * LANES)
        q_ref[:, head] = _bf16((_dot(cqb, wq_ref[:, head]) * cosf + _dot(cqb, wq_ref[:, rot]) * sinf) * scale)
        k_ref[:, head] = _bf16(_dot(ckvb, wkv_ref[:, head]) + k_rope)
    v_ref[...] = _bf16(_dot(ckvb, wkv_ref[:, MLA_QK_W:MLA_QK_W + MLA_V_W]))


def _chunk_mask(tq, tk):
    qc = lax.broadcasted_iota(jnp.int32, (tq, tk), 0) // CHUNK
    kc = lax.broadcasted_iota(jnp.int32, (tq, tk), 1) // CHUNK
    return kc <= qc


def _softmax_update(s, v, m_ref, l_ref, acc_ref, idx, shift=None):
    tk = s.shape[1]
    m_old = m_ref[idx]
    row_max = jnp.max(s, axis=1, keepdims=True)
    if shift is not None:
        row_max = row_max + shift
    m_new = jnp.maximum(m_old, row_max)
    alpha = jnp.exp2(m_old - m_new)
    m_sub = m_new if shift is None else m_new - shift
    p = jnp.exp2(s - jnp.concatenate([m_sub] * (tk // LANES), axis=1))
    psum = p[:, :LANES]
    for c in range(LANES, tk, LANES):
        psum = psum + p[:, c:c + LANES]
    l_ref[idx] = alpha * l_ref[idx] + psum
    acc_ref[idx] = alpha * acc_ref[idx] + _dot(_bf16(p), v)
    m_ref[idx] = m_new


def _softmax_init(m_ref, l_ref, acc_ref):
    m_ref[...] = jnp.full(m_ref.shape, NEG_INF, jnp.float32)
    l_ref[...] = jnp.zeros(l_ref.shape, jnp.float32)
    acc_ref[...] = jnp.zeros(acc_ref.shape, jnp.float32)


def _softmax_result(l_ref, acc_ref, idx):
    return acc_ref[idx] / jnp.sum(l_ref[idx], axis=1, keepdims=True)


def _mla_attn_kernel(q_ref, k_ref, v_ref, o_ref, m_ref, l_ref, acc_ref, *, t):
    qi = pl.program_id(1)
    _softmax_init(m_ref, l_ref, acc_ref)

    def tile(j, masked, width=1):
        ks = pl.multiple_of(j * t, t)
        for hh in range(MLA_HEADS):
            q = q_ref[:, hh * LANES:(hh + 1) * LANES]
            k = k_ref[pl.ds(ks, width * t), hh * LANES:(hh + 1) * LANES]
            v = v_ref[pl.ds(ks, width * t), (hh // 2) * LANES:(hh // 2 + 1) * LANES]
            s = lax.dot_general(q, k, _NT, preferred_element_type=jnp.float32)
            if masked:
                s = jnp.where(_chunk_mask(t, t), s, NEG_INF)
            _softmax_update(s, v, m_ref, l_ref, acc_ref, hh)

    done = 0
    for width in FAR_TILE_WIDTHS + (1,):
        trips = (qi - done) // width

        def body(i, carry, width=width, done=done):
            tile(done + i * width, False, width=width)
            return carry

        lax.fori_loop(0, trips, body, 0)
        done = done + trips * width

    tile(qi, True)
    lane = lax.broadcasted_iota(jnp.int32, (t, LANES), 1)
    for pair in range(MLA_HEADS // 2):
        o_ref[:, pair * LANES:(pair + 1) * LANES] = _bf16(
            jnp.where(lane < MLA_V, _softmax_result(l_ref, acc_ref, 2 * pair),
                      _softmax_result(l_ref, acc_ref, 2 * pair + 1)))


def _mla_attn(q, k, v, B, S, t):
    nq = S // t
    wq, wv = MLA_QK_W, MLA_V_W
    return pl.pallas_call(
        functools.partial(_mla_attn_kernel, t=t),
        grid=(B, nq),
        in_specs=[pl.BlockSpec((t, wq), lambda b, i: (b * nq + i, 0)),
                  pl.BlockSpec((S, wq), lambda b, i: (b, 0)),
                  pl.BlockSpec((S, wv), lambda b, i: (b, 0))],
        out_specs=pl.BlockSpec((t, wv), lambda b, i: (b * nq + i, 0)),
        out_shape=jax.ShapeDtypeStruct((B * S, wv), jnp.bfloat16),
        scratch_shapes=[pltpu.VMEM((MLA_HEADS, t, LANES), jnp.float32)] * 3,
        compiler_params=_cparams(("parallel", "arbitrary")),
        name="mla_attn",
    )(q, k, v)


def _t5_bias(rel, rb_ref, h):
    nb = REL_BUCKETS // 2
    max_exact = nb // 2
    n = jnp.abs(rel)
    nf = jnp.maximum(n, 1).astype(jnp.float32)
    large = max_exact + (jnp.log(nf / max_exact) / math.log(REL_MAX_DIST / max_exact)
                         * (nb - max_exact)).astype(jnp.int32)
    large = jnp.minimum(large, nb - 1)
    low = jnp.where(n < max_exact, n, large)
    neg = jnp.zeros(rel.shape, jnp.float32)
    pos = jnp.zeros(rel.shape, jnp.float32)
    for j in range(nb):
        eq = low == j
        neg = jnp.where(eq, rb_ref[h * REL_BUCKETS + j], neg)
        pos = jnp.where(eq, rb_ref[h * REL_BUCKETS + nb + j], pos)
    return jnp.where(rel > 0, pos, neg)


def _diff_attn_kernel(qmin_ref, kmax_ref, first_ref, consec_ref,
                      rb_ref, q_ref, k_ref, v_ref, posq_ref, posk_ref,
                      lq1_ref, lk1_ref, lq2_ref, lk2_ref, sub_ref,
                      o_ref, m_ref, l_ref, acc_ref, bias_ref, cache_ref, *, t, nq, lam_init):
    b = pl.program_id(0)
    hp = pl.program_id(1)
    qi = pl.program_id(2)
    _softmax_init(m_ref, l_ref, acc_ref)

    @pl.when(qi == 0)
    def _():
        for n in range(4):
            cache_ref[n] = 0

    lane = lax.broadcasted_iota(jnp.int32, (t, LANES), 1)
    qs = []
    for hh in range(DIFF_HEADS_PER_STEP):
        q = q_ref[:, hh * LANES:(hh + 1) * LANES]
        zero = jnp.zeros_like(q)
        qs.append(jnp.concatenate([jnp.where(lane < DIFF_QK, q, zero), jnp.where(lane < DIFF_QK, zero, q)], axis=0))
    tq = b * nq + qi
    qmin = qmin_ref[tq]

    def toeplitz_bias(j, h):
        d0 = first_ref[b * nq + j] - first_ref[tq]
        x = lax.broadcasted_iota(jnp.int32, (1, 2 * t), 1)
        g = _t5_bias(d0 + jnp.where(x < t, x, x - 2 * t), rb_ref, h)
        g = pltpu.roll(jnp.broadcast_to(g, (t, 2 * t)), 0, 1, stride=1, stride_axis=0)
        return g[:, :t]

    def general_bias(j, h):
        rel = posk_ref[j] - posq_ref[...]
        return _t5_bias(rel, rb_ref, h)

    def masked_bias(bias, masked):
        bias = bias * LOG2E
        return jnp.where(_chunk_mask(t, t), bias, NEG_INF) if masked else bias

    def tile(j, masked, bias_fn, width=1):
        ks = pl.multiple_of(j * t, t)
        for hh in range(DIFF_HEADS_PER_STEP):
            h = hp * DIFF_HEADS_PER_STEP + hh
            k = k_ref[pl.ds(ks, width * t), hh * LANES:(hh + 1) * LANES]
            v = v_ref[pl.ds(ks, width * t), hh * LANES:(hh + 1) * LANES]
            s = lax.dot_general(qs[hh], k, _NT, preferred_element_type=jnp.float32)
            if bias_fn is None:
                shift = rb_ref[h * REL_BUCKETS + REL_BUCKETS // 2 - 1] * LOG2E
            else:
                shift = None
                s = (s.reshape(2, t, t) + bias_fn(j, h, hh)[None]).reshape(2 * t, t)
            _softmax_update(s, v, m_ref, l_ref, acc_ref, hh, shift)

    def near_tile(j, masked):
        consecutive = jnp.logical_and(consec_ref[tq] == 1, consec_ref[b * nq + j] == 1)
        slot = 1 if masked else 0

        @pl.when(consecutive)
        def _():
            d0 = first_ref[b * nq + j] - first_ref[tq]
            stale = jnp.logical_or(cache_ref[2 * slot + 1] != 1, cache_ref[2 * slot] != d0)

            @pl.when(stale)
            def _():
                for hh in range(DIFF_HEADS_PER_STEP):
                    h = hp * DIFF_HEADS_PER_STEP + hh
                    bias_ref[slot, hh] = masked_bias(toeplitz_bias(j, h), masked)
                cache_ref[2 * slot] = d0
                cache_ref[2 * slot + 1] = 1

            tile(j, masked, lambda j, h, hh: bias_ref[slot, hh])

        @pl.when(jnp.logical_not(consecutive))
        def _():
            tile(j, masked, lambda j, h, hh: masked_bias(general_bias(j, h), masked))

    def is_far(j):
        return kmax_ref[b * nq + jnp.minimum(j, qi)] - qmin <= -T5_FAR

    def wide_far_loop(start, width):
        def cond(j):
            ok = j + width - 1 < qi
            for d in range(width):
                ok = jnp.logical_and(ok, is_far(j + d))
            return ok

        def step(j):
            tile(j, False, None, width=width)
            return j + width

        return lax.while_loop(cond, step, start)

    def body(j, carry):
        far = is_far(j)

        @pl.when(far)
        def _():
            tile(j, False, None)

        @pl.when(jnp.logical_not(far))
        def _():
            near_tile(j, False)

        return carry

    done = 0
    for width in FAR_TILE_WIDTHS:
        done = wide_far_loop(done, width)
    lax.fori_loop(done, qi, body, 0)
    near_tile(qi, True)

    f32 = jnp.float32
    lam = (jnp.exp(jnp.sum(lq1_ref[...].astype(f32) * lk1_ref[...].astype(f32), keepdims=True))
           - jnp.exp(jnp.sum(lq2_ref[...].astype(f32) * lk2_ref[...].astype(f32), keepdims=True))
           + lam_init)
    for hh in range(DIFF_HEADS_PER_STEP):
        a = _softmax_result(l_ref, acc_ref, hh)
        o = a[:t] - lam * a[t:]
        o = o * lax.rsqrt(jnp.mean(o * o, -1, keepdims=True) + EPS) * sub_ref[...] * (1.0 - lam_init)
        o_ref[:, hh * LANES:(hh + 1) * LANES] = _bf16(o)


def _diff_attn(ub, pos_col, pos_tiles, qmin, kmax, first, consec, rb_flat, lq1, lk1, lq2, lk2, subln,
               B, S, lam_init, t):
    nq = S // t
    per = DIFF_HEADS_PER_STEP
    G = DIFF_HEADS // per
    wg = per * DIFF_V
    small = lambda a: pl.BlockSpec(a.shape, lambda b, h, i, *_: (0, 0))
    grid_spec = pltpu.PrefetchScalarGridSpec(
        num_scalar_prefetch=4,
        grid=(B, G, nq),
        in_specs=[pl.BlockSpec(memory_space=pltpu.SMEM),
                  pl.BlockSpec((t, wg), lambda b, h, i, *_: (b * nq + i, h)),
                  pl.BlockSpec((S, wg), lambda b, h, i, *_: (b, G + h)),
                  pl.BlockSpec((S, wg), lambda b, h, i, *_: (b, 2 * G + h)),
                  pl.BlockSpec((t, 1), lambda b, h, i, *_: (b * nq + i, 0)),
                  pl.BlockSpec((nq, 1, t), lambda b, h, i, *_: (b, 0, 0)),
                  small(lq1), small(lk1), small(lq2), small(lk2), small(subln)],
        out_specs=pl.BlockSpec((t, wg), lambda b, h, i, *_: (b * nq + i, h)),
        scratch_shapes=[pltpu.VMEM((per, 2 * t, LANES), jnp.float32)] * 3
                       + [pltpu.VMEM((2, per, t, t), jnp.float32), pltpu.SMEM((4,), jnp.int32)],
    )
    return pl.pallas_call(
        functools.partial(_diff_attn_kernel, t=t, nq=nq, lam_init=lam_init),
        grid_spec=grid_spec,
        out_shape=jax.ShapeDtypeStruct((B * S, DIFF_HEADS * DIFF_V), jnp.bfloat16),
        compiler_params=_cparams(("parallel", "parallel", "arbitrary")),
        name="diff_attn",
    )(qmin, kmax, first, consec, rb_flat, ub, ub, ub, pos_col, pos_tiles, lq1, lk1, lq2, lk2, subln)


def _sb_attn_kernel(q_ref, k_ref, v_ref, o_ref, c_ref, acc_ref, *, t):
    qi = pl.program_id(1)
    n_pairs = SB_HEADS // 2
    row = lax.broadcasted_iota(jnp.int32, (t, t), 0)
    col = lax.broadcasted_iota(jnp.int32, (t, t), 1)
    tri = jnp.where(row > col, 1.0, 0.0).astype(jnp.bfloat16)
    strict = (col < row)[None]
    lane = lax.broadcasted_iota(jnp.int32, (t, LANES), 1)
    qs = []
    for hp in range(n_pairs):
        q = q_ref[:, hp * LANES:(hp + 1) * LANES]
        zero = jnp.zeros_like(q)
        qs.append(jnp.concatenate([jnp.where(lane < SB_D, q, zero), jnp.where(lane < SB_D, zero, q)], axis=0))
    c_ref[...] = jnp.zeros(c_ref.shape, jnp.float32)
    acc_ref[...] = jnp.zeros(acc_ref.shape, jnp.float32)

    def tile(j, diag):
        ks = pl.multiple_of(j * t, t)
        for hp in range(n_pairs):
            k = k_ref[pl.ds(ks, t), hp * LANES:(hp + 1) * LANES]
            v = v_ref[pl.ds(ks, t), hp * LANES:(hp + 1) * LANES]
            z = lax.dot_general(qs[hp], k, _NT, preferred_element_type=jnp.float32)
            lf = -(jnp.maximum(z, 0.0) + jnp.log(1.0 + jnp.exp(-jnp.abs(z))))
            if diag:
                lf = jnp.where(strict, lf.reshape(2, t, t), 0.0).reshape(2 * t, t)
            hi = _bf16(lf)
            lo = _bf16(lf - hi.astype(jnp.float32))
            c = c_ref[hp]
            later = _dot(hi, tri) + _dot(lo, tri) + jnp.concatenate([c] * (t // LANES), axis=1)
            w = jnp.exp(lf + z + later)
            if diag:
                w = jnp.where(strict, w.reshape(2, t, t), 0.0).reshape(2 * t, t)
            acc_ref[hp] += _dot(_bf16(w), v)
            c_ref[hp] = c + jnp.sum(lf, axis=1, keepdims=True)

    tile(qi, True)

    def cond(carry):
        j, cmax = carry
        return jnp.logical_and(j >= 0, cmax > SB_LOG_ZERO)

    def body(carry):
        j, _ = carry
        tile(j, False)
        return j - 1, jnp.max(c_ref[...])

    lax.while_loop(cond, body, (qi - 1, jnp.max(c_ref[...])))
    for hp in range(n_pairs):
        acc = acc_ref[hp]
        o_ref[:, hp * LANES:(hp + 1) * LANES] = _bf16(jnp.where(lane < SB_D, acc[:t], acc[t:]))


def _sb_attn(ub, B, S, t):
    nq = S // t
    w = SB_HEADS * SB_D
    c0 = DIFF_IN // w
    return pl.pallas_call(
        functools.partial(_sb_attn_kernel, t=t),
        grid=(B, nq),
        in_specs=[pl.BlockSpec((t, w), lambda b, i: (b * nq + i, c0)),
                  pl.BlockSpec((S, w), lambda b, i: (b, c0 + 1)),
                  pl.BlockSpec((S, w), lambda b, i: (b, c0 + 2))],
        out_specs=pl.BlockSpec((t, w), lambda b, i: (b * nq + i, 0)),
        out_shape=jax.ShapeDtypeStruct((B * S, w), jnp.bfloat16),
        scratch_shapes=[pltpu.VMEM((SB_HEADS // 2, 2 * t, LANES), jnp.float32)] * 2,
        compiler_params=_cparams(("parallel", "arbitrary")),
        name="sb_attn",
    )(ub, ub, ub)


def _layer_norm(x, g, b):
    mu = jnp.mean(x, -1, keepdims=True)
    xc = x - mu
    var = jnp.mean(xc * xc, -1, keepdims=True)
    return xc * lax.rsqrt(var + EPS) * g + b


def _out_proj_kernel(h_ref, ym_ref, yd_ref, ys_ref, wm_ref, wd_ref, ws_ref, g_ref, b_ref, o_ref, *, alpha):
    mix = _dot(ym_ref[...], wm_ref[...]) + _dot(yd_ref[...], wd_ref[...]) + _dot(ys_ref[...], ws_ref[...])
    o_ref[...] = _layer_norm(alpha * h_ref[...] + mix, g_ref[...], b_ref[...])


def _out_proj(h2, ym, yd, ys, wm, wd, ws, g, b, alpha, tm=TOKEN_TILE):
    T = h2.shape[0]
    row = lambda a: pl.BlockSpec((tm, a.shape[1]), lambda i: (i, 0))
    full = lambda a: pl.BlockSpec(a.shape, lambda i: (0, 0))
    return pl.pallas_call(
        functools.partial(_out_proj_kernel, alpha=alpha),
        grid=(T // tm,),
        in_specs=[row(h2), row(ym), row(yd), row(ys), full(wm), full(wd), full(ws), full(g), full(b)],
        out_specs=pl.BlockSpec((tm, D_MODEL), lambda i: (i, 0)),
        out_shape=jax.ShapeDtypeStruct((T, D_MODEL), jnp.float32),
        compiler_params=_cparams(("parallel",)),
        name="out_proj_ln",
    )(h2, ym, yd, ys, wm, wd, ws, g, b)


def _first_max(vals):
    m = vals[0]
    for v in vals[1:]:
        m = jnp.maximum(m, v)
    taken = jnp.zeros(m.shape, jnp.bool_)
    hot = []
    for v in vals:
        is_first = jnp.logical_and(v == m, jnp.logical_not(taken))
        hot.append(is_first)
        taken = jnp.logical_or(taken, is_first)
    return m, hot


def _router_kernel(h_ref, rw_ref, rb_ref, gate_ref, slot_ref, dest_ref, chosen_ref, first_ref):
    h = h_ref[...]
    rw = rw_ref[...]
    h_hi, rw_hi = _bf16(h), _bf16(rw)
    h_lo, rw_lo = _bf16(h - h_hi.astype(jnp.float32)), _bf16(rw - rw_hi.astype(jnp.float32))
    logits = (_dot(h_hi, rw_hi) + (_dot(h_hi, rw_lo) + _dot(h_lo, rw_hi))).T[:N_EXPERTS]
    scores = jax.nn.sigmoid(logits)
    sel = scores + rb_ref[...]
    ninf = -jnp.inf
    group_score, first, second = [], [], []
    for g in range(N_GROUPS):
        vals = [sel[g * EXPERTS_PER_GROUP + k:g * EXPERTS_PER_GROUP + k + 1, :] for k in range(EXPERTS_PER_GROUP)]
        m1, hot1 = _first_max(vals)
        m2, hot2 = _first_max([jnp.where(hh, ninf, v) for hh, v in zip(hot1, vals)])
        group_score.append(m1 + m2)
        first.append(hot1)
        second.append(hot2)
    _, best = _first_max(group_score)
    w1 = jnp.zeros_like(group_score[0])
    w2 = jnp.zeros_like(group_score[0])
    for g in range(N_GROUPS):
        for k in range(EXPERTS_PER_GROUP):
            e = g * EXPERTS_PER_GROUP + k
            sc = scores[e:e + 1, :]
            w1 = jnp.where(jnp.logical_and(best[g], first[g][k]), sc, w1)
            w2 = jnp.where(jnp.logical_and(best[g], second[g][k]), sc, w2)
    tot = w1 + w2
    for g in range(N_GROUPS):
        for k in range(EXPERTS_PER_GROUP):
            e = g * EXPERTS_PER_GROUP + k
            is1 = jnp.logical_and(best[g], first[g][k])
            is2 = jnp.logical_and(best[g], second[g][k])
            gate_ref[e:e + 1, :] = jnp.where(is1, w1 / tot, 0.0) + jnp.where(is2, w2 / tot, 0.0)
            chosen_ref[e:e + 1, :] = jnp.where(jnp.logical_or(is1, is2), 1.0, 0.0)
            first_ref[e:e + 1, :] = jnp.where(is1, 1.0, 0.0)
    chosen = chosen_ref[...]
    tm = chosen.shape[1]
    before = (lax.broadcasted_iota(jnp.int32, (tm, tm), 0) < lax.broadcasted_iota(jnp.int32, (tm, tm), 1))
    rank = _dot(_bf16(chosen), jnp.where(before, 1.0, 0.0).astype(jnp.bfloat16)).astype(jnp.int32)
    slot_ref[...] = jnp.where(chosen > 0.0, rank, -1)
    expert = lax.broadcasted_iota(jnp.int32, chosen.shape, 0)
    parked = jnp.logical_and(chosen > 0.0, rank < MOE_SLOTS)
    code = jnp.where(parked, expert * MOE_SLOTS + rank + 1, 0)
    is_first = first_ref[...] > 0.0
    dest_ref[0:1, :] = jnp.sum(jnp.where(is_first, code, 0), axis=0, keepdims=True) - 1
    dest_ref[1:2, :] = jnp.sum(jnp.where(is_first, 0, code), axis=0, keepdims=True) - 1


def _router(h2, rw_pad, rb_col):
    T = h2.shape[0]
    tm = MOE_SUB
    tok = lambda rows: pl.BlockSpec((rows, tm), lambda i: (0, i))
    return pl.pallas_call(
        _router_kernel,
        grid=(T // tm,),
        in_specs=[pl.BlockSpec((tm, D_MODEL), lambda i: (i, 0)),
                  pl.BlockSpec(rw_pad.shape, lambda i: (0, 0)),
                  pl.BlockSpec(rb_col.shape, lambda i: (0, 0))],
        out_specs=[tok(N_EXPERTS), tok(N_EXPERTS), tok(2)],
        out_shape=[jax.ShapeDtypeStruct((N_EXPERTS, T), jnp.float32),
                   jax.ShapeDtypeStruct((N_EXPERTS, T), jnp.int32),
                   jax.ShapeDtypeStruct((2, T), jnp.int32)],
        scratch_shapes=[pltpu.VMEM((N_EXPERTS, tm), jnp.float32)] * 2,
        compiler_params=_cparams(("parallel",)),
        name="router",
    )(h2, rw_pad, rb_col)


def _moe_kernel(npass_ref, h_ref, slot_ref, slot_t_ref, dest_t_ref, gate_ref, p_ref, wg_ref, wu_ref, wd_ref,
                pg_ref, pp_ref, g_ref, b_ref, o_ref, over_ref, xb_ref, y_ref, *, alpha, n_sub):
    i = pl.program_id(0)
    step = pl.program_id(1)
    R, SUB = MOE_SLOTS, MOE_SUB

    @pl.when(step == 0)
    def _():
        xb_ref[...] = _bf16(h_ref[...])
        over_ref[...] = jnp.zeros(over_ref.shape, jnp.float32)

    def expert_pass(e, k, c):
        slot_row = slot_ref[pl.ds(e, 1), :]
        gate_row = gate_ref[pl.ds(e, 1), :]
        xs, gs = [], []
        for j in range(n_sub):
            sl = slot_row[:, j * SUB:(j + 1) * SUB] - c * R
            hit = lax.broadcasted_iota(jnp.int32, (R, SUB), 0) == sl
            onehot = jnp.where(hit, 1.0, 0.0).astype(jnp.bfloat16)
            xs.append(_bf16(_dot(onehot, xb_ref[j * SUB:(j + 1) * SUB, :])))
            gs.append(jnp.sum(jnp.where(hit, gate_row[:, j * SUB:(j + 1) * SUB], 0.0), axis=1, keepdims=True))
        xe = jnp.concatenate(xs, axis=0)
        hid = jax.nn.silu(_dot(xe, wg_ref[k])) * _dot(xe, wu_ref[k])
        y = _dot(_bf16(hid), wd_ref[k])
        return [_bf16(y[j * R:(j + 1) * R] * gs[j]) for j in range(n_sub)]

    for k in range(MOE_EXPERTS_PER_STEP):
        e = step * MOE_EXPERTS_PER_STEP + k
        ys = expert_pass(e, k, 0)
        for j in range(n_sub):
            y_ref[j, pl.ds(pl.multiple_of(e * R, 16), R), :] = ys[j]

    for k in range(MOE_EXPERTS_PER_STEP):
        e = step * MOE_EXPERTS_PER_STEP + k

        def overflow_pass(c, carry, e=e, k=k):
            ys = expert_pass(e, k, c)
            lane16 = lax.broadcasted_iota(jnp.int32, (SUB, N_EXPERTS), 1)
            for j in range(n_sub):
                rows = slice(j * SUB, (j + 1) * SUB)
                slot_col = jnp.sum(jnp.where(lane16 == e, slot_t_ref[rows, :], 0), axis=1, keepdims=True)
                hit_t = lax.broadcasted_iota(jnp.int32, (SUB, R), 1) == slot_col - c * R
                over_ref[rows, :] += _dot(jnp.where(hit_t, 1.0, 0.0).astype(jnp.bfloat16), ys[j])
            return carry

        lax.fori_loop(1, npass_ref[i * N_EXPERTS + e], overflow_pass, 0)

    @pl.when(step == N_EXPERTS // MOE_EXPERTS_PER_STEP - 1)
    def _():
        lane = lax.broadcasted_iota(jnp.int32, (SUB, N_EXPERTS * R), 1)
        for j in range(n_sub):
            rows = slice(j * SUB, (j + 1) * SUB)
            dest = dest_t_ref[rows, :]
            hit = jnp.logical_or(lane == dest[:, 0:1], lane == dest[:, 1:2])
            scatter = jnp.where(hit, 1.0, 0.0).astype(jnp.bfloat16)
            ffn = _dot(scatter, y_ref[j]) + over_ref[rows, :]
            h = h_ref[rows, :]
            ple = jax.nn.sigmoid(_dot(xb_ref[rows, :], pg_ref[...])) * _dot(_bf16(p_ref[rows, :]), pp_ref[...])
            o_ref[rows, :] = _layer_norm(alpha * h + ffn + ple, g_ref[...], b_ref[...])


def _moe(h2, slot, dest, gate, p2, wg, wu, wd, layer, pg, pp, g, b, alpha, tm=MOE_TILE):
    T = h2.shape[0]
    n_sub = tm // MOE_SUB
    n_tiles = T // tm
    per = MOE_EXPERTS_PER_STEP
    count = jnp.sum((slot >= 0).reshape(N_EXPERTS, n_tiles, n_sub, MOE_SUB), axis=3)
    npass = jnp.maximum(1, (jnp.max(count, axis=2) + MOE_SLOTS - 1) // MOE_SLOTS).T.reshape(-1).astype(jnp.int32)
    full = lambda a: pl.BlockSpec(a.shape, lambda i, e, *_: (0, 0))
    experts = lambda a: pl.BlockSpec((None, per) + a.shape[2:], lambda i, e, *_: (layer, e, 0, 0))
    grid_spec = pltpu.PrefetchScalarGridSpec(
        num_scalar_prefetch=1,
        grid=(n_tiles, N_EXPERTS // per),
        in_specs=[pl.BlockSpec((tm, D_MODEL), lambda i, e, *_: (i, 0)),
                  pl.BlockSpec((N_EXPERTS, tm), lambda i, e, *_: (0, i)),
                  pl.BlockSpec((tm, N_EXPERTS), lambda i, e, *_: (i, 0)),
                  pl.BlockSpec((tm, 2), lambda i, e, *_: (i, 0)),
                  pl.BlockSpec((N_EXPERTS, tm), lambda i, e, *_: (0, i)),
                  pl.BlockSpec((tm, PLE_DIM), lambda i, e, *_: (i, 0)),
                  experts(wg), experts(wu), experts(wd),
                  full(pg), full(pp), full(g), full(b)],
        out_specs=pl.BlockSpec((tm, D_MODEL), lambda i, e, *_: (i, 0)),
        scratch_shapes=[pltpu.VMEM((tm, D_MODEL), jnp.float32),
                        pltpu.VMEM((tm, D_MODEL), jnp.bfloat16),
                        pltpu.VMEM((n_sub, N_EXPERTS * MOE_SLOTS, D_MODEL), jnp.bfloat16)],
    )
    return pl.pallas_call(
        functools.partial(_moe_kernel, alpha=alpha, n_sub=n_sub),
        grid_spec=grid_spec,
        out_shape=jax.ShapeDtypeStruct((T, D_MODEL), jnp.float32),
        compiler_params=_cparams(("parallel", "arbitrary")),
        name="moe_ple_ln",
    )(npass, h2, slot, slot.T, dest.T, gate, p2, wg, wu, wd, pg, pp, g, b)


def _rotate_half_cols(w):
    half = w.shape[1] // 2
    return jnp.concatenate([-w[:, half:], w[:, :half]], axis=1)


def _in_proj_weights(w_in):
    z = lambda n: jnp.zeros((D_MODEL, n), w_in.dtype)
    cq = w_in[:, :MLA_Q_RANK]
    ckv = w_in[:, MLA_Q_RANK:MLA_Q_RANK + MLA_KV_RANK]
    kr = w_in[:, MLA_Q_RANK + MLA_KV_RANK:MLA_IN]
    n_dq = DIFF_HEADS * 2 * DIFF_QK
    sb0 = MLA_IN + DIFF_IN
    n_sq = SB_HEADS * SB_D
    cols = [ckv,
            z(MLA_NOPE), kr, z(LANES - MLA_NOPE - MLA_ROPE),
            z(MLA_NOPE), _rotate_half_cols(kr), z(LANES - MLA_NOPE - MLA_ROPE),
            cq, z(UA_CQ_W - MLA_Q_RANK),
            w_in[:, MLA_IN:MLA_IN + n_dq] * (DIFF_QK ** -0.5 * LOG2E),
            w_in[:, MLA_IN + n_dq:sb0],
            w_in[:, sb0:sb0 + n_sq] * (SB_D ** -0.5),
            w_in[:, sb0 + n_sq:]]
    return _bf16(jnp.concatenate(cols, axis=1))


def _mla_up_weights(w_uq, w_ukv):
    dq = MLA_NOPE + MLA_ROPE
    zq = lambda n: jnp.zeros((MLA_Q_RANK, n), w_uq.dtype)
    plain, rot = [], []
    for h in range(MLA_HEADS):
        wh = w_uq[:, h * dq:(h + 1) * dq]
        plain += [wh, zq(LANES - dq)]
        rot += [zq(MLA_NOPE), _rotate_half_cols(wh[:, MLA_NOPE:]), zq(LANES - dq)]
    wq = jnp.concatenate(plain + rot, axis=1)
    wq = jnp.concatenate([wq, jnp.zeros((UA_CQ_W - MLA_Q_RANK, wq.shape[1]), wq.dtype)], axis=0)
    dkv = MLA_NOPE + MLA_V
    zk = lambda n: jnp.zeros((MLA_KV_RANK, n), w_ukv.dtype)
    kcols, vcols = [], []
    for h in range(MLA_HEADS):
        wh = w_ukv[:, h * dkv:(h + 1) * dkv]
        kcols += [wh[:, :MLA_NOPE], zk(LANES - MLA_NOPE)]
        vcols += [wh[:, MLA_NOPE:]]
    wkv = jnp.concatenate(kcols + vcols, axis=1)
    return _bf16(wq), _bf16(wkv)


def _attn_tiles(S):
    pick = lambda want: max(c for c in (128, 256, 512, 1024) if c <= want and S % c == 0)
    return pick(512), pick(512), pick(256)


def kernel(x, p, positions, w_in, mla_q_norm, mla_w_uq, mla_kv_norm, mla_w_ukv, diff_lambda_q1, diff_lambda_k1, diff_lambda_q2, diff_lambda_k2, diff_subln, rel_bias, w_o, ln1_g, ln1_b, router_w, router_b, w_gate, w_up, w_down, ple_proj, ple_gate, ln2_g, ln2_b):
    B, S, _ = x.shape
    depth = w_in.shape[0]
    T = B * S
    alpha = (2 * depth) ** 0.25
    t_mla, t_diff, t_sb = _attn_tiles(S)
    nq = S // t_diff

    pos_col = positions.reshape(T, 1)
    pos_tiles = positions.reshape(B * nq, 1, t_diff)
    tile_pos = positions.reshape(B * nq, t_diff)
    qmin = jnp.min(tile_pos, axis=1)
    kmax = jnp.max(tile_pos, axis=1)
    first = tile_pos[:, 0]
    consec = jnp.all(tile_pos[:, 1:] - tile_pos[:, :-1] == 1, axis=1).astype(jnp.int32)
    rb_flat = rel_bias.T.reshape(-1).astype(jnp.float32)

    half = MLA_ROPE // 2
    inv = ROPE_THETA ** (-jnp.arange(half, dtype=jnp.float32) / half)
    cos_t, sin_t = _rope_table(positions.reshape(1, T), jnp.concatenate([inv, inv]).reshape(MLA_ROPE, 1))
    pad_l = lambda v: jnp.full((T, MLA_NOPE), v, jnp.float32)
    pad_r = jnp.zeros((T, LANES - MLA_NOPE - MLA_ROPE), jnp.float32)
    cosf = jnp.concatenate([pad_l(1.0), cos_t.T, pad_r], axis=1)
    sinf = jnp.concatenate([pad_l(0.0), sin_t.T, pad_r], axis=1)

    rw_pad = jnp.pad(router_w.astype(jnp.float32), ((0, 0), (0, LANES - N_EXPERTS)))
    rb_col = router_b.reshape(N_EXPERTS, 1).astype(jnp.float32)
    row = lambda a: a.reshape(1, -1)

    wg_all, wu_all, wd_all = _bf16(w_gate), _bf16(w_up), _bf16(w_down)

    h = x.reshape(T, D_MODEL)
    for i in range(depth):
        lam_init = 0.8 - 0.6 * math.exp(-0.3 * i)
        wq, wkv = _mla_up_weights(mla_w_uq[i], mla_w_ukv[i])
        gq = jnp.pad(mla_q_norm[i], (0, UA_CQ_W - MLA_Q_RANK)).reshape(1, UA_CQ_W)
        ub, q_m, k_m, v_m = _in_proj(h, _in_proj_weights(w_in[i]), cosf, sinf, gq, row(mla_kv_norm[i]), wq, wkv)
        y_mla = _mla_attn(q_m, k_m, v_m, B, S, t=t_mla)
        y_diff = _diff_attn(ub, pos_col, pos_tiles, qmin, kmax, first, consec, rb_flat,
                            row(diff_lambda_q1[i]), row(diff_lambda_k1[i]),
                            row(diff_lambda_q2[i]), row(diff_lambda_k2[i]), row(diff_subln[i]),
                            B, S, lam_init, t=t_diff)
        y_sb = _sb_attn(ub, B, S, t=t_sb)
        wo = _bf16(w_o[i])
        n_m, n_d = MLA_HEADS * MLA_V, DIFF_HEADS * DIFF_V
        h = _out_proj(h, y_mla, y_diff, y_sb, wo[:n_m], wo[n_m:n_m + n_d], wo[n_m + n_d:],
                      row(ln1_g[i]), row(ln1_b[i]), alpha)
        gate, slot, dest = _router(h, rw_pad, rb_col)
        h = _moe(h, slot, dest, gate, p[i].reshape(T, PLE_DIM), wg_all, wu_all, wd_all, i,
                 _bf16(ple_gate[i]), _bf16(ple_proj[i]), row(ln2_g[i]), row(ln2_b[i]), alpha)
    return h.reshape(B, S, D_MODEL)
```

```python
import functools
import math

import jax
import jax.numpy as jnp
from jax import lax
from jax.experimental import pallas as pl
from jax.experimental.pallas import tpu as pltpu

D_MODEL = 1024
CHUNK = 64
PLE_DIM = 256
MLA_HEADS = 4
MLA_NOPE = 64
MLA_ROPE = 32
MLA_V = 64
MLA_Q_RANK = 192
MLA_KV_RANK = 128
ROPE_THETA = 10000.0
DIFF_HEADS = 4
DIFF_QK = 64
DIFF_V = 2 * DIFF_QK
SB_HEADS = 4
SB_D = 64
REL_BUCKETS = 32
REL_MAX_DIST = 128
N_EXPERTS = 16
N_GROUPS = 4
EXPERTS_PER_GROUP = N_EXPERTS // N_GROUPS
D_EXPERT = 512
MLA_IN = MLA_Q_RANK + MLA_KV_RANK + MLA_ROPE
DIFF_IN = 2 * DIFF_HEADS * 2 * DIFF_QK + DIFF_HEADS * DIFF_V
SB_IN = 3 * SB_HEADS * SB_D
EPS = 1e-5
NEG_INF = -1e30

LANES = 128
VMEM_LIMIT = 56 * 1024 * 1024

UA_CKV = 0
UA_KR = UA_CKV + MLA_KV_RANK
UA_KRS = UA_KR + LANES
UA_CQ = UA_KRS + LANES
UA_CQ_W = 2 * LANES
UA_W = UA_CQ + UA_CQ_W
UB_W = DIFF_IN + SB_IN
MLA_QK_W = MLA_HEADS * LANES
MLA_V_W = MLA_HEADS * MLA_V

T5_FAR = 91
SB_LOG_ZERO = -88.0
TOKEN_TILE = 512
PROJ_COLS = 512
MOE_TILE = 1024
DIFF_HEADS_PER_STEP = 2
FAR_TILE_WIDTHS = (4, 2)
MOE_SUB = 512
MOE_SLOTS = 80
MOE_EXPERTS_PER_STEP = 2

_NT = (((1,), (1,)), ((), ()))
LOG2E = math.log2(math.e)


def _cparams(sem):
    return pltpu.CompilerParams(dimension_semantics=sem, vmem_limit_bytes=VMEM_LIMIT)


def _bf16(a):
    return a.astype(jnp.bfloat16)


def _dot(a, b):
    return jnp.dot(a, b, preferred_element_type=jnp.float32)


def _in_proj_kernel(x_ref, w_ref, cos_ref, sin_ref, gq_ref, gkv_ref, wq_ref, wkv_ref,
                    ub_ref, q_ref, k_ref, v_ref, ua_ref):
    x = _bf16(x_ref[...])
    step = PROJ_COLS
    for c in range(0, UA_W, step):
        e = min(c + step, UA_W)
        ua_ref[:, c:e] = _dot(x, w_ref[:, c:e])
    for c in range(0, UB_W, step):
        e = min(c + step, UB_W)
        ub_ref[:, c:e] = _bf16(_dot(x, w_ref[:, UA_W + c:UA_W + e]))
    _mla_prep_kernel(ua_ref, cos_ref, sin_ref, gq_ref, gkv_ref, wq_ref, wkv_ref, q_ref, k_ref, v_ref)


def _in_proj(h2, w_all, cosf, sinf, gq, gkv, wq, wkv, tm=TOKEN_TILE):
    T = h2.shape[0]
    full = lambda a: pl.BlockSpec(a.shape, lambda i: (0, 0))
    row = lambda w: pl.BlockSpec((tm, w), lambda i: (i, 0))
    return pl.pallas_call(
        _in_proj_kernel,
        grid=(T // tm,),
        in_specs=[row(D_MODEL), full(w_all), row(LANES), row(LANES), full(gq), full(gkv), full(wq), full(wkv)],
        out_specs=[row(UB_W), row(MLA_QK_W), row(MLA_QK_W), row(MLA_V_W)],
        out_shape=[jax.ShapeDtypeStruct((T, UB_W), jnp.bfloat16),
                   jax.ShapeDtypeStruct((T, MLA_QK_W), jnp.bfloat16),
                   jax.ShapeDtypeStruct((T, MLA_QK_W), jnp.bfloat16),
                   jax.ShapeDtypeStruct((T, MLA_V_W), jnp.bfloat16)],
        scratch_shapes=[pltpu.VMEM((tm, UA_W), jnp.float32)],
        compiler_params=_cparams(("parallel",)),
        name="in_proj_mla_prep",
    )(h2, w_all, cosf, sinf, gq, gkv, wq, wkv)


def _rope_table_kernel(pos_ref, inv_ref, cos_ref, sin_ref):
    ang = pos_ref[...].astype(jnp.float32) * inv_ref[...]
    cos_ref[...] = jnp.cos(ang)
    sin_ref[...] = jnp.sin(ang)


def _rope_table(pos_row, inv_col, tn=2048):
    T = pos_row.shape[1]
    n = inv_col.shape[0]
    tn = min(tn, T)
    return pl.pallas_call(
        _rope_table_kernel,
        grid=(T // tn,),
        in_specs=[pl.BlockSpec((1, tn), lambda i: (0, i)),
                  pl.BlockSpec((n, 1), lambda i: (0, 0))],
        out_specs=[pl.BlockSpec((n, tn), lambda i: (0, i))] * 2,
        out_shape=[jax.ShapeDtypeStruct((n, T), jnp.float32)] * 2,
        compiler_params=_cparams(("parallel",)),
        name="rope_table",
    )(pos_row, inv_col)


def _mla_prep_kernel(ua_ref, cos_ref, sin_ref, gq_ref, gkv_ref, wq_ref, wkv_ref,
                     q_ref, k_ref, v_ref):
    ckv = ua_ref[:, UA_CKV:UA_CKV + MLA_KV_RANK]
    kr = ua_ref[:, UA_KR:UA_KR + LANES]
    krs = ua_ref[:, UA_KRS:UA_KRS + LANES]
    cq = ua_ref[:, UA_CQ:UA_CQ + UA_CQ_W]
    cqn = cq * lax.rsqrt(jnp.sum(cq * cq, -1, keepdims=True) * (1.0 / MLA_Q_RANK) + EPS) * gq_ref[...]
    ckvn = ckv * lax.rsqrt(jnp.sum(ckv * ckv, -1, keepdims=True) * (1.0 / MLA_KV_RANK) + EPS) * gkv_ref[...]
    cosf = cos_ref[...]
    sinf = sin_ref[...]
    scale = (MLA_NOPE + MLA_ROPE) ** -0.5 * LOG2E
    cqb = _bf16(cqn)
    ckvb = _bf16(ckvn)
    k_rope = kr * cosf + krs * sinf
    for h in range(MLA_HEADS):
        head = slice(h * LANES, (h + 1) * LANES)
        rot = slice(MLA_QK_W + h * LANES, MLA_QK_W + (h + 1) * LANES)
        q_ref[:, head] = _bf16((_dot(cqb, wq_ref[:, head]) * cosf + _dot(cqb, wq_ref[:, rot]) * sinf) * scale)
        k_ref[:, head] = _bf16(_dot(ckvb, wkv_ref[:, head]) + k_rope)
    v_ref[...] = _bf16(_dot(ckvb, wkv_ref[:, MLA_QK_W:MLA_QK_W + MLA_V_W]))


def _chunk_mask(tq, tk):
    qc = lax.broadcasted_iota(jnp.int32, (tq, tk), 0) // CHUNK
    kc = lax.broadcasted_iota(jnp.int32, (tq, tk), 1) // CHUNK
    return kc <= qc


def _softmax_update(s, v, m_ref, l_ref, acc_ref, idx, shift=None):
    tk = s.shape[1]
    m_old = m_ref[idx]
    row_max = jnp.max(s, axis=1, keepdims=True)
    if shift is not None:
        row_max = row_max + shift
    m_new = jnp.maximum(m_old, row_max)
    alpha = jnp.exp2(m_old - m_new)
    m_sub = m_new if shift is None else m_new - shift
    p = jnp.exp2(s - jnp.concatenate([m_sub] * (tk // LANES), axis=1))
    psum = p[:, :LANES]
    for c in range(LANES, tk, LANES):
        psum = psum + p[:, c:c + LANES]
    l_ref[idx] = alpha * l_ref[idx] + psum
    acc_ref[idx] = alpha * acc_ref[idx] + _dot(_bf16(p), v)
    m_ref[idx] = m_new


def _softmax_init(m_ref, l_ref, acc_ref):
    m_ref[...] = jnp.full(m_ref.shape, NEG_INF, jnp.float32)
    l_ref[...] = jnp.zeros(l_ref.shape, jnp.float32)
    acc_ref[...] = jnp.zeros(acc_ref.shape, jnp.float32)


def _softmax_result(l_ref, acc_ref, idx):
    return acc_ref[idx] / jnp.sum(l_ref[idx], axis=1, keepdims=True)


def _mla_attn_kernel(q_ref, k_ref, v_ref, o_ref, m_ref, l_ref, acc_ref, *, t):
    qi = pl.program_id(1)
    _softmax_init(m_ref, l_ref, acc_ref)

    def tile(j, masked, width=1):
        ks = pl.multiple_of(j * t, t)
        for hh in range(MLA_HEADS):
            q = q_ref[:, hh * LANES:(hh + 1) * LANES]
            k = k_ref[pl.ds(ks, width * t), hh * LANES:(hh + 1) * LANES]
            v = v_ref[pl.ds(ks, width * t), (hh // 2) * LANES:(hh // 2 + 1) * LANES]
            s = lax.dot_general(q, k, _NT, preferred_element_type=jnp.float32)
            if masked:
                s = jnp.where(_chunk_mask(t, t), s, NEG_INF)
            _softmax_update(s, v, m_ref, l_ref, acc_ref, hh)

    done = 0
    for width in FAR_TILE_WIDTHS + (1,):
        trips = (qi - done) // width

        def body(i, carry, width=width, done=done):
            tile(done + i * width, False, width=width)
            return carry

        lax.fori_loop(0, trips, body, 0)
        done = done + trips * width

    tile(qi, True)
    lane = lax.broadcasted_iota(jnp.int32, (t, LANES), 1)
    for pair in range(MLA_HEADS // 2):
        o_ref[:, pair * LANES:(pair + 1) * LANES] = _bf16(
            jnp.where(lane < MLA_V, _softmax_result(l_ref, acc_ref, 2 * pair),
                      _softmax_result(l_ref, acc_ref, 2 * pair + 1)))


def _mla_attn(q, k, v, B, S, t):
    nq = S // t
    wq, wv = MLA_QK_W, MLA_V_W
    return pl.pallas_call(
        functools.partial(_mla_attn_kernel, t=t),
        grid=(B, nq),
        in_specs=[pl.BlockSpec((t, wq), lambda b, i: (b * nq + i, 0)),
                  pl.BlockSpec((S, wq), lambda b, i: (b, 0)),
                  pl.BlockSpec((S, wv), lambda b, i: (b, 0))],
        out_specs=pl.BlockSpec((t, wv), lambda b, i: (b * nq + i, 0)),
        out_shape=jax.ShapeDtypeStruct((B * S, wv), jnp.bfloat16),
        scratch_shapes=[pltpu.VMEM((MLA_HEADS, t, LANES), jnp.float32)] * 3,
        compiler_params=_cparams(("parallel", "arbitrary")),
        name="mla_attn",
    )(q, k, v)


def _t5_bias(rel, rb_ref, h):
    nb = REL_BUCKETS // 2
    max_exact = nb // 2
    n = jnp.abs(rel)
    nf = jnp.maximum(n, 1).astype(jnp.float32)
    large = max_exact + (jnp.log(nf / max_exact) / math.log(REL_MAX_DIST / max_exact)
                         * (nb - max_exact)).astype(jnp.int32)
    large = jnp.minimum(large, nb - 1)
    low = jnp.where(n < max_exact, n, large)
    neg = jnp.zeros(rel.shape, jnp.float32)
    pos = jnp.zeros(rel.shape, jnp.float32)
    for j in range(nb):
        eq = low == j
        neg = jnp.where(eq, rb_ref[h * REL_BUCKETS + j], neg)
        pos = jnp.where(eq, rb_ref[h * REL_BUCKETS + nb + j], pos)
    return jnp.where(rel > 0, pos, neg)


def _diff_attn_kernel(qmin_ref, kmax_ref, first_ref, consec_ref,
                      rb_ref, q_ref, k_ref, v_ref, posq_ref, posk_ref,
                      lq1_ref, lk1_ref, lq2_ref, lk2_ref, sub_ref,
                      o_ref, m_ref, l_ref, acc_ref, bias_ref, cache_ref, *, t, nq, lam_init):
    b = pl.program_id(0)
    hp = pl.program_id(1)
    qi = pl.program_id(2)
    _softmax_init(m_ref, l_ref, acc_ref)

    @pl.when(qi == 0)
    def _():
        for n in range(4):
            cache_ref[n] = 0

    lane = lax.broadcasted_iota(jnp.int32, (t, LANES), 1)
    qs = []
    for hh in range(DIFF_HEADS_PER_STEP):
        q = q_ref[:, hh * LANES:(hh + 1) * LANES]
        zero = jnp.zeros_like(q)
        qs.append(jnp.concatenate([jnp.where(lane < DIFF_QK, q, zero), jnp.where(lane < DIFF_QK, zero, q)], axis=0))
    tq = b * nq + qi
    qmin = qmin_ref[tq]

    def toeplitz_bias(j, h):
        d0 = first_ref[b * nq + j] - first_ref[tq]
        x = lax.broadcasted_iota(jnp.int32, (1, 2 * t), 1)
        g = _t5_bias(d0 + jnp.where(x < t, x, x - 2 * t), rb_ref, h)
        g = pltpu.roll(jnp.broadcast_to(g, (t, 2 * t)), 0, 1, stride=1, stride_axis=0)
        return g[:, :t]

    def general_bias(j, h):
        rel = posk_ref[j] - posq_ref[...]
        return _t5_bias(rel, rb_ref, h)

    def masked_bias(bias, masked):
        bias = bias * LOG2E
        return jnp.where(_chunk_mask(t, t), bias, NEG_INF) if masked else bias

    def tile(j, masked, bias_fn, width=1):
        ks = pl.multiple_of(j * t, t)
        for hh in range(DIFF_HEADS_PER_STEP):
            h = hp * DIFF_HEADS_PER_STEP + hh
            k = k_ref[pl.ds(ks, width * t), hh * LANES:(hh + 1) * LANES]
            v = v_ref[pl.ds(ks, width * t), hh * LANES:(hh + 1) * LANES]
            s = lax.dot_general(qs[hh], k, _NT, preferred_element_type=jnp.float32)
            if bias_fn is None:
                shift = rb_ref[h * REL_BUCKETS + REL_BUCKETS // 2 - 1] * LOG2E
            else:
                shift = None
                s = (s.reshape(2, t, t) + bias_fn(j, h, hh)[None]).reshape(2 * t, t)
            _softmax_update(s, v, m_ref, l_ref, acc_ref, hh, shift)

    def near_tile(j, masked):
        consecutive = jnp.logical_and(consec_ref[tq] == 1, consec_ref[b * nq + j] == 1)
        slot = 1 if masked else 0

        @pl.when(consecutive)
        def _():
            d0 = first_ref[b * nq + j] - first_ref[tq]
            stale = jnp.logical_or(cache_ref[2 * slot + 1] != 1, cache_ref[2 * slot] != d0)

            @pl.when(stale)
            def _():
                for hh in range(DIFF_HEADS_PER_STEP):
                    h = hp * DIFF_HEADS_PER_STEP + hh
                    bias_ref[slot, hh] = masked_bias(toeplitz_bias(j, h), masked)
                cache_ref[2 * slot] = d0
                cache_ref[2 * slot + 1] = 1

            tile(j, masked, lambda j, h, hh: bias_ref[slot, hh])

        @pl.when(jnp.logical_not(consecutive))
        def _():
            tile(j, masked, lambda j, h, hh: masked_bias(general_bias(j, h), masked))

    def is_far(j):
        return kmax_ref[b * nq + jnp.minimum(j, qi)] - qmin <= -T5_FAR

    def wide_far_loop(start, width):
        def cond(j):
            ok = j + width - 1 < qi
            for d in range(width):
                ok = jnp.logical_and(ok, is_far(j + d))
            return ok

        def step(j):
            tile(j, False, None, width=width)
            return j + width

        return lax.while_loop(cond, step, start)

    def body(j, carry):
        far = is_far(j)

        @pl.when(far)
        def _():
            tile(j, False, None)

        @pl.when(jnp.logical_not(far))
        def _():
            near_tile(j, False)

        return carry

    done = 0
    for width in FAR_TILE_WIDTHS:
        done = wide_far_loop(done, width)
    lax.fori_loop(done, qi, body, 0)
    near_tile(qi, True)

    f32 = jnp.float32
    lam = (jnp.exp(jnp.sum(lq1_ref[...].astype(f32) * lk1_ref[...].astype(f32), keepdims=True))
           - jnp.exp(jnp.sum(lq2_ref[...].astype(f32) * lk2_ref[...].astype(f32), keepdims=True))
           + lam_init)
    for hh in range(DIFF_HEADS_PER_STEP):
        a = _softmax_result(l_ref, acc_ref, hh)
        o = a[:t] - lam * a[t:]
        o = o * lax.rsqrt(jnp.mean(o * o, -1, keepdims=True) + EPS) * sub_ref[...] * (1.0 - lam_init)
        o_ref[:, hh * LANES:(hh + 1) * LANES] = _bf16(o)


def _diff_attn(ub, pos_col, pos_tiles, qmin, kmax, first, consec, rb_flat, lq1, lk1, lq2, lk2, subln,
               B, S, lam_init, t):
    nq = S // t
    per = DIFF_HEADS_PER_STEP
    G = DIFF_HEADS // per
    wg = per * DIFF_V
    small = lambda a: pl.BlockSpec(a.shape, lambda b, h, i, *_: (0, 0))
    grid_spec = pltpu.PrefetchScalarGridSpec(
        num_scalar_prefetch=4,
        grid=(B, G, nq),
        in_specs=[pl.BlockSpec(memory_space=pltpu.SMEM),
                  pl.BlockSpec((t, wg), lambda b, h, i, *_: (b * nq + i, h)),
                  pl.BlockSpec((S, wg), lambda b, h, i, *_: (b, G + h)),
                  pl.BlockSpec((S, wg), lambda b, h, i, *_: (b, 2 * G + h)),
                  pl.BlockSpec((t, 1), lambda b, h, i, *_: (b * nq + i, 0)),
                  pl.BlockSpec((nq, 1, t), lambda b, h, i, *_: (b, 0, 0)),
                  small(lq1), small(lk1), small(lq2), small(lk2), small(subln)],
        out_specs=pl.BlockSpec((t, wg), lambda b, h, i, *_: (b * nq + i, h)),
        scratch_shapes=[pltpu.VMEM((per, 2 * t, LANES), jnp.float32)] * 3
                       + [pltpu.VMEM((2, per, t, t), jnp.float32), pltpu.SMEM((4,), jnp.int32)],
    )
    return pl.pallas_call(
        functools.partial(_diff_attn_kernel, t=t, nq=nq, lam_init=lam_init),
        grid_spec=grid_spec,
        out_shape=jax.ShapeDtypeStruct((B * S, DIFF_HEADS * DIFF_V), jnp.bfloat16),
        compiler_params=_cparams(("parallel", "parallel", "arbitrary")),
        name="diff_attn",
    )(qmin, kmax, first, consec, rb_flat, ub, ub, ub, pos_col, pos_tiles, lq1, lk1, lq2, lk2, subln)


def _sb_attn_kernel(q_ref, k_ref, v_ref, o_ref, c_ref, acc_ref, *, t):
    qi = pl.program_id(1)
    n_pairs = SB_HEADS // 2
    row = lax.broadcasted_iota(jnp.int32, (t, t), 0)
    col = lax.broadcasted_iota(jnp.int32, (t, t), 1)
    tri = jnp.where(row > col, 1.0, 0.0).astype(jnp.bfloat16)
    strict = (col < row)[None]
    lane = lax.broadcasted_iota(jnp.int32, (t, LANES), 1)
    qs = []
    for hp in range(n_pairs):
        q = q_ref[:, hp * LANES:(hp + 1) * LANES]
        zero = jnp.zeros_like(q)
        qs.append(jnp.concatenate([jnp.where(lane < SB_D, q, zero), jnp.where(lane < SB_D, zero, q)], axis=0))
    c_ref[...] = jnp.zeros(c_ref.shape, jnp.float32)
    acc_ref[...] = jnp.zeros(acc_ref.shape, jnp.float32)

    def tile(j, diag):
        ks = pl.multiple_of(j * t, t)
        for hp in range(n_pairs):
            k = k_ref[pl.ds(ks, t), hp * LANES:(hp + 1) * LANES]
            v = v_ref[pl.ds(ks, t), hp * LANES:(hp + 1) * LANES]
            z = lax.dot_general(qs[hp], k, _NT, preferred_element_type=jnp.float32)
            lf = -(jnp.maximum(z, 0.0) + jnp.log(1.0 + jnp.exp(-jnp.abs(z))))
            if diag:
                lf = jnp.where(strict, lf.reshape(2, t, t), 0.0).reshape(2 * t, t)
            hi = _bf16(lf)
            lo = _bf16(lf - hi.astype(jnp.float32))
            c = c_ref[hp]
            later = _dot(hi, tri) + _dot(lo, tri) + jnp.concatenate([c] * (t // LANES), axis=1)
            w = jnp.exp(lf + z + later)
            if diag:
                w = jnp.where(strict, w.reshape(2, t, t), 0.0).reshape(2 * t, t)
            acc_ref[hp] += _dot(_bf16(w), v)
            c_ref[hp] = c + jnp.sum(lf, axis=1, keepdims=True)

    tile(qi, True)

    def cond(carry):
        j, cmax = carry
        return jnp.logical_and(j >= 0, cmax > SB_LOG_ZERO)

    def body(carry):
        j, _ = carry
        tile(j, False)
        return j - 1, jnp.max(c_ref[...])

    lax.while_loop(cond, body, (qi - 1, jnp.max(c_ref[...])))
    for hp in range(n_pairs):
        acc = acc_ref[hp]
        o_ref[:, hp * LANES:(hp + 1) * LANES] = _bf16(jnp.where(lane < SB_D, acc[:t], acc[t:]))


def _sb_attn(ub, B, S, t):
    nq = S // t
    w = SB_HEADS * SB_D
    c0 = DIFF_IN // w
    return pl.pallas_call(
        functools.partial(_sb_attn_kernel, t=t),
        grid=(B, nq),
        in_specs=[pl.BlockSpec((t, w), lambda b, i: (b * nq + i, c0)),
                  pl.BlockSpec((S, w), lambda b, i: (b, c0 + 1)),
                  pl.BlockSpec((S, w), lambda b, i: (b, c0 + 2))],
        out_specs=pl.BlockSpec((t, w), lambda b, i: (b * nq + i, 0)),
        out_shape=jax.ShapeDtypeStruct((B * S, w), jnp.bfloat16),
        scratch_shapes=[pltpu.VMEM((SB_HEADS // 2, 2 * t, LANES), jnp.float32)] * 2,
        compiler_params=_cparams(("parallel", "arbitrary")),
        name="sb_attn",
    )(ub, ub, ub)


def _layer_norm(x, g, b):
    mu = jnp.mean(x, -1, keepdims=True)
    xc = x - mu
    var = jnp.mean(xc * xc, -1, keepdims=True)
    return xc * lax.rsqrt(var + EPS) * g + b


def _out_proj_kernel(h_ref, ym_ref, yd_ref, ys_ref, wm_ref, wd_ref, ws_ref, g_ref, b_ref, o_ref, *, alpha):
    mix = _dot(ym_ref[...], wm_ref[...]) + _dot(yd_ref[...], wd_ref[...]) + _dot(ys_ref[...], ws_ref[...])
    o_ref[...] = _layer_norm(alpha * h_ref[...] + mix, g_ref[...], b_ref[...])


def _out_proj(h2, ym, yd, ys, wm, wd, ws, g, b, alpha, tm=TOKEN_TILE):
    T = h2.shape[0]
    row = lambda a: pl.BlockSpec((tm, a.shape[1]), lambda i: (i, 0))
    full = lambda a: pl.BlockSpec(a.shape, lambda i: (0, 0))
    return pl.pallas_call(
        functools.partial(_out_proj_kernel, alpha=alpha),
        grid=(T // tm,),
        in_specs=[row(h2), row(ym), row(yd), row(ys), full(wm), full(wd), full(ws), full(g), full(b)],
        out_specs=pl.BlockSpec((tm, D_MODEL), lambda i: (i, 0)),
        out_shape=jax.ShapeDtypeStruct((T, D_MODEL), jnp.float32),
        compiler_params=_cparams(("parallel",)),
        name="out_proj_ln",
    )(h2, ym, yd, ys, wm, wd, ws, g, b)


def _first_max(vals):
    m = vals[0]
    for v in vals[1:]:
        m = jnp.maximum(m, v)
    taken = jnp.zeros(m.shape, jnp.bool_)
    hot = []
    for v in vals:
        is_first = jnp.logical_and(v == m, jnp.logical_not(taken))
        hot.append(is_first)
        taken = jnp.logical_or(taken, is_first)
    return m, hot


def _router_kernel(h_ref, rw_ref, rb_ref, gate_ref, slot_ref, dest_ref, chosen_ref, first_ref):
    h = h_ref[...]
    rw = rw_ref[...]
    h_hi, rw_hi = _bf16(h), _bf16(rw)
    h_lo, rw_lo = _bf16(h - h_hi.astype(jnp.float32)), _bf16(rw - rw_hi.astype(jnp.float32))
    logits = (_dot(h_hi, rw_hi) + (_dot(h_hi, rw_lo) + _dot(h_lo, rw_hi))).T[:N_EXPERTS]
    scores = jax.nn.sigmoid(logits)
    sel = scores + rb_ref[...]
    ninf = -jnp.inf
    group_score, first, second = [], [], []
    for g in range(N_GROUPS):
        vals = [sel[g * EXPERTS_PER_GROUP + k:g * EXPERTS_PER_GROUP + k + 1, :] for k in range(EXPERTS_PER_GROUP)]
        m1, hot1 = _first_max(vals)
        m2, hot2 = _first_max([jnp.where(hh, ninf, v) for hh, v in zip(hot1, vals)])
        group_score.append(m1 + m2)
        first.append(hot1)
        second.append(hot2)
    _, best = _first_max(group_score)
    w1 = jnp.zeros_like(group_score[0])
    w2 = jnp.zeros_like(group_score[0])
    for g in range(N_GROUPS):
        for k in range(EXPERTS_PER_GROUP):
            e = g * EXPERTS_PER_GROUP + k
            sc = scores[e:e + 1, :]
            w1 = jnp.where(jnp.logical_and(best[g], first[g][k]), sc, w1)
            w2 = jnp.where(jnp.logical_and(best[g], second[g][k]), sc, w2)
    tot = w1 + w2
    for g in range(N_GROUPS):
        for k in range(EXPERTS_PER_GROUP):
            e = g * EXPERTS_PER_GROUP + k
            is1 = jnp.logical_and(best[g], first[g][k])
            is2 = jnp.logical_and(best[g], second[g][k])
            gate_ref[e:e + 1, :] = jnp.where(is1, w1 / tot, 0.0) + jnp.where(is2, w2 / tot, 0.0)
            chosen_ref[e:e + 1, :] = jnp.where(jnp.logical_or(is1, is2), 1.0, 0.0)
            first_ref[e:e + 1, :] = jnp.where(is1, 1.0, 0.0)
    chosen = chosen_ref[...]
    tm = chosen.shape[1]
    before = (lax.broadcasted_iota(jnp.int32, (tm, tm), 0) < lax.broadcasted_iota(jnp.int32, (tm, tm), 1))
    rank = _dot(_bf16(chosen), jnp.where(before, 1.0, 0.0).astype(jnp.bfloat16)).astype(jnp.int32)
    slot_ref[...] = jnp.where(chosen > 0.0, rank, -1)
    expert = lax.broadcasted_iota(jnp.int32, chosen.shape, 0)
    parked = jnp.logical_and(chosen > 0.0, rank < MOE_SLOTS)
    code = jnp.where(parked, expert * MOE_SLOTS + rank + 1, 0)
    is_first = first_ref[...] > 0.0
    dest_ref[0:1, :] = jnp.sum(jnp.where(is_first, code, 0), axis=0, keepdims=True) - 1
    dest_ref[1:2, :] = jnp.sum(jnp.where(is_first, 0, code), axis=0, keepdims=True) - 1


def _router(h2, rw_pad, rb_col):
    T = h2.shape[0]
    tm = MOE_SUB
    tok = lambda rows: pl.BlockSpec((rows, tm), lambda i: (0, i))
    return pl.pallas_call(
        _router_kernel,
        grid=(T // tm,),
        in_specs=[pl.BlockSpec((tm, D_MODEL), lambda i: (i, 0)),
                  pl.BlockSpec(rw_pad.shape, lambda i: (0, 0)),
                  pl.BlockSpec(rb_col.shape, lambda i: (0, 0))],
        out_specs=[tok(N_EXPERTS), tok(N_EXPERTS), tok(2)],
        out_shape=[jax.ShapeDtypeStruct((N_EXPERTS, T), jnp.float32),
                   jax.ShapeDtypeStruct((N_EXPERTS, T), jnp.int32),
                   jax.ShapeDtypeStruct((2, T), jnp.int32)],
        scratch_shapes=[pltpu.VMEM((N_EXPERTS, tm), jnp.float32)] * 2,
        compiler_params=_cparams(("parallel",)),
        name="router",
    )(h2, rw_pad, rb_col)


def _moe_kernel(npass_ref, h_ref, slot_ref, slot_t_ref, dest_t_ref, gate_ref, p_ref, wg_ref, wu_ref, wd_ref,
                pg_ref, pp_ref, g_ref, b_ref, o_ref, over_ref, xb_ref, y_ref, *, alpha, n_sub):
    i = pl.program_id(0)
    step = pl.program_id(1)
    R, SUB = MOE_SLOTS, MOE_SUB

    @pl.when(step == 0)
    def _():
        xb_ref[...] = _bf16(h_ref[...])
        over_ref[...] = jnp.zeros(over_ref.shape, jnp.float32)

    def expert_pass(e, k, c):
        slot_row = slot_ref[pl.ds(e, 1), :]
        gate_row = gate_ref[pl.ds(e, 1), :]
        xs, gs = [], []
        for j in range(n_sub):
            sl = slot_row[:, j * SUB:(j + 1) * SUB] - c * R
            hit = lax.broadcasted_iota(jnp.int32, (R, SUB), 0) == sl
            onehot = jnp.where(hit, 1.0, 0.0).astype(jnp.bfloat16)
            xs.append(_bf16(_dot(onehot, xb_ref[j * SUB:(j + 1) * SUB, :])))
            gs.append(jnp.sum(jnp.where(hit, gate_row[:, j * SUB:(j + 1) * SUB], 0.0), axis=1, keepdims=True))
        xe = jnp.concatenate(xs, axis=0)
        hid = jax.nn.silu(_dot(xe, wg_ref[k])) * _dot(xe, wu_ref[k])
        y = _dot(_bf16(hid), wd_ref[k])
        return [_bf16(y[j * R:(j + 1) * R] * gs[j]) for j in range(n_sub)]

    for k in range(MOE_EXPERTS_PER_STEP):
        e = step * MOE_EXPERTS_PER_STEP + k
        ys = expert_pass(e, k, 0)
        for j in range(n_sub):
            y_ref[j, pl.ds(pl.multiple_of(e * R, 16), R), :] = ys[j]

    for k in range(MOE_EXPERTS_PER_STEP):
        e = step * MOE_EXPERTS_PER_STEP + k

        def overflow_pass(c, carry, e=e, k=k):
            ys = expert_pass(e, k, c)
            lane16 = lax.broadcasted_iota(jnp.int32, (SUB, N_EXPERTS), 1)
            for j in range(n_sub):
                rows = slice(j * SUB, (j + 1) * SUB)
                slot_col = jnp.sum(jnp.where(lane16 == e, slot_t_ref[rows, :], 0), axis=1, keepdims=True)
                hit_t = lax.broadcasted_iota(jnp.int32, (SUB, R), 1) == slot_col - c * R
                over_ref[rows, :] += _dot(jnp.where(hit_t, 1.0, 0.0).astype(jnp.bfloat16), ys[j])
            return carry

        lax.fori_loop(1, npass_ref[i * N_EXPERTS + e], overflow_pass, 0)

    @pl.when(step == N_EXPERTS // MOE_EXPERTS_PER_STEP - 1)
    def _():
        lane = lax.broadcasted_iota(jnp.int32, (SUB, N_EXPERTS * R), 1)
        for j in range(n_sub):
            rows = slice(j * SUB, (j + 1) * SUB)
            dest = dest_t_ref[rows, :]
            hit = jnp.logical_or(lane == dest[:, 0:1], lane == dest[:, 1:2])
            scatter = jnp.where(hit, 1.0, 0.0).astype(jnp.bfloat16)
            ffn = _dot(scatter, y_ref[j]) + over_ref[rows, :]
            h = h_ref[rows, :]
            ple = jax.nn.sigmoid(_dot(xb_ref[rows, :], pg_ref[...])) * _dot(_bf16(p_ref[rows, :]), pp_ref[...])
            o_ref[rows, :] = _layer_norm(alpha * h + ffn + ple, g_ref[...], b_ref[...])


def _moe(h2, slot, dest, gate, p2, wg, wu, wd, layer, pg, pp, g, b, alpha, tm=MOE_TILE):
    T = h2.shape[0]
    n_sub = tm // MOE_SUB
    n_tiles = T // tm
    per = MOE_EXPERTS_PER_STEP
    count = jnp.sum((slot >= 0).reshape(N_EXPERTS, n_tiles, n_sub, MOE_SUB), axis=3)
    npass = jnp.maximum(1, (jnp.max(count, axis=2) + MOE_SLOTS - 1) // MOE_SLOTS).T.reshape(-1).astype(jnp.int32)
    full = lambda a: pl.BlockSpec(a.shape, lambda i, e, *_: (0, 0))
    experts = lambda a: pl.BlockSpec((None, per) + a.shape[2:], lambda i, e, *_: (layer, e, 0, 0))
    grid_spec = pltpu.PrefetchScalarGridSpec(
        num_scalar_prefetch=1,
        grid=(n_tiles, N_EXPERTS // per),
        in_specs=[pl.BlockSpec((tm, D_MODEL), lambda i, e, *_: (i, 0)),
                  pl.BlockSpec((N_EXPERTS, tm), lambda i, e, *_: (0, i)),
                  pl.BlockSpec((tm, N_EXPERTS), lambda i, e, *_: (i, 0)),
                  pl.BlockSpec((tm, 2), lambda i, e, *_: (i, 0)),
                  pl.BlockSpec((N_EXPERTS, tm), lambda i, e, *_: (0, i)),
                  pl.BlockSpec((tm, PLE_DIM), lambda i, e, *_: (i, 0)),
                  experts(wg), experts(wu), experts(wd),
                  full(pg), full(pp), full(g), full(b)],
        out_specs=pl.BlockSpec((tm, D_MODEL), lambda i, e, *_: (i, 0)),
        scratch_shapes=[pltpu.VMEM((tm, D_MODEL), jnp.float32),
                        pltpu.VMEM((tm, D_MODEL), jnp.bfloat16),
                        pltpu.VMEM((n_sub, N_EXPERTS * MOE_SLOTS, D_MODEL), jnp.bfloat16)],
    )
    return pl.pallas_call(
        functools.partial(_moe_kernel, alpha=alpha, n_sub=n_sub),
        grid_spec=grid_spec,
        out_shape=jax.ShapeDtypeStruct((T, D_MODEL), jnp.float32),
        compiler_params=_cparams(("parallel", "arbitrary")),
        name="moe_ple_ln",
    )(npass, h2, slot, slot.T, dest.T, gate, p2, wg, wu, wd, pg, pp, g, b)


def _rotate_half_cols(w):
    half = w.shape[1] // 2
    return jnp.concatenate([-w[:, half:], w[:, :half]], axis=1)


def _in_proj_weights(w_in):
    z = lambda n: jnp.zeros((D_MODEL, n), w_in.dtype)
    cq = w_in[:, :MLA_Q_RANK]
    ckv = w_in[:, MLA_Q_RANK:MLA_Q_RANK + MLA_KV_RANK]
    kr = w_in[:, MLA_Q_RANK + MLA_KV_RANK:MLA_IN]
    n_dq = DIFF_HEADS * 2 * DIFF_QK
    sb0 = MLA_IN + DIFF_IN
    n_sq = SB_HEADS * SB_D
    cols = [ckv,
            z(MLA_NOPE), kr, z(LANES - MLA_NOPE - MLA_ROPE),
            z(MLA_NOPE), _rotate_half_cols(kr), z(LANES - MLA_NOPE - MLA_ROPE),
            cq, z(UA_CQ_W - MLA_Q_RANK),
            w_in[:, MLA_IN:MLA_IN + n_dq] * (DIFF_QK ** -0.5 * LOG2E),
            w_in[:, MLA_IN + n_dq:sb0],
            w_in[:, sb0:sb0 + n_sq] * (SB_D ** -0.5),
            w_in[:, sb0 + n_sq:]]
    return _bf16(jnp.concatenate(cols, axis=1))


def _mla_up_weights(w_uq, w_ukv):
    dq = MLA_NOPE + MLA_ROPE
    zq = lambda n: jnp.zeros((MLA_Q_RANK, n), w_uq.dtype)
    plain, rot = [], []
    for h in range(MLA_HEADS):
        wh = w_uq[:, h * dq:(h + 1) * dq]
        plain += [wh, zq(LANES - dq)]
        rot += [zq(MLA_NOPE), _rotate_half_cols(wh[:, MLA_NOPE:]), zq(LANES - dq)]
    wq = jnp.concatenate(plain + rot, axis=1)
    wq = jnp.concatenate([wq, jnp.zeros((UA_CQ_W - MLA_Q_RANK, wq.shape[1]), wq.dtype)], axis=0)
    dkv = MLA_NOPE + MLA_V
    zk = lambda n: jnp.zeros((MLA_KV_RANK, n), w_ukv.dtype)
    kcols, vcols = [], []
    for h in range(MLA_HEADS):
        wh = w_ukv[:, h * dkv:(h + 1) * dkv]
        kcols += [wh[:, :MLA_NOPE], zk(LANES - MLA_NOPE)]
        vcols += [wh[:, MLA_NOPE:]]
    wkv = jnp.concatenate(kcols + vcols, axis=1)
    return _bf16(wq), _bf16(wkv)


def _attn_tiles(S):
    pick = lambda want: max(c for c in (128, 256, 512, 1024) if c <= want and S % c == 0)
    return pick(512), pick(512), pick(256)


def kernel(x, p, positions, w_in, mla_q_norm, mla_w_uq, mla_kv_norm, mla_w_ukv, diff_lambda_q1, diff_lambda_k1, diff_lambda_q2, diff_lambda_k2, diff_subln, rel_bias, w_o, ln1_g, ln1_b, router_w, router_b, w_gate, w_up, w_down, ple_proj, ple_gate, ln2_g, ln2_b):
    B, S, _ = x.shape
    depth = w_in.shape[0]
    T = B * S
    alpha = (2 * depth) ** 0.25
    t_mla, t_diff, t_sb = _attn_tiles(S)
    nq = S // t_diff

    pos_col = positions.reshape(T, 1)
    pos_tiles = positions.reshape(B * nq, 1, t_diff)
    tile_pos = positions.reshape(B * nq, t_diff)
    qmin = jnp.min(tile_pos, axis=1)
    kmax = jnp.max(tile_pos, axis=1)
    first = tile_pos[:, 0]
    consec = jnp.all(tile_pos[:, 1:] - tile_pos[:, :-1] == 1, axis=1).astype(jnp.int32)
    rb_flat = rel_bias.T.reshape(-1).astype(jnp.float32)

    half = MLA_ROPE // 2
    inv = ROPE_THETA ** (-jnp.arange(half, dtype=jnp.float32) / half)
    cos_t, sin_t = _rope_table(positions.reshape(1, T), jnp.concatenate([inv, inv]).reshape(MLA_ROPE, 1))
    pad_l = lambda v: jnp.full((T, MLA_NOPE), v, jnp.float32)
    pad_r = jnp.zeros((T, LANES - MLA_NOPE - MLA_ROPE), jnp.float32)
    cosf = jnp.concatenate([pad_l(1.0), cos_t.T, pad_r], axis=1)
    sinf = jnp.concatenate([pad_l(0.0), sin_t.T, pad_r], axis=1)

    rw_pad = jnp.pad(router_w.astype(jnp.float32), ((0, 0), (0, LANES - N_EXPERTS)))
    rb_col = router_b.reshape(N_EXPERTS, 1).astype(jnp.float32)
    row = lambda a: a.reshape(1, -1)

    wg_all, wu_all, wd_all = _bf16(w_gate), _bf16(w_up), _bf16(w_down)

    h = x.reshape(T, D_MODEL)
    for i in range(depth):
        lam_init = 0.8 - 0.6 * math.exp(-0.3 * i)
        wq, wkv = _mla_up_weights(mla_w_uq[i], mla_w_ukv[i])
        gq = jnp.pad(mla_q_norm[i], (0, UA_CQ_W - MLA_Q_RANK)).reshape(1, UA_CQ_W)
        ub, q_m, k_m, v_m = _in_proj(h, _in_proj_weights(w_in[i]), cosf, sinf, gq, row(mla_kv_norm[i]), wq, wkv)
        y_mla = _mla_attn(q_m, k_m, v_m, B, S, t=t_mla)
        y_diff = _diff_attn(ub, pos_col, pos_tiles, qmin, kmax, first, consec, rb_flat,
                            row(diff_lambda_q1[i]), row(diff_lambda_k1[i]),
                            row(diff_lambda_q2[i]), row(diff_lambda_k2[i]), row(diff_subln[i]),
                            B, S, lam_init, t=t_diff)
        y_sb = _sb_attn(ub, B, S, t=t_sb)
        wo = _bf16(w_o[i])
        n_m, n_d = MLA_HEADS * MLA_V, DIFF_HEADS * DIFF_V
        h = _out_proj(h, y_mla, y_diff, y_sb, wo[:n_m], wo[n_m:n_m + n_d], wo[n_m + n_d:],
                      row(ln1_g[i]), row(ln1_b[i]), alpha)
        gate, slot, dest = _router(h, rw_pad, rb_col)
        h = _moe(h, slot, dest, gate, p[i].reshape(T, PLE_DIM), wg_all, wu_all, wd_all, i,
                 _bf16(ple_gate[i]), _bf16(ple_proj[i]), row(ln2_g[i]), row(ln2_b[i]), alpha)
    return h.reshape(B, S, D_MODEL)
```

```python
import functools
import math

import jax
import jax.numpy as jnp
from jax import lax
from jax.experimental import pallas as pl
from jax.experimental.pallas import tpu as pltpu

D_MODEL = 1024
CHUNK = 64
PLE_DIM = 256
MLA_HEADS = 4
MLA_NOPE = 64
MLA_ROPE = 32
MLA_V = 64
MLA_Q_RANK = 192
MLA_KV_RANK = 128
ROPE_THETA = 10000.0
DIFF_HEADS = 4
DIFF_QK = 64
DIFF_V = 2 * DIFF_QK
SB_HEADS = 4
SB_D = 64
REL_BUCKETS = 32
REL_MAX_DIST = 128
N_EXPERTS = 16
N_GROUPS = 4
EXPERTS_PER_GROUP = N_EXPERTS // N_GROUPS
D_EXPERT = 512
MLA_IN = MLA_Q_RANK + MLA_KV_RANK + MLA_ROPE
DIFF_IN = 2 * DIFF_HEADS * 2 * DIFF_QK + DIFF_HEADS * DIFF_V
SB_IN = 3 * SB_HEADS * SB_D
EPS = 1e-5
NEG_INF = -1e30

LANES = 128
VMEM_LIMIT = 56 * 1024 * 1024

UA_CKV = 0
UA_KR = UA_CKV + MLA_KV_RANK
UA_KRS = UA_KR + LANES
UA_CQ = UA_KRS + LANES
UA_CQ_W = 2 * LANES
UA_W = UA_CQ + UA_CQ_W
UB_W = DIFF_IN + SB_IN
MLA_QK_W = MLA_HEADS * LANES
MLA_V_W = MLA_HEADS * MLA_V

T5_FAR = 91
SB_LOG_ZERO = -88.0
TOKEN_TILE = 1024
PROJ_COLS = 512
MOE_TILE = 1024
DIFF_HEADS_PER_STEP = 2
FAR_TILE_WIDTHS = (4, 2)
MOE_SUB = 512
MOE_SLOTS = 96
MOE_EXPERTS_PER_STEP = 2

_NT = (((1,), (1,)), ((), ()))
LOG2E = math.log2(math.e)


def _cparams(sem):
    return pltpu.CompilerParams(dimension_semantics=sem, vmem_limit_bytes=VMEM_LIMIT)


def _bf16(a):
    return a.astype(jnp.bfloat16)


def _dot(a, b):
    return jnp.dot(a, b, preferred_element_type=jnp.float32)


def _in_proj_kernel(x_ref, w_ref, cos_ref, sin_ref, gq_ref, gkv_ref, wq_ref, wkv_ref,
                    ub_ref, q_ref, k_ref, v_ref, ua_ref):
    x = _bf16(x_ref[...])
    step = PROJ_COLS
    for c in range(0, UA_W, step):
        e = min(c + step, UA_W)
        ua_ref[:, c:e] = _dot(x, w_ref[:, c:e])
    for c in range(0, UB_W, step):
        e = min(c + step, UB_W)
        ub_ref[:, c:e] = _bf16(_dot(x, w_ref[:, UA_W + c:UA_W + e]))
    _mla_prep_kernel(ua_ref, cos_ref, sin_ref, gq_ref, gkv_ref, wq_ref, wkv_ref, q_ref, k_ref, v_ref)


def _in_proj(h2, w_all, cosf, sinf, gq, gkv, wq, wkv, tm=TOKEN_TILE):
    T = h2.shape[0]
    full = lambda a: pl.BlockSpec(a.shape, lambda i: (0, 0))
    row = lambda w: pl.BlockSpec((tm, w), lambda i: (i, 0))
    return pl.pallas_call(
        _in_proj_kernel,
        grid=(T // tm,),
        in_specs=[row(D_MODEL), full(w_all), row(LANES), row(LANES), full(gq), full(gkv), full(wq), full(wkv)],
        out_specs=[row(UB_W), row(MLA_QK_W), row(MLA_QK_W), row(MLA_V_W)],
        out_shape=[jax.ShapeDtypeStruct((T, UB_W), jnp.bfloat16),
                   jax.ShapeDtypeStruct((T, MLA_QK_W), jnp.bfloat16),
                   jax.ShapeDtypeStruct((T, MLA_QK_W), jnp.bfloat16),
                   jax.ShapeDtypeStruct((T, MLA_V_W), jnp.bfloat16)],
        scratch_shapes=[pltpu.VMEM((tm, UA_W), jnp.float32)],
        compiler_params=_cparams(("parallel",)),
        name="in_proj_mla_prep",
    )(h2, w_all, cosf, sinf, gq, gkv, wq, wkv)


def _rope_table_kernel(pos_ref, inv_ref, cos_ref, sin_ref):
    ang = pos_ref[...].astype(jnp.float32) * inv_ref[...]
    cos_ref[...] = jnp.cos(ang)
    sin_ref[...] = jnp.sin(ang)


def _rope_table(pos_row, inv_col, tn=2048):
    T = pos_row.shape[1]
    n = inv_col.shape[0]
    tn = min(tn, T)
    return pl.pallas_call(
        _rope_table_kernel,
        grid=(T // tn,),
        in_specs=[pl.BlockSpec((1, tn), lambda i: (0, i)),
                  pl.BlockSpec((n, 1), lambda i: (0, 0))],
        out_specs=[pl.BlockSpec((n, tn), lambda i: (0, i))] * 2,
        out_shape=[jax.ShapeDtypeStruct((n, T), jnp.float32)] * 2,
        compiler_params=_cparams(("parallel",)),
        name="rope_table",
    )(pos_row, inv_col)


def _mla_prep_kernel(ua_ref, cos_ref, sin_ref, gq_ref, gkv_ref, wq_ref, wkv_ref,
                     q_ref, k_ref, v_ref):
    ckv = ua_ref[:, UA_CKV:UA_CKV + MLA_KV_RANK]
    kr = ua_ref[:, UA_KR:UA_KR + LANES]
    krs = ua_ref[:, UA_KRS:UA_KRS + LANES]
    cq = ua_ref[:, UA_CQ:UA_CQ + UA_CQ_W]
    cqn = cq * lax.rsqrt(jnp.sum(cq * cq, -1, keepdims=True) * (1.0 / MLA_Q_RANK) + EPS) * gq_ref[...]
    ckvn = ckv * lax.rsqrt(jnp.sum(ckv * ckv, -1, keepdims=True) * (1.0 / MLA_KV_RANK) + EPS) * gkv_ref[...]
    cosf = cos_ref[...]
    sinf = sin_ref[...]
    scale = (MLA_NOPE + MLA_ROPE) ** -0.5 * LOG2E
    cqb = _bf16(cqn)
    ckvb = _bf16(ckvn)
    k_rope = kr * cosf + krs * sinf
    for h in range(MLA_HEADS):
        head = slice(h * LANES, (h + 1) * LANES)
        rot = slice(MLA_QK_W + h * LANES, MLA_QK_W + (h + 1) * LANES)
        q_ref[:, head] = _bf16((_dot(cqb, wq_ref[:, head]) * cosf + _dot(cqb, wq_ref[:, rot]) * sinf) * scale)
        k_ref[:, head] = _bf16(_dot(ckvb, wkv_ref[:, head]) + k_rope)
    v_ref[...] = _bf16(_dot(ckvb, wkv_ref[:, MLA_QK_W:MLA_QK_W + MLA_V_W]))


def _chunk_mask(tq, tk):
    qc = lax.broadcasted_iota(jnp.int32, (tq, tk), 0) // CHUNK
    kc = lax.broadcasted_iota(jnp.int32, (tq, tk), 1) // CHUNK
    return kc <= qc


def _softmax_update(s, v, m_ref, l_ref, acc_ref, idx, shift=None):
    tk = s.shape[1]
    m_old = m_ref[idx]
    row_max = jnp.max(s, axis=1, keepdims=True)
    if shift is not None:
        row_max = row_max + shift
    m_new = jnp.maximum(m_old, row_max)
    alpha = jnp.exp2(m_old - m_new)
    m_sub = m_new if shift is None else m_new - shift
    p = jnp.exp2(s - jnp.concatenate([m_sub] * (tk // LANES), axis=1))
    psum = p[:, :LANES]
    for c in range(LANES, tk, LANES):
        psum = psum + p[:, c:c + LANES]
    l_ref[idx] = alpha * l_ref[idx] + psum
    acc_ref[idx] = alpha * acc_ref[idx] + _dot(_bf16(p), v)
    m_ref[idx] = m_new


def _softmax_init(m_ref, l_ref, acc_ref):
    m_ref[...] = jnp.full(m_ref.shape, NEG_INF, jnp.float32)
    l_ref[...] = jnp.zeros(l_ref.shape, jnp.float32)
    acc_ref[...] = jnp.zeros(acc_ref.shape, jnp.float32)


def _softmax_result(l_ref, acc_ref, idx):
    return acc_ref[idx] / jnp.sum(l_ref[idx], axis=1, keepdims=True)


def _mla_attn_kernel(q_ref, k_ref, v_ref, o_ref, m_ref, l_ref, acc_ref, *, t):
    qi = pl.program_id(1)
    _softmax_init(m_ref, l_ref, acc_ref)

    def tile(j, masked, width=1):
        ks = pl.multiple_of(j * t, t)
        for hh in range(MLA_HEADS):
            q = q_ref[:, hh * LANES:(hh + 1) * LANES]
            k = k_ref[pl.ds(ks, width * t), hh * LANES:(hh + 1) * LANES]
            v = v_ref[pl.ds(ks, width * t), (hh // 2) * LANES:(hh // 2 + 1) * LANES]
            s = lax.dot_general(q, k, _NT, preferred_element_type=jnp.float32)
            if masked:
                s = jnp.where(_chunk_mask(t, t), s, NEG_INF)
            _softmax_update(s, v, m_ref, l_ref, acc_ref, hh)

    done = 0
    for width in FAR_TILE_WIDTHS + (1,):
        trips = (qi - done) // width

        def body(i, carry, width=width, done=done):
            tile(done + i * width, False, width=width)
            return carry

        lax.fori_loop(0, trips, body, 0)
        done = done + trips * width

    tile(qi, True)
    lane = lax.broadcasted_iota(jnp.int32, (t, LANES), 1)
    for pair in range(MLA_HEADS // 2):
        o_ref[:, pair * LANES:(pair + 1) * LANES] = _bf16(
            jnp.where(lane < MLA_V, _softmax_result(l_ref, acc_ref, 2 * pair),
                      _softmax_result(l_ref, acc_ref, 2 * pair + 1)))


def _mla_attn(q, k, v, B, S, t):
    nq = S // t
    wq, wv = MLA_QK_W, MLA_V_W
    return pl.pallas_call(
        functools.partial(_mla_attn_kernel, t=t),
        grid=(B, nq),
        in_specs=[pl.BlockSpec((t, wq), lambda b, i: (b * nq + i, 0)),
                  pl.BlockSpec((S, wq), lambda b, i: (b, 0)),
                  pl.BlockSpec((S, wv), lambda b, i: (b, 0))],
        out_specs=pl.BlockSpec((t, wv), lambda b, i: (b * nq + i, 0)),
        out_shape=jax.ShapeDtypeStruct((B * S, wv), jnp.bfloat16),
        scratch_shapes=[pltpu.VMEM((MLA_HEADS, t, LANES), jnp.float32)] * 3,
        compiler_params=_cparams(("parallel", "arbitrary")),
        name="mla_attn",
    )(q, k, v)


def _t5_bias(rel, rb_ref, h):
    nb = REL_BUCKETS // 2
    max_exact = nb // 2
    n = jnp.abs(rel)
    nf = jnp.maximum(n, 1).astype(jnp.float32)
    large = max_exact + (jnp.log(nf / max_exact) / math.log(REL_MAX_DIST / max_exact)
                         * (nb - max_exact)).astype(jnp.int32)
    large = jnp.minimum(large, nb - 1)
    low = jnp.where(n < max_exact, n, large)
    neg = jnp.zeros(rel.shape, jnp.float32)
    pos = jnp.zeros(rel.shape, jnp.float32)
    for j in range(nb):
        eq = low == j
        neg = jnp.where(eq, rb_ref[h * REL_BUCKETS + j], neg)
        pos = jnp.where(eq, rb_ref[h * REL_BUCKETS + nb + j], pos)
    return jnp.where(rel > 0, pos, neg)


def _diff_attn_kernel(qmin_ref, kmax_ref, first_ref, consec_ref,
                      rb_ref, q_ref, k_ref, v_ref, posq_ref, posk_ref,
                      lq1_ref, lk1_ref, lq2_ref, lk2_ref, sub_ref,
                      o_ref, m_ref, l_ref, acc_ref, bias_ref, cache_ref, *, t, nq, lam_init):
    b = pl.program_id(0)
    hp = pl.program_id(1)
    qi = pl.program_id(2)
    _softmax_init(m_ref, l_ref, acc_ref)

    @pl.when(qi == 0)
    def _():
        for n in range(4):
            cache_ref[n] = 0

    lane = lax.broadcasted_iota(jnp.int32, (t, LANES), 1)
    qs = []
    for hh in range(DIFF_HEADS_PER_STEP):
        q = q_ref[:, hh * LANES:(hh + 1) * LANES]
        zero = jnp.zeros_like(q)
        qs.append(jnp.concatenate([jnp.where(lane < DIFF_QK, q, zero), jnp.where(lane < DIFF_QK, zero, q)], axis=0))
    tq = b * nq + qi
    qmin = qmin_ref[tq]

    def toeplitz_bias(j, h):
        d0 = first_ref[b * nq + j] - first_ref[tq]
        x = lax.broadcasted_iota(jnp.int32, (1, 2 * t), 1)
        g = _t5_bias(d0 + jnp.where(x < t, x, x - 2 * t), rb_ref, h)
        g = pltpu.roll(jnp.broadcast_to(g, (t, 2 * t)), 0, 1, stride=1, stride_axis=0)
        return g[:, :t]

    def general_bias(j, h):
        rel = posk_ref[j] - posq_ref[...]
        return _t5_bias(rel, rb_ref, h)

    def masked_bias(bias, masked):
        bias = bias * LOG2E
        return jnp.where(_chunk_mask(t, t), bias, NEG_INF) if masked else bias

    def tile(j, masked, bias_fn, width=1):
        ks = pl.multiple_of(j * t, t)
        for hh in range(DIFF_HEADS_PER_STEP):
            h = hp * DIFF_HEADS_PER_STEP + hh
            k = k_ref[pl.ds(ks, width * t), hh * LANES:(hh + 1) * LANES]
            v = v_ref[pl.ds(ks, width * t), hh * LANES:(hh + 1) * LANES]
            s = lax.dot_general(qs[hh], k, _NT, preferred_element_type=jnp.float32)
            if bias_fn is None:
                shift = rb_ref[h * REL_BUCKETS + REL_BUCKETS // 2 - 1] * LOG2E
            else:
                shift = None
                s = (s.reshape(2, t, t) + bias_fn(j, h, hh)[None]).reshape(2 * t, t)
            _softmax_update(s, v, m_ref, l_ref, acc_ref, hh, shift)

    def near_tile(j, masked):
        consecutive = jnp.logical_and(consec_ref[tq] == 1, consec_ref[b * nq + j] == 1)
        slot = 1 if masked else 0

        @pl.when(consecutive)
        def _():
            d0 = first_ref[b * nq + j] - first_ref[tq]
            stale = jnp.logical_or(cache_ref[2 * slot + 1] != 1, cache_ref[2 * slot] != d0)

            @pl.when(stale)
            def _():
                for hh in range(DIFF_HEADS_PER_STEP):
                    h = hp * DIFF_HEADS_PER_STEP + hh
                    bias_ref[slot, hh] = masked_bias(toeplitz_bias(j, h), masked)
                cache_ref[2 * slot] = d0
                cache_ref[2 * slot + 1] = 1

            tile(j, masked, lambda j, h, hh: bias_ref[slot, hh])

        @pl.when(jnp.logical_not(consecutive))
        def _():
            tile(j, masked, lambda j, h, hh: masked_bias(general_bias(j, h), masked))

    def is_far(j):
        return kmax_ref[b * nq + jnp.minimum(j, qi)] - qmin <= -T5_FAR

    def wide_far_loop(start, width):
        def cond(j):
            ok = j + width - 1 < qi
            for d in range(width):
                ok = jnp.logical_and(ok, is_far(j + d))
            return ok

        def step(j):
            tile(j, False, None, width=width)
            return j + width

        return lax.while_loop(cond, step, start)

    def body(j, carry):
        far = is_far(j)

        @pl.when(far)
        def _():
            tile(j, False, None)

        @pl.when(jnp.logical_not(far))
        def _():
            near_tile(j, False)

        return carry

    done = 0
    for width in FAR_TILE_WIDTHS:
        done = wide_far_loop(done, width)
    lax.fori_loop(done, qi, body, 0)
    near_tile(qi, True)

    f32 = jnp.float32
    lam = (jnp.exp(jnp.sum(lq1_ref[...].astype(f32) * lk1_ref[...].astype(f32), keepdims=True))
           - jnp.exp(jnp.sum(lq2_ref[...].astype(f32) * lk2_ref[...].astype(f32), keepdims=True))
           + lam_init)
    for hh in range(DIFF_HEADS_PER_STEP):
        a = _softmax_result(l_ref, acc_ref, hh)
        o = a[:t] - lam * a[t:]
        o = o * lax.rsqrt(jnp.mean(o * o, -1, keepdims=True) + EPS) * sub_ref[...] * (1.0 - lam_init)
        o_ref[:, hh * LANES:(hh + 1) * LANES] = _bf16(o)


def _diff_attn(ub, pos_col, pos_tiles, qmin, kmax, first, consec, rb_flat, lq1, lk1, lq2, lk2, subln,
               B, S, lam_init, t):
    nq = S // t
    per = DIFF_HEADS_PER_STEP
    G = DIFF_HEADS // per
    wg = per * DIFF_V
    small = lambda a: pl.BlockSpec(a.shape, lambda b, h, i, *_: (0, 0))
    grid_spec = pltpu.PrefetchScalarGridSpec(
        num_scalar_prefetch=4,
        grid=(B, G, nq),
        in_specs=[pl.BlockSpec(memory_space=pltpu.SMEM),
                  pl.BlockSpec((t, wg), lambda b, h, i, *_: (b * nq + i, h)),
                  pl.BlockSpec((S, wg), lambda b, h, i, *_: (b, G + h)),
                  pl.BlockSpec((S, wg), lambda b, h, i, *_: (b, 2 * G + h)),
                  pl.BlockSpec((t, 1), lambda b, h, i, *_: (b * nq + i, 0)),
                  pl.BlockSpec((nq, 1, t), lambda b, h, i, *_: (b, 0, 0)),
                  small(lq1), small(lk1), small(lq2), small(lk2), small(subln)],
        out_specs=pl.BlockSpec((t, wg), lambda b, h, i, *_: (b * nq + i, h)),
        scratch_shapes=[pltpu.VMEM((per, 2 * t, LANES), jnp.float32)] * 3
                       + [pltpu.VMEM((2, per, t, t), jnp.float32), pltpu.SMEM((4,), jnp.int32)],
    )
    return pl.pallas_call(
        functools.partial(_diff_attn_kernel, t=t, nq=nq, lam_init=lam_init),
        grid_spec=grid_spec,
        out_shape=jax.ShapeDtypeStruct((B * S, DIFF_HEADS * DIFF_V), jnp.bfloat16),
        compiler_params=_cparams(("parallel", "parallel", "arbitrary")),
        name="diff_attn",
    )(qmin, kmax, first, consec, rb_flat, ub, ub, ub, pos_col, pos_tiles, lq1, lk1, lq2, lk2, subln)


def _sb_attn_kernel(q_ref, k_ref, v_ref, o_ref, c_ref, acc_ref, *, t):
    qi = pl.program_id(1)
    n_pairs = SB_HEADS // 2
    row = lax.broadcasted_iota(jnp.int32, (t, t), 0)
    col = lax.broadcasted_iota(jnp.int32, (t, t), 1)
    tri = jnp.where(row > col, 1.0, 0.0).astype(jnp.bfloat16)
    strict = (col < row)[None]
    lane = lax.broadcasted_iota(jnp.int32, (t, LANES), 1)
    qs = []
    for hp in range(n_pairs):
        q = q_ref[:, hp * LANES:(hp + 1) * LANES]
        zero = jnp.zeros_like(q)
        qs.append(jnp.concatenate([jnp.where(lane < SB_D, q, zero), jnp.where(lane < SB_D, zero, q)], axis=0))
    c_ref[...] = jnp.zeros(c_ref.shape, jnp.float32)
    acc_ref[...] = jnp.zeros(acc_ref.shape, jnp.float32)

    def tile(j, diag):
        ks = pl.multiple_of(j * t, t)
        for hp in range(n_pairs):
            k = k_ref[pl.ds(ks, t), hp * LANES:(hp + 1) * LANES]
            v = v_ref[pl.ds(ks, t), hp * LANES:(hp + 1) * LANES]
            z = lax.dot_general(qs[hp], k, _NT, preferred_element_type=jnp.float32)
            lf = -(jnp.maximum(z, 0.0) + jnp.log(1.0 + jnp.exp(-jnp.abs(z))))
            if diag:
                lf = jnp.where(strict, lf.reshape(2, t, t), 0.0).reshape(2 * t, t)
            hi = _bf16(lf)
            lo = _bf16(lf - hi.astype(jnp.float32))
            c = c_ref[hp]
            later = _dot(hi, tri) + _dot(lo, tri) + jnp.concatenate([c] * (t // LANES), axis=1)
            w = jnp.exp(lf + z + later)
            if diag:
                w = jnp.where(strict, w.reshape(2, t, t), 0.0).reshape(2 * t, t)
            acc_ref[hp] += _dot(_bf16(w), v)
            c_ref[hp] = c + jnp.sum(lf, axis=1, keepdims=True)

    tile(qi, True)

    def cond(carry):
        j, cmax = carry
        return jnp.logical_and(j >= 0, cmax > SB_LOG_ZERO)

    def body(carry):
        j, _ = carry
        tile(j, False)
        return j - 1, jnp.max(c_ref[...])

    lax.while_loop(cond, body, (qi - 1, jnp.max(c_ref[...])))
    for hp in range(n_pairs):
        acc = acc_ref[hp]
        o_ref[:, hp * LANES:(hp + 1) * LANES] = _bf16(jnp.where(lane < SB_D, acc[:t], acc[t:]))


def _sb_attn(ub, B, S, t):
    nq = S // t
    w = SB_HEADS * SB_D
    c0 = DIFF_IN // w
    return pl.pallas_call(
        functools.partial(_sb_attn_kernel, t=t),
        grid=(B, nq),
        in_specs=[pl.BlockSpec((t, w), lambda b, i: (b * nq + i, c0)),
                  pl.BlockSpec((S, w), lambda b, i: (b, c0 + 1)),
                  pl.BlockSpec((S, w), lambda b, i: (b, c0 + 2))],
        out_specs=pl.BlockSpec((t, w), lambda b, i: (b * nq + i, 0)),
        out_shape=jax.ShapeDtypeStruct((B * S, w), jnp.bfloat16),
        scratch_shapes=[pltpu.VMEM((SB_HEADS // 2, 2 * t, LANES), jnp.float32)] * 2,
        compiler_params=_cparams(("parallel", "arbitrary")),
        name="sb_attn",
    )(ub, ub, ub)


def _layer_norm(x, g, b):
    mu = jnp.mean(x, -1, keepdims=True)
    xc = x - mu
    var = jnp.mean(xc * xc, -1, keepdims=True)
    return xc * lax.rsqrt(var + EPS) * g + b


def _out_proj_kernel(h_ref, ym_ref, yd_ref, ys_ref, wm_ref, wd_ref, ws_ref, g_ref, b_ref, o_ref, *, alpha):
    mix = _dot(ym_ref[...], wm_ref[...]) + _dot(yd_ref[...], wd_ref[...]) + _dot(ys_ref[...], ws_ref[...])
    o_ref[...] = _layer_norm(alpha * h_ref[...] + mix, g_ref[...], b_ref[...])


def _out_proj(h2, ym, yd, ys, wm, wd, ws, g, b, alpha, tm=TOKEN_TILE):
    T = h2.shape[0]
    row = lambda a: pl.BlockSpec((tm, a.shape[1]), lambda i: (i, 0))
    full = lambda a: pl.BlockSpec(a.shape, lambda i: (0, 0))
    return pl.pallas_call(
        functools.partial(_out_proj_kernel, alpha=alpha),
        grid=(T // tm,),
        in_specs=[row(h2), row(ym), row(yd), row(ys), full(wm), full(wd), full(ws), full(g), full(b)],
        out_specs=pl.BlockSpec((tm, D_MODEL), lambda i: (i, 0)),
        out_shape=jax.ShapeDtypeStruct((T, D_MODEL), jnp.float32),
        compiler_params=_cparams(("parallel",)),
        name="out_proj_ln",
    )(h2, ym, yd, ys, wm, wd, ws, g, b)


def _first_max(vals):
    m = vals[0]
    for v in vals[1:]:
        m = jnp.maximum(m, v)
    taken = jnp.zeros(m.shape, jnp.bool_)
    hot = []
    for v in vals:
        is_first = jnp.logical_and(v == m, jnp.logical_not(taken))
        hot.append(is_first)
        taken = jnp.logical_or(taken, is_first)
    return m, hot


def _router_kernel(h_ref, rw_ref, rb_ref, gate_ref, slot_ref, dest_ref, chosen_ref, first_ref):
    h = h_ref[...]
    rw = rw_ref[...]
    h_hi, rw_hi = _bf16(h), _bf16(rw)
    h_lo, rw_lo = _bf16(h - h_hi.astype(jnp.float32)), _bf16(rw - rw_hi.astype(jnp.float32))
    logits = (_dot(h_hi, rw_hi) + (_dot(h_hi, rw_lo) + _dot(h_lo, rw_hi))).T[:N_EXPERTS]
    scores = jax.nn.sigmoid(logits)
    sel = scores + rb_ref[...]
    ninf = -jnp.inf
    group_score, first, second = [], [], []
    for g in range(N_GROUPS):
        vals = [sel[g * EXPERTS_PER_GROUP + k:g * EXPERTS_PER_GROUP + k + 1, :] for k in range(EXPERTS_PER_GROUP)]
        m1, hot1 = _first_max(vals)
        m2, hot2 = _first_max([jnp.where(hh, ninf, v) for hh, v in zip(hot1, vals)])
        group_score.append(m1 + m2)
        first.append(hot1)
        second.append(hot2)
    _, best = _first_max(group_score)
    w1 = jnp.zeros_like(group_score[0])
    w2 = jnp.zeros_like(group_score[0])
    for g in range(N_GROUPS):
        for k in range(EXPERTS_PER_GROUP):
            e = g * EXPERTS_PER_GROUP + k
            sc = scores[e:e + 1, :]
            w1 = jnp.where(jnp.logical_and(best[g], first[g][k]), sc, w1)
            w2 = jnp.where(jnp.logical_and(best[g], second[g][k]), sc, w2)
    tot = w1 + w2
    for g in range(N_GROUPS):
        for k in range(EXPERTS_PER_GROUP):
            e = g * EXPERTS_PER_GROUP + k
            is1 = jnp.logical_and(best[g], first[g][k])
            is2 = jnp.logical_and(best[g], second[g][k])
            gate_ref[e:e + 1, :] = jnp.where(is1, w1 / tot, 0.0) + jnp.where(is2, w2 / tot, 0.0)
            chosen_ref[e:e + 1, :] = jnp.where(jnp.logical_or(is1, is2), 1.0, 0.0)
            first_ref[e:e + 1, :] = jnp.where(is1, 1.0, 0.0)
    chosen = chosen_ref[...]
    tm = chosen.shape[1]
    before = (lax.broadcasted_iota(jnp.int32, (tm, tm), 0) < lax.broadcasted_iota(jnp.int32, (tm, tm), 1))
    rank = _dot(_bf16(chosen), jnp.where(before, 1.0, 0.0).astype(jnp.bfloat16)).astype(jnp.int32)
    slot_ref[...] = jnp.where(chosen > 0.0, rank, -1)
    expert = lax.broadcasted_iota(jnp.int32, chosen.shape, 0)
    parked = jnp.logical_and(chosen > 0.0, rank < MOE_SLOTS)
    code = jnp.where(parked, expert * MOE_SLOTS + rank + 1, 0)
    is_first = first_ref[...] > 0.0
    dest_ref[0:1, :] = jnp.sum(jnp.where(is_first, code, 0), axis=0, keepdims=True) - 1
    dest_ref[1:2, :] = jnp.sum(jnp.where(is_first, 0, code), axis=0, keepdims=True) - 1


def _router(h2, rw_pad, rb_col):
    T = h2.shape[0]
    tm = MOE_SUB
    tok = lambda rows: pl.BlockSpec((rows, tm), lambda i: (0, i))
    return pl.pallas_call(
        _router_kernel,
        grid=(T // tm,),
        in_specs=[pl.BlockSpec((tm, D_MODEL), lambda i: (i, 0)),
                  pl.BlockSpec(rw_pad.shape, lambda i: (0, 0)),
                  pl.BlockSpec(rb_col.shape, lambda i: (0, 0))],
        out_specs=[tok(N_EXPERTS), tok(N_EXPERTS), tok(2)],
        out_shape=[jax.ShapeDtypeStruct((N_EXPERTS, T), jnp.float32),
                   jax.ShapeDtypeStruct((N_EXPERTS, T), jnp.int32),
                   jax.ShapeDtypeStruct((2, T), jnp.int32)],
        scratch_shapes=[pltpu.VMEM((N_EXPERTS, tm), jnp.float32)] * 2,
        compiler_params=_cparams(("parallel",)),
        name="router",
    )(h2, rw_pad, rb_col)


def _moe_kernel(npass_ref, h_ref, slot_ref, slot_t_ref, dest_t_ref, gate_ref, p_ref, wg_ref, wu_ref, wd_ref,
                pg_ref, pp_ref, g_ref, b_ref, o_ref, over_ref, xb_ref, y_ref, *, alpha, n_sub):
    i = pl.program_id(0)
    step = pl.program_id(1)
    R, SUB = MOE_SLOTS, MOE_SUB

    @pl.when(step == 0)
    def _():
        xb_ref[...] = _bf16(h_ref[...])
        over_ref[...] = jnp.zeros(over_ref.shape, jnp.float32)

    def expert_pass(e, k, c):
        slot_row = slot_ref[pl.ds(e, 1), :]
        gate_row = gate_ref[pl.ds(e, 1), :]
        xs, gs = [], []
        for j in range(n_sub):
            sl = slot_row[:, j * SUB:(j + 1) * SUB] - c * R
            hit = lax.broadcasted_iota(jnp.int32, (R, SUB), 0) == sl
            onehot = jnp.where(hit, 1.0, 0.0).astype(jnp.bfloat16)
            xs.append(_bf16(_dot(onehot, xb_ref[j * SUB:(j + 1) * SUB, :])))
            gs.append(jnp.sum(jnp.where(hit, gate_row[:, j * SUB:(j + 1) * SUB], 0.0), axis=1, keepdims=True))
        xe = jnp.concatenate(xs, axis=0)
        hid = jax.nn.silu(_dot(xe, wg_ref[k])) * _dot(xe, wu_ref[k])
        y = _dot(_bf16(hid), wd_ref[k])
        return [_bf16(y[j * R:(j + 1) * R] * gs[j]) for j in range(n_sub)]

    for k in range(MOE_EXPERTS_PER_STEP):
        e = step * MOE_EXPERTS_PER_STEP + k
        ys = expert_pass(e, k, 0)
        for j in range(n_sub):
            y_ref[j, pl.ds(pl.multiple_of(e * R, 16), R), :] = ys[j]

    for k in range(MOE_EXPERTS_PER_STEP):
        e = step * MOE_EXPERTS_PER_STEP + k

        def overflow_pass(c, carry, e=e, k=k):
            ys = expert_pass(e, k, c)
            lane16 = lax.broadcasted_iota(jnp.int32, (SUB, N_EXPERTS), 1)
            for j in range(n_sub):
                rows = slice(j * SUB, (j + 1) * SUB)
                slot_col = jnp.sum(jnp.where(lane16 == e, slot_t_ref[rows, :], 0), axis=1, keepdims=True)
                hit_t = lax.broadcasted_iota(jnp.int32, (SUB, R), 1) == slot_col - c * R
                over_ref[rows, :] += _dot(jnp.where(hit_t, 1.0, 0.0).astype(jnp.bfloat16), ys[j])
            return carry

        lax.fori_loop(1, npass_ref[i * N_EXPERTS + e], overflow_pass, 0)

    @pl.when(step == N_EXPERTS // MOE_EXPERTS_PER_STEP - 1)
    def _():
        lane = lax.broadcasted_iota(jnp.int32, (SUB, N_EXPERTS * R), 1)
        for j in range(n_sub):
            rows = slice(j * SUB, (j + 1) * SUB)
            dest = dest_t_ref[rows, :]
            hit = jnp.logical_or(lane == dest[:, 0:1], lane == dest[:, 1:2])
            scatter = jnp.where(hit, 1.0, 0.0).astype(jnp.bfloat16)
            ffn = _dot(scatter, y_ref[j]) + over_ref[rows, :]
            h = h_ref[rows, :]
            ple = jax.nn.sigmoid(_dot(xb_ref[rows, :], pg_ref[...])) * _dot(_bf16(p_ref[rows, :]), pp_ref[...])
            o_ref[rows, :] = _layer_norm(alpha * h + ffn + ple, g_ref[...], b_ref[...])


def _moe(h2, slot, dest, gate, p2, wg, wu, wd, layer, pg, pp, g, b, alpha, tm=MOE_TILE):
    T = h2.shape[0]
    n_sub = tm // MOE_SUB
    n_tiles = T // tm
    per = MOE_EXPERTS_PER_STEP
    count = jnp.sum((slot >= 0).reshape(N_EXPERTS, n_tiles, n_sub, MOE_SUB), axis=3)
    npass = jnp.maximum(1, (jnp.max(count, axis=2) + MOE_SLOTS - 1) // MOE_SLOTS).T.reshape(-1).astype(jnp.int32)
    full = lambda a: pl.BlockSpec(a.shape, lambda i, e, *_: (0, 0))
    experts = lambda a: pl.BlockSpec((None, per) + a.shape[2:], lambda i, e, *_: (layer, e, 0, 0))
    grid_spec = pltpu.PrefetchScalarGridSpec(
        num_scalar_prefetch=1,
        grid=(n_tiles, N_EXPERTS // per),
        in_specs=[pl.BlockSpec((tm, D_MODEL), lambda i, e, *_: (i, 0)),
                  pl.BlockSpec((N_EXPERTS, tm), lambda i, e, *_: (0, i)),
                  pl.BlockSpec((tm, N_EXPERTS), lambda i, e, *_: (i, 0)),
                  pl.BlockSpec((tm, 2), lambda i, e, *_: (i, 0)),
                  pl.BlockSpec((N_EXPERTS, tm), lambda i, e, *_: (0, i)),
                  pl.BlockSpec((tm, PLE_DIM), lambda i, e, *_: (i, 0)),
                  experts(wg), experts(wu), experts(wd),
                  full(pg), full(pp), full(g), full(b)],
        out_specs=pl.BlockSpec((tm, D_MODEL), lambda i, e, *_: (i, 0)),
        scratch_shapes=[pltpu.VMEM((tm, D_MODEL), jnp.float32),
                        pltpu.VMEM((tm, D_MODEL), jnp.bfloat16),
                        pltpu.VMEM((n_sub, N_EXPERTS * MOE_SLOTS, D_MODEL), jnp.bfloat16)],
    )
    return pl.pallas_call(
        functools.partial(_moe_kernel, alpha=alpha, n_sub=n_sub),
        grid_spec=grid_spec,
        out_shape=jax.ShapeDtypeStruct((T, D_MODEL), jnp.float32),
        compiler_params=_cparams(("parallel", "arbitrary")),
        name="moe_ple_ln",
    )(npass, h2, slot, slot.T, dest.T, gate, p2, wg, wu, wd, pg, pp, g, b)


def _rotate_half_cols(w):
    half = w.shape[1] // 2
    return jnp.concatenate([-w[:, half:], w[:, :half]], axis=1)


def _in_proj_weights(w_in):
    z = lambda n: jnp.zeros((D_MODEL, n), w_in.dtype)
    cq = w_in[:, :MLA_Q_RANK]
    ckv = w_in[:, MLA_Q_RANK:MLA_Q_RANK + MLA_KV_RANK]
    kr = w_in[:, MLA_Q_RANK + MLA_KV_RANK:MLA_IN]
    n_dq = DIFF_HEADS * 2 * DIFF_QK
    sb0 = MLA_IN + DIFF_IN
    n_sq = SB_HEADS * SB_D
    cols = [ckv,
            z(MLA_NOPE), kr, z(LANES - MLA_NOPE - MLA_ROPE),
            z(MLA_NOPE), _rotate_half_cols(kr), z(LANES - MLA_NOPE - MLA_ROPE),
            cq, z(UA_CQ_W - MLA_Q_RANK),
            w_in[:, MLA_IN:MLA_IN + n_dq] * (DIFF_QK ** -0.5 * LOG2E),
            w_in[:, MLA_IN + n_dq:sb0],
            w_in[:, sb0:sb0 + n_sq] * (SB_D ** -0.5),
            w_in[:, sb0 + n_sq:]]
    return _bf16(jnp.concatenate(cols, axis=1))


def _mla_up_weights(w_uq, w_ukv):
    dq = MLA_NOPE + MLA_ROPE
    zq = lambda n: jnp.zeros((MLA_Q_RANK, n), w_uq.dtype)
    plain, rot = [], []
    for h in range(MLA_HEADS):
        wh = w_uq[:, h * dq:(h + 1) * dq]
        plain += [wh, zq(LANES - dq)]
        rot += [zq(MLA_NOPE), _rotate_half_cols(wh[:, MLA_NOPE:]), zq(LANES - dq)]
    wq = jnp.concatenate(plain + rot, axis=1)
    wq = jnp.concatenate([wq, jnp.zeros((UA_CQ_W - MLA_Q_RANK, wq.shape[1]), wq.dtype)], axis=0)
    dkv = MLA_NOPE + MLA_V
    zk = lambda n: jnp.zeros((MLA_KV_RANK, n), w_ukv.dtype)
    kcols, vcols = [], []
    for h in range(MLA_HEADS):
        wh = w_ukv[:, h * dkv:(h + 1) * dkv]
        kcols += [wh[:, :MLA_NOPE], zk(LANES - MLA_NOPE)]
        vcols += [wh[:, MLA_NOPE:]]
    wkv = jnp.concatenate(kcols + vcols, axis=1)
    return _bf16(wq), _bf16(wkv)


def _attn_tiles(S):
    pick = lambda want: max(c for c in (128, 256, 512, 1024) if c <= want and S % c == 0)
    return pick(512), pick(512), pick(256)


def kernel(x, p, positions, w_in, mla_q_norm, mla_w_uq, mla_kv_norm, mla_w_ukv, diff_lambda_q1, diff_lambda_k1, diff_lambda_q2, diff_lambda_k2, diff_subln, rel_bias, w_o, ln1_g, ln1_b, router_w, router_b, w_gate, w_up, w_down, ple_proj, ple_gate, ln2_g, ln2_b):
    B, S, _ = x.shape
    depth = w_in.shape[0]
    T = B * S
    alpha = (2 * depth) ** 0.25
    t_mla, t_diff, t_sb = _attn_tiles(S)
    nq = S // t_diff

    pos_col = positions.reshape(T, 1)
    pos_tiles = positions.reshape(B * nq, 1, t_diff)
    tile_pos = positions.reshape(B * nq, t_diff)
    qmin = jnp.min(tile_pos, axis=1)
    kmax = jnp.max(tile_pos, axis=1)
    first = tile_pos[:, 0]
    consec = jnp.all(tile_pos[:, 1:] - tile_pos[:, :-1] == 1, axis=1).astype(jnp.int32)
    rb_flat = rel_bias.T.reshape(-1).astype(jnp.float32)

    half = MLA_ROPE // 2
    inv = ROPE_THETA ** (-jnp.arange(half, dtype=jnp.float32) / half)
    cos_t, sin_t = _rope_table(positions.reshape(1, T), jnp.concatenate([inv, inv]).reshape(MLA_ROPE, 1))
    pad_l = lambda v: jnp.full((T, MLA_NOPE), v, jnp.float32)
    pad_r = jnp.zeros((T, LANES - MLA_NOPE - MLA_ROPE), jnp.float32)
    cosf = jnp.concatenate([pad_l(1.0), cos_t.T, pad_r], axis=1)
    sinf = jnp.concatenate([pad_l(0.0), sin_t.T, pad_r], axis=1)

    rw_pad = jnp.pad(router_w.astype(jnp.float32), ((0, 0), (0, LANES - N_EXPERTS)))
    rb_col = router_b.reshape(N_EXPERTS, 1).astype(jnp.float32)
    row = lambda a: a.reshape(1, -1)

    wg_all, wu_all, wd_all = _bf16(w_gate), _bf16(w_up), _bf16(w_down)

    h = x.reshape(T, D_MODEL)
    for i in range(depth):
        lam_init = 0.8 - 0.6 * math.exp(-0.3 * i)
        wq, wkv = _mla_up_weights(mla_w_uq[i], mla_w_ukv[i])
        gq = jnp.pad(mla_q_norm[i], (0, UA_CQ_W - MLA_Q_RANK)).reshape(1, UA_CQ_W)
        ub, q_m, k_m, v_m = _in_proj(h, _in_proj_weights(w_in[i]), cosf, sinf, gq, row(mla_kv_norm[i]), wq, wkv)
        y_mla = _mla_attn(q_m, k_m, v_m, B, S, t=t_mla)
        y_diff = _diff_attn(ub, pos_col, pos_tiles, qmin, kmax, first, consec, rb_flat,
                            row(diff_lambda_q1[i]), row(diff_lambda_k1[i]),
                            row(diff_lambda_q2[i]), row(diff_lambda_k2[i]), row(diff_subln[i]),
                            B, S, lam_init, t=t_diff)
        y_sb = _sb_attn(ub, B, S, t=t_sb)
        wo = _bf16(w_o[i])
        n_m, n_d = MLA_HEADS * MLA_V, DIFF_HEADS * DIFF_V
        h = _out_proj(h, y_mla, y_diff, y_sb, wo[:n_m], wo[n_m:n_m + n_d], wo[n_m + n_d:],
                      row(ln1_g[i]), row(ln1_b[i]), alpha)
        gate, slot, dest = _router(h, rw_pad, rb_col)
        h = _moe(h, slot, dest, gate, p[i].reshape(T, PLE_DIM), wg_all, wu_all, wd_all, i,
                 _bf16(ple_gate[i]), _bf16(ple_proj[i]), row(ln2_g[i]), row(ln2_b[i]), alpha)
    return h.reshape(B, S, D_MODEL)
```

```python
import functools
import math

import jax
import jax.numpy as jnp
from jax import lax
from jax.experimental import pallas as pl
from jax.experimental.pallas import tpu as pltpu

D_MODEL = 1024
CHUNK = 64
PLE_DIM = 256
MLA_HEADS = 4
MLA_NOPE = 64
MLA_ROPE = 32
MLA_V = 64
MLA_Q_RANK = 192
MLA_KV_RANK = 128
ROPE_THETA = 10000.0
DIFF_HEADS = 4
DIFF_QK = 64
DIFF_V = 2 * DIFF_QK
SB_HEADS = 4
SB_D = 64
REL_BUCKETS = 32
REL_MAX_DIST = 128
N_EXPERTS = 16
N_GROUPS = 4
EXPERTS_PER_GROUP = N_EXPERTS // N_GROUPS
D_EXPERT = 512
MLA_IN = MLA_Q_RANK + MLA_KV_RANK + MLA_ROPE
DIFF_IN = 2 * DIFF_HEADS * 2 * DIFF_QK + DIFF_HEADS * DIFF_V
SB_IN = 3 * SB_HEADS * SB_D
EPS = 1e-5
NEG_INF = -1e30

LANES = 128
VMEM_LIMIT = 56 * 1024 * 1024

UA_CKV = 0
UA_KR = UA_CKV + MLA_KV_RANK
UA_KRS = UA_KR + LANES
UA_CQ = UA_KRS + LANES
UA_CQ_W = 2 * LANES
UA_W = UA_CQ + UA_CQ_W
UB_W = DIFF_IN + SB_IN
MLA_QK_W = MLA_HEADS * LANES
MLA_V_W = MLA_HEADS * MLA_V

T5_FAR = 91
SB_LOG_ZERO = -88.0
TOKEN_TILE = 1024
PROJ_COLS = 512
MOE_TILE = 1024
DIFF_HEADS_PER_STEP = 2
FAR_TILE_WIDTHS = (4, 2)
MOE_SUB = 512
MOE_SLOTS = 96
MOE_EXPERTS_PER_STEP = 2

_NT = (((1,), (1,)), ((), ()))
LOG2E = math.log2(math.e)


def _cparams(sem):
    return pltpu.CompilerParams(dimension_semantics=sem, vmem_limit_bytes=VMEM_LIMIT)


def _bf16(a):
    return a.astype(jnp.bfloat16)


def _dot(a, b):
    return jnp.dot(a, b, preferred_element_type=jnp.float32)


def _in_proj_kernel(x_ref, w_ref, cos_ref, sin_ref, gq_ref, gkv_ref, wq_ref, wkv_ref,
                    ub_ref, q_ref, k_ref, v_ref, ua_ref):
    x = _bf16(x_ref[...])
    step = PROJ_COLS
    for c in range(0, UA_W, step):
        e = min(c + step, UA_W)
        ua_ref[:, c:e] = _dot(x, w_ref[:, c:e])
    for c in range(0, UB_W, step):
        e = min(c + step, UB_W)
        ub_ref[:, c:e] = _bf16(_dot(x, w_ref[:, UA_W + c:UA_W + e]))
    _mla_prep_kernel(ua_ref, cos_ref, sin_ref, gq_ref, gkv_ref, wq_ref, wkv_ref, q_ref, k_ref, v_ref)


def _in_proj(h2, w_all, cosf, sinf, gq, gkv, wq, wkv, tm=TOKEN_TILE):
    T = h2.shape[0]
    full = lambda a: pl.BlockSpec(a.shape, lambda i: (0, 0))
    row = lambda w: pl.BlockSpec((tm, w), lambda i: (i, 0))
    return pl.pallas_call(
        _in_proj_kernel,
        grid=(T // tm,),
        in_specs=[row(D_MODEL), full(w_all), row(LANES), row(LANES), full(gq), full(gkv), full(wq), full(wkv)],
        out_specs=[row(UB_W), row(MLA_QK_W), row(MLA_QK_W), row(MLA_V_W)],
        out_shape=[jax.ShapeDtypeStruct((T, UB_W), jnp.bfloat16),
                   jax.ShapeDtypeStruct((T, MLA_QK_W), jnp.bfloat16),
                   jax.ShapeDtypeStruct((T, MLA_QK_W), jnp.bfloat16),
                   jax.ShapeDtypeStruct((T, MLA_V_W), jnp.bfloat16)],
        scratch_shapes=[pltpu.VMEM((tm, UA_W), jnp.float32)],
        compiler_params=_cparams(("parallel",)),
        name="in_proj_mla_prep",
    )(h2, w_all, cosf, sinf, gq, gkv, wq, wkv)


def _rope_table_kernel(pos_ref, inv_ref, cos_ref, sin_ref):
    ang = pos_ref[...].astype(jnp.float32) * inv_ref[...]
    cos_ref[...] = jnp.cos(ang)
    sin_ref[...] = jnp.sin(ang)


def _rope_table(pos_row, inv_col, tn=2048):
    T = pos_row.shape[1]
    n = inv_col.shape[0]
    tn = min(tn, T)
    return pl.pallas_call(
        _rope_table_kernel,
        grid=(T // tn,),
        in_specs=[pl.BlockSpec((1, tn), lambda i: (0, i)),
                  pl.BlockSpec((n, 1), lambda i: (0, 0))],
        out_specs=[pl.BlockSpec((n, tn), lambda i: (0, i))] * 2,
        out_shape=[jax.ShapeDtypeStruct((n, T), jnp.float32)] * 2,
        compiler_params=_cparams(("parallel",)),
        name="rope_table",
    )(pos_row, inv_col)


def _mla_prep_kernel(ua_ref, cos_ref, sin_ref, gq_ref, gkv_ref, wq_ref, wkv_ref,
                     q_ref, k_ref, v_ref):
    ckv = ua_ref[:, UA_CKV:UA_CKV + MLA_KV_RANK]
    kr = ua_ref[:, UA_KR:UA_KR + LANES]
    krs = ua_ref[:, UA_KRS:UA_KRS + LANES]
    cq = ua_ref[:, UA_CQ:UA_CQ + UA_CQ_W]
    cqn = cq * lax.rsqrt(jnp.sum(cq * cq, -1, keepdims=True) * (1.0 / MLA_Q_RANK) + EPS) * gq_ref[...]
    ckvn = ckv * lax.rsqrt(jnp.sum(ckv * ckv, -1, keepdims=True) * (1.0 / MLA_KV_RANK) + EPS) * gkv_ref[...]
    cosf = cos_ref[...]
    sinf = sin_ref[...]
    scale = (MLA_NOPE + MLA_ROPE) ** -0.5 * LOG2E
    cqb = _bf16(cqn)
    ckvb = _bf16(ckvn)
    k_rope = kr * cosf + krs * sinf
    for h in range(MLA_HEADS):
        head = slice(h * LANES, (h + 1) * LANES)
        rot = slice(MLA_QK_W + h * LANES, MLA_QK_W + (h + 1) * LANES)
        q_ref[:, head] = _bf16((_dot(cqb, wq_ref[:, head]) * cosf + _dot(cqb, wq_ref[:, rot]) * sinf) * scale)
        k_ref[:, head] = _bf16(_dot(ckvb, wkv_ref[:, head]) + k_rope)
    v_ref[...] = _bf16(_dot(ckvb, wkv_ref[:, MLA_QK_W:MLA_QK_W + MLA_V_W]))


def _chunk_mask(tq, tk):
    qc = lax.broadcasted_iota(jnp.int32, (tq, tk), 0) // CHUNK
    kc = lax.broadcasted_iota(jnp.int32, (tq, tk), 1) // CHUNK
    return kc <= qc


def _softmax_update(s, v, m_ref, l_ref, acc_ref, idx, shift=None):
    tk = s.shape[1]
    m_old = m_ref[idx]
    row_max = jnp.max(s, axis=1, keepdims=True)
    if shift is not None:
        row_max = row_max + shift
    m_new = jnp.maximum(m_old, row_max)
    alpha = jnp.exp2(m_old - m_new)
    m_sub = m_new if shift is None else m_new - shift
    p = jnp.exp2(s - jnp.concatenate([m_sub] * (tk // LANES), axis=1))
    psum = p[:, :LANES]
    for c in range(LANES, tk, LANES):
        psum = psum + p[:, c:c + LANES]
    l_ref[idx] = alpha * l_ref[idx] + psum
    acc_ref[idx] = alpha * acc_ref[idx] + _dot(_bf16(p), v)
    m_ref[idx] = m_new


def _softmax_init(m_ref, l_ref, acc_ref):
    m_ref[...] = jnp.full(m_ref.shape, NEG_INF, jnp.float32)
    l_ref[...] = jnp.zeros(l_ref.shape, jnp.float32)
    acc_ref[...] = jnp.zeros(acc_ref.shape, jnp.float32)


def _softmax_result(l_ref, acc_ref, idx):
    return acc_ref[idx] / jnp.sum(l_ref[idx], axis=1, keepdims=True)


def _mla_attn_kernel(q_ref, k_ref, v_ref, o_ref, m_ref, l_ref, acc_ref, *, t):
    qi = pl.program_id(1)
    _softmax_init(m_ref, l_ref, acc_ref)

    def tile(j, width=1):
        ks = pl.multiple_of(j * t, t)
        for hh in range(MLA_HEADS):
            q = q_ref[:, hh * LANES:(hh + 1) * LANES]
            k = k_ref[pl.ds(ks, width * t), hh * LANES:(hh + 1) * LANES]
            v = v_ref[pl.ds(ks, width * t), (hh // 2) * LANES:(hh // 2 + 1) * LANES]
            s = lax.dot_general(q, k, _NT, preferred_element_type=jnp.float32)
            _softmax_update(s, v, m_ref, l_ref, acc_ref, hh)

    def diagonal_tile():
        half = t // 2
        split = half % LANES == 0 and half % CHUNK == 0
        pieces = ((0, 0, half), (half, half, half)) if split else ((0, 0, t),)
        for hh in range(MLA_HEADS):
            cols = slice(hh * LANES, (hh + 1) * LANES)
            vcols = slice((hh // 2) * LANES, (hh // 2 + 1) * LANES)
            for r0, k0, nk in pieces:
                ks = pl.multiple_of(qi * t + k0, nk)
                s = lax.dot_general(q_ref[r0:, cols], k_ref[pl.ds(ks, nk), cols], _NT,
                                    preferred_element_type=jnp.float32)
                s = jnp.where(_chunk_mask(t - r0, nk), s, NEG_INF)
                _softmax_update(s, v_ref[pl.ds(ks, nk), vcols], m_ref, l_ref, acc_ref, (hh, slice(r0, t)))

    done = 0
    for width in FAR_TILE_WIDTHS + (1,):
        trips = (qi - done) // width

        def body(i, carry, width=width, done=done):
            tile(done + i * width, width=width)
            return carry

        lax.fori_loop(0, trips, body, 0)
        done = done + trips * width

    diagonal_tile()
    lane = lax.broadcasted_iota(jnp.int32, (t, LANES), 1)
    for pair in range(MLA_HEADS // 2):
        o_ref[:, pair * LANES:(pair + 1) * LANES] = _bf16(
            jnp.where(lane < MLA_V, _softmax_result(l_ref, acc_ref, 2 * pair),
                      _softmax_result(l_ref, acc_ref, 2 * pair + 1)))


def _mla_attn(q, k, v, B, S, t):
    nq = S // t
    wq, wv = MLA_QK_W, MLA_V_W
    return pl.pallas_call(
        functools.partial(_mla_attn_kernel, t=t),
        grid=(B, nq),
        in_specs=[pl.BlockSpec((t, wq), lambda b, i: (b * nq + i, 0)),
                  pl.BlockSpec((S, wq), lambda b, i: (b, 0)),
                  pl.BlockSpec((S, wv), lambda b, i: (b, 0))],
        out_specs=pl.BlockSpec((t, wv), lambda b, i: (b * nq + i, 0)),
        out_shape=jax.ShapeDtypeStruct((B * S, wv), jnp.bfloat16),
        scratch_shapes=[pltpu.VMEM((MLA_HEADS, t, LANES), jnp.float32)] * 3,
        compiler_params=_cparams(("parallel", "arbitrary")),
        name="mla_attn",
    )(q, k, v)


def _t5_bias(rel, rb_ref, h):
    nb = REL_BUCKETS // 2
    max_exact = nb // 2
    n = jnp.abs(rel)
    nf = jnp.maximum(n, 1).astype(jnp.float32)
    large = max_exact + (jnp.log(nf / max_exact) / math.log(REL_MAX_DIST / max_exact)
                         * (nb - max_exact)).astype(jnp.int32)
    large = jnp.minimum(large, nb - 1)
    low = jnp.where(n < max_exact, n, large)
    neg = jnp.zeros(rel.shape, jnp.float32)
    pos = jnp.zeros(rel.shape, jnp.float32)
    for j in range(nb):
        eq = low == j
        neg = jnp.where(eq, rb_ref[h * REL_BUCKETS + j], neg)
        pos = jnp.where(eq, rb_ref[h * REL_BUCKETS + nb + j], pos)
    return jnp.where(rel > 0, pos, neg)


def _diff_attn_kernel(qmin_ref, kmax_ref, first_ref, consec_ref,
                      rb_ref, q_ref, k_ref, v_ref, posq_ref, posk_ref,
                      lq1_ref, lk1_ref, lq2_ref, lk2_ref, sub_ref,
                      o_ref, m_ref, l_ref, acc_ref, bias_ref, cache_ref, *, t, nq, lam_init):
    b = pl.program_id(0)
    hp = pl.program_id(1)
    qi = pl.program_id(2)
    _softmax_init(m_ref, l_ref, acc_ref)

    @pl.when(qi == 0)
    def _():
        for n in range(4):
            cache_ref[n] = 0

    lane = lax.broadcasted_iota(jnp.int32, (t, LANES), 1)
    qs = []
    for hh in range(DIFF_HEADS_PER_STEP):
        q = q_ref[:, hh * LANES:(hh + 1) * LANES]
        zero = jnp.zeros_like(q)
        qs.append(jnp.concatenate([jnp.where(lane < DIFF_QK, q, zero), jnp.where(lane < DIFF_QK, zero, q)], axis=0))
    tq = b * nq + qi
    qmin = qmin_ref[tq]

    def toeplitz_bias(j, h):
        d0 = first_ref[b * nq + j] - first_ref[tq]
        x = lax.broadcasted_iota(jnp.int32, (1, 2 * t), 1)
        g = _t5_bias(d0 + jnp.where(x < t, x, x - 2 * t), rb_ref, h)
        g = pltpu.roll(jnp.broadcast_to(g, (t, 2 * t)), 0, 1, stride=1, stride_axis=0)
        return g[:, :t]

    def general_bias(j, h):
        rel = posk_ref[j] - posq_ref[...]
        return _t5_bias(rel, rb_ref, h)

    def masked_bias(bias, masked):
        bias = bias * LOG2E
        return jnp.where(_chunk_mask(t, t), bias, NEG_INF) if masked else bias

    half = t // 2
    split_diagonal = half % LANES == 0 and half % CHUNK == 0

    def tile(j, masked, bias_fn, width=1):
        ks = pl.multiple_of(j * t, t)
        for hh in range(DIFF_HEADS_PER_STEP):
            h = hp * DIFF_HEADS_PER_STEP + hh
            k = k_ref[pl.ds(ks, width * t), hh * LANES:(hh + 1) * LANES]
            v = v_ref[pl.ds(ks, width * t), hh * LANES:(hh + 1) * LANES]
            if masked and split_diagonal:
                bias = bias_fn(j, h, hh)
                s = lax.dot_general(qs[hh], k[:half], _NT, preferred_element_type=jnp.float32)
                s = (s.reshape(2, t, half) + bias[None, :, :half]).reshape(2 * t, half)
                _softmax_update(s, v[:half], m_ref, l_ref, acc_ref, hh)
                for r0 in (half, t + half):
                    s = lax.dot_general(qs[hh][r0:r0 + half], k[half:], _NT, preferred_element_type=jnp.float32)
                    _softmax_update(s + bias[half:, half:], v[half:], m_ref, l_ref, acc_ref,
                                    (hh, slice(r0, r0 + half)))
                continue
            s = lax.dot_general(qs[hh], k, _NT, preferred_element_type=jnp.float32)
            if bias_fn is None:
                shift = rb_ref[h * REL_BUCKETS + REL_BUCKETS // 2 - 1] * LOG2E
            else:
                shift = None
                s = (s.reshape(2, t, t) + bias_fn(j, h, hh)[None]).reshape(2 * t, t)
            _softmax_update(s, v, m_ref, l_ref, acc_ref, hh, shift)

    def near_tile(j, masked):
        consecutive = jnp.logical_and(consec_ref[tq] == 1, consec_ref[b * nq + j] == 1)
        slot = 1 if masked else 0

        @pl.when(consecutive)
        def _():
            d0 = first_ref[b * nq + j] - first_ref[tq]
            stale = jnp.logical_or(cache_ref[2 * slot + 1] != 1, cache_ref[2 * slot] != d0)

            @pl.when(stale)
            def _():
                for hh in range(DIFF_HEADS_PER_STEP):
                    h = hp * DIFF_HEADS_PER_STEP + hh
                    bias_ref[slot, hh] = masked_bias(toeplitz_bias(j, h), masked)
                cache_ref[2 * slot] = d0
                cache_ref[2 * slot + 1] = 1

            tile(j, masked, lambda j, h, hh: bias_ref[slot, hh])

        @pl.when(jnp.logical_not(consecutive))
        def _():
            tile(j, masked, lambda j, h, hh: masked_bias(general_bias(j, h), masked))

    def is_far(j):
        return kmax_ref[b * nq + jnp.minimum(j, qi)] - qmin <= -T5_FAR

    def wide_far_loop(start, width):
        def cond(j):
            ok = j + width - 1 < qi
            for d in range(width):
                ok = jnp.logical_and(ok, is_far(j + d))
            return ok

        def step(j):
            tile(j, False, None, width=width)
            return j + width

        return lax.while_loop(cond, step, start)

    def body(j, carry):
        far = is_far(j)

        @pl.when(far)
        def _():
            tile(j, False, None)

        @pl.when(jnp.logical_not(far))
        def _():
            near_tile(j, False)

        return carry

    done = 0
    for width in FAR_TILE_WIDTHS:
        done = wide_far_loop(done, width)
    lax.fori_loop(done, qi, body, 0)
    near_tile(qi, True)

    f32 = jnp.float32
    lam = (jnp.exp(jnp.sum(lq1_ref[...].astype(f32) * lk1_ref[...].astype(f32), keepdims=True))
           - jnp.exp(jnp.sum(lq2_ref[...].astype(f32) * lk2_ref[...].astype(f32), keepdims=True))
           + lam_init)
    for hh in range(DIFF_HEADS_PER_STEP):
        a = _softmax_result(l_ref, acc_ref, hh)
        o = a[:t] - lam * a[t:]
        o = o * lax.rsqrt(jnp.mean(o * o, -1, keepdims=True) + EPS) * sub_ref[...] * (1.0 - lam_init)
        o_ref[:, hh * LANES:(hh + 1) * LANES] = _bf16(o)


def _diff_attn(ub, pos_col, pos_tiles, qmin, kmax, first, consec, rb_flat, lq1, lk1, lq2, lk2, subln,
               B, S, lam_init, t):
    nq = S // t
    per = DIFF_HEADS_PER_STEP
    G = DIFF_HEADS // per
    wg = per * DIFF_V
    small = lambda a: pl.BlockSpec(a.shape, lambda b, h, i, *_: (0, 0))
    grid_spec = pltpu.PrefetchScalarGridSpec(
        num_scalar_prefetch=4,
        grid=(B, G, nq),
        in_specs=[pl.BlockSpec(memory_space=pltpu.SMEM),
                  pl.BlockSpec((t, wg), lambda b, h, i, *_: (b * nq + i, h)),
                  pl.BlockSpec((S, wg), lambda b, h, i, *_: (b, G + h)),
                  pl.BlockSpec((S, wg), lambda b, h, i, *_: (b, 2 * G + h)),
                  pl.BlockSpec((t, 1), lambda b, h, i, *_: (b * nq + i, 0)),
                  pl.BlockSpec((nq, 1, t), lambda b, h, i, *_: (b, 0, 0)),
                  small(lq1), small(lk1), small(lq2), small(lk2), small(subln)],
        out_specs=pl.BlockSpec((t, wg), lambda b, h, i, *_: (b * nq + i, h)),
        scratch_shapes=[pltpu.VMEM((per, 2 * t, LANES), jnp.float32)] * 3
                       + [pltpu.VMEM((2, per, t, t), jnp.float32), pltpu.SMEM((4,), jnp.int32)],
    )
    return pl.pallas_call(
        functools.partial(_diff_attn_kernel, t=t, nq=nq, lam_init=lam_init),
        grid_spec=grid_spec,
        out_shape=jax.ShapeDtypeStruct((B * S, DIFF_HEADS * DIFF_V), jnp.bfloat16),
        compiler_params=_cparams(("parallel", "parallel", "arbitrary")),
        name="diff_attn",
    )(qmin, kmax, first, consec, rb_flat, ub, ub, ub, pos_col, pos_tiles, lq1, lk1, lq2, lk2, subln)


def _sb_attn_kernel(q_ref, k_ref, v_ref, o_ref, c_ref, acc_ref, *, t):
    qi = pl.program_id(1)
    n_pairs = SB_HEADS // 2
    row = lax.broadcasted_iota(jnp.int32, (t, t), 0)
    col = lax.broadcasted_iota(jnp.int32, (t, t), 1)
    tri = jnp.where(row > col, 1.0, 0.0).astype(jnp.bfloat16)
    strict = (col < row)[None]
    lane = lax.broadcasted_iota(jnp.int32, (t, LANES), 1)
    qs = []
    for hp in range(n_pairs):
        q = q_ref[:, hp * LANES:(hp + 1) * LANES]
        zero = jnp.zeros_like(q)
        qs.append(jnp.concatenate([jnp.where(lane < SB_D, q, zero), jnp.where(lane < SB_D, zero, q)], axis=0))
    c_ref[...] = jnp.zeros(c_ref.shape, jnp.float32)
    acc_ref[...] = jnp.zeros(acc_ref.shape, jnp.float32)

    def tile(j, diag):
        ks = pl.multiple_of(j * t, t)
        for hp in range(n_pairs):
            k = k_ref[pl.ds(ks, t), hp * LANES:(hp + 1) * LANES]
            v = v_ref[pl.ds(ks, t), hp * LANES:(hp + 1) * LANES]
            z = lax.dot_general(qs[hp], k, _NT, preferred_element_type=jnp.float32)
            lf = -(jnp.maximum(z, 0.0) + jnp.log(1.0 + jnp.exp(-jnp.abs(z))))
            if diag:
                lf = jnp.where(strict, lf.reshape(2, t, t), 0.0).reshape(2 * t, t)
            hi = _bf16(lf)
            lo = _bf16(lf - hi.astype(jnp.float32))
            c = c_ref[hp]
            later = _dot(hi, tri) + _dot(lo, tri) + jnp.concatenate([c] * (t // LANES), axis=1)
            w = jnp.exp(lf + z + later)
            if diag:
                w = jnp.where(strict, w.reshape(2, t, t), 0.0).reshape(2 * t, t)
            acc_ref[hp] += _dot(_bf16(w), v)
            c_ref[hp] = c + jnp.sum(lf, axis=1, keepdims=True)

    tile(qi, True)

    def cond(carry):
        j, cmax = carry
        return jnp.logical_and(j >= 0, cmax > SB_LOG_ZERO)

    def body(carry):
        j, _ = carry
        tile(j, False)
        return j - 1, jnp.max(c_ref[...])

    lax.while_loop(cond, body, (qi - 1, jnp.max(c_ref[...])))
    for hp in range(n_pairs):
        acc = acc_ref[hp]
        o_ref[:, hp * LANES:(hp + 1) * LANES] = _bf16(jnp.where(lane < SB_D, acc[:t], acc[t:]))


def _sb_attn(ub, B, S, t):
    nq = S // t
    w = SB_HEADS * SB_D
    c0 = DIFF_IN // w
    return pl.pallas_call(
        functools.partial(_sb_attn_kernel, t=t),
        grid=(B, nq),
        in_specs=[pl.BlockSpec((t, w), lambda b, i: (b * nq + i, c0)),
                  pl.BlockSpec((S, w), lambda b, i: (b, c0 + 1)),
                  pl.BlockSpec((S, w), lambda b, i: (b, c0 + 2))],
        out_specs=pl.BlockSpec((t, w), lambda b, i: (b * nq + i, 0)),
        out_shape=jax.ShapeDtypeStruct((B * S, w), jnp.bfloat16),
        scratch_shapes=[pltpu.VMEM((SB_HEADS // 2, 2 * t, LANES), jnp.float32)] * 2,
        compiler_params=_cparams(("parallel", "arbitrary")),
        name="sb_attn",
    )(ub, ub, ub)


def _layer_norm(x, g, b):
    mu = jnp.mean(x, -1, keepdims=True)
    xc = x - mu
    var = jnp.mean(xc * xc, -1, keepdims=True)
    return xc * lax.rsqrt(var + EPS) * g + b


def _out_proj_kernel(h_ref, ym_ref, yd_ref, ys_ref, wm_ref, wd_ref, ws_ref, g_ref, b_ref, o_ref, *, alpha):
    mix = _dot(ym_ref[...], wm_ref[...]) + _dot(yd_ref[...], wd_ref[...]) + _dot(ys_ref[...], ws_ref[...])
    o_ref[...] = _layer_norm(alpha * h_ref[...] + mix, g_ref[...], b_ref[...])


def _out_proj(h2, ym, yd, ys, wm, wd, ws, g, b, alpha, tm=TOKEN_TILE):
    T = h2.shape[0]
    row = lambda a: pl.BlockSpec((tm, a.shape[1]), lambda i: (i, 0))
    full = lambda a: pl.BlockSpec(a.shape, lambda i: (0, 0))
    return pl.pallas_call(
        functools.partial(_out_proj_kernel, alpha=alpha),
        grid=(T // tm,),
        in_specs=[row(h2), row(ym), row(yd), row(ys), full(wm), full(wd), full(ws), full(g), full(b)],
        out_specs=pl.BlockSpec((tm, D_MODEL), lambda i: (i, 0)),
        out_shape=jax.ShapeDtypeStruct((T, D_MODEL), jnp.float32),
        compiler_params=_cparams(("parallel",)),
        name="out_proj_ln",
    )(h2, ym, yd, ys, wm, wd, ws, g, b)


def _first_max(vals):
    m = vals[0]
    for v in vals[1:]:
        m = jnp.maximum(m, v)
    taken = jnp.zeros(m.shape, jnp.bool_)
    hot = []
    for v in vals:
        is_first = jnp.logical_and(v == m, jnp.logical_not(taken))
        hot.append(is_first)
        taken = jnp.logical_or(taken, is_first)
    return m, hot


def _router_kernel(h_ref, rw_ref, rb_ref, gate_ref, slot_ref, dest_ref, chosen_ref, first_ref):
    h = h_ref[...]
    rw = rw_ref[...]
    h_hi, rw_hi = _bf16(h), _bf16(rw)
    h_lo, rw_lo = _bf16(h - h_hi.astype(jnp.float32)), _bf16(rw - rw_hi.astype(jnp.float32))
    logits = (_dot(h_hi, rw_hi) + (_dot(h_hi, rw_lo) + _dot(h_lo, rw_hi))).T[:N_EXPERTS]
    scores = jax.nn.sigmoid(logits)
    sel = scores + rb_ref[...]
    ninf = -jnp.inf
    group_score, first, second = [], [], []
    for g in range(N_GROUPS):
        vals = [sel[g * EXPERTS_PER_GROUP + k:g * EXPERTS_PER_GROUP + k + 1, :] for k in range(EXPERTS_PER_GROUP)]
        m1, hot1 = _first_max(vals)
        m2, hot2 = _first_max([jnp.where(hh, ninf, v) for hh, v in zip(hot1, vals)])
        group_score.append(m1 + m2)
        first.append(hot1)
        second.append(hot2)
    _, best = _first_max(group_score)
    w1 = jnp.zeros_like(group_score[0])
    w2 = jnp.zeros_like(group_score[0])
    for g in range(N_GROUPS):
        for k in range(EXPERTS_PER_GROUP):
            e = g * EXPERTS_PER_GROUP + k
            sc = scores[e:e + 1, :]
            w1 = jnp.where(jnp.logical_and(best[g], first[g][k]), sc, w1)
            w2 = jnp.where(jnp.logical_and(best[g], second[g][k]), sc, w2)
    tot = w1 + w2
    for g in range(N_GROUPS):
        for k in range(EXPERTS_PER_GROUP):
            e = g * EXPERTS_PER_GROUP + k
            is1 = jnp.logical_and(best[g], first[g][k])
            is2 = jnp.logical_and(best[g], second[g][k])
            gate_ref[e:e + 1, :] = jnp.where(is1, w1 / tot, 0.0) + jnp.where(is2, w2 / tot, 0.0)
            chosen_ref[e:e + 1, :] = jnp.where(jnp.logical_or(is1, is2), 1.0, 0.0)
            first_ref[e:e + 1, :] = jnp.where(is1, 1.0, 0.0)
    chosen = chosen_ref[...]
    tm = chosen.shape[1]
    before = (lax.broadcasted_iota(jnp.int32, (tm, tm), 0) < lax.broadcasted_iota(jnp.int32, (tm, tm), 1))
    rank = _dot(_bf16(chosen), jnp.where(before, 1.0, 0.0).astype(jnp.bfloat16)).astype(jnp.int32)
    slot_ref[...] = jnp.where(chosen > 0.0, rank, -1)
    expert = lax.broadcasted_iota(jnp.int32, chosen.shape, 0)
    parked = jnp.logical_and(chosen > 0.0, rank < MOE_SLOTS)
    code = jnp.where(parked, expert * MOE_SLOTS + rank + 1, 0)
    is_first = first_ref[...] > 0.0
    dest_ref[0:1, :] = jnp.sum(jnp.where(is_first, code, 0), axis=0, keepdims=True) - 1
    dest_ref[1:2, :] = jnp.sum(jnp.where(is_first, 0, code), axis=0, keepdims=True) - 1


def _router(h2, rw_pad, rb_col):
    T = h2.shape[0]
    tm = MOE_SUB
    tok = lambda rows: pl.BlockSpec((rows, tm), lambda i: (0, i))
    return pl.pallas_call(
        _router_kernel,
        grid=(T // tm,),
        in_specs=[pl.BlockSpec((tm, D_MODEL), lambda i: (i, 0)),
                  pl.BlockSpec(rw_pad.shape, lambda i: (0, 0)),
                  pl.BlockSpec(rb_col.shape, lambda i: (0, 0))],
        out_specs=[tok(N_EXPERTS), tok(N_EXPERTS), tok(2)],
        out_shape=[jax.ShapeDtypeStruct((N_EXPERTS, T), jnp.float32),
                   jax.ShapeDtypeStruct((N_EXPERTS, T), jnp.int32),
                   jax.ShapeDtypeStruct((2, T), jnp.int32)],
        scratch_shapes=[pltpu.VMEM((N_EXPERTS, tm), jnp.float32)] * 2,
        compiler_params=_cparams(("parallel",)),
        name="router",
    )(h2, rw_pad, rb_col)


def _moe_kernel(npass_ref, h_ref, slot_ref, slot_t_ref, dest_t_ref, gate_ref, p_ref, wg_ref, wu_ref, wd_ref,
                pg_ref, pp_ref, g_ref, b_ref, o_ref, over_ref, xb_ref, y_ref, *, alpha, n_sub):
    i = pl.program_id(0)
    step = pl.program_id(1)
    R, SUB = MOE_SLOTS, MOE_SUB

    @pl.when(step == 0)
    def _():
        xb_ref[...] = _bf16(h_ref[...])
        over_ref[...] = jnp.zeros(over_ref.shape, jnp.float32)

    def expert_pass(e, k, c):
        slot_row = slot_ref[pl.ds(e, 1), :]
        gate_row = gate_ref[pl.ds(e, 1), :]
        xs, gs = [], []
        for j in range(n_sub):
            sl = slot_row[:, j * SUB:(j + 1) * SUB] - c * R
            hit = lax.broadcasted_iota(jnp.int32, (R, SUB), 0) == sl
            onehot = jnp.where(hit, 1.0, 0.0).astype(jnp.bfloat16)
            xs.append(_bf16(_dot(onehot, xb_ref[j * SUB:(j + 1) * SUB, :])))
            gs.append(jnp.sum(jnp.where(hit, gate_row[:, j * SUB:(j + 1) * SUB], 0.0), axis=1, keepdims=True))
        xe = jnp.concatenate(xs, axis=0)
        hid = jax.nn.silu(_dot(xe, wg_ref[k])) * _dot(xe, wu_ref[k])
        y = _dot(_bf16(hid), wd_ref[k])
        return [_bf16(y[j * R:(j + 1) * R] * gs[j]) for j in range(n_sub)]

    for k in range(MOE_EXPERTS_PER_STEP):
        e = step * MOE_EXPERTS_PER_STEP + k
        ys = expert_pass(e, k, 0)
        for j in range(n_sub):
            y_ref[j, pl.ds(pl.multiple_of(e * R, 16), R), :] = ys[j]

    for k in range(MOE_EXPERTS_PER_STEP):
        e = step * MOE_EXPERTS_PER_STEP + k

        def overflow_pass(c, carry, e=e, k=k):
            ys = expert_pass(e, k, c)
            lane16 = lax.broadcasted_iota(jnp.int32, (SUB, N_EXPERTS), 1)
            for j in range(n_sub):
                rows = slice(j * SUB, (j + 1) * SUB)
                slot_col = jnp.sum(jnp.where(lane16 == e, slot_t_ref[rows, :], 0), axis=1, keepdims=True)
                hit_t = lax.broadcasted_iota(jnp.int32, (SUB, R), 1) == slot_col - c * R
                over_ref[rows, :] += _dot(jnp.where(hit_t, 1.0, 0.0).astype(jnp.bfloat16), ys[j])
            return carry

        lax.fori_loop(1, npass_ref[i * N_EXPERTS + e], overflow_pass, 0)

    @pl.when(step == N_EXPERTS // MOE_EXPERTS_PER_STEP - 1)
    def _():
        lane = lax.broadcasted_iota(jnp.int32, (SUB, N_EXPERTS * R), 1)
        for j in range(n_sub):
            rows = slice(j * SUB, (j + 1) * SUB)
            dest = dest_t_ref[rows, :]
            hit = jnp.logical_or(lane == dest[:, 0:1], lane == dest[:, 1:2])
            scatter = jnp.where(hit, 1.0, 0.0).astype(jnp.bfloat16)
            ffn = _dot(scatter, y_ref[j]) + over_ref[rows, :]
            h = h_ref[rows, :]
            ple = jax.nn.sigmoid(_dot(xb_ref[rows, :], pg_ref[...])) * _dot(_bf16(p_ref[rows, :]), pp_ref[...])
            o_ref[rows, :] = _layer_norm(alpha * h + ffn + ple, g_ref[...], b_ref[...])


def _moe(h2, slot, dest, gate, p2, wg, wu, wd, layer, pg, pp, g, b, alpha, tm=MOE_TILE):
    T = h2.shape[0]
    n_sub = tm // MOE_SUB
    n_tiles = T // tm
    per = MOE_EXPERTS_PER_STEP
    count = jnp.sum((slot >= 0).reshape(N_EXPERTS, n_tiles, n_sub, MOE_SUB), axis=3)
    npass = jnp.maximum(1, (jnp.max(count, axis=2) + MOE_SLOTS - 1) // MOE_SLOTS).T.reshape(-1).astype(jnp.int32)
    full = lambda a: pl.BlockSpec(a.shape, lambda i, e, *_: (0, 0))
    experts = lambda a: pl.BlockSpec((None, per) + a.shape[2:], lambda i, e, *_: (layer, e, 0, 0))
    grid_spec = pltpu.PrefetchScalarGridSpec(
        num_scalar_prefetch=1,
        grid=(n_tiles, N_EXPERTS // per),
        in_specs=[pl.BlockSpec((tm, D_MODEL), lambda i, e, *_: (i, 0)),
                  pl.BlockSpec((N_EXPERTS, tm), lambda i, e, *_: (0, i)),
                  pl.BlockSpec((tm, N_EXPERTS), lambda i, e, *_: (i, 0)),
                  pl.BlockSpec((tm, 2), lambda i, e, *_: (i, 0)),
                  pl.BlockSpec((N_EXPERTS, tm), lambda i, e, *_: (0, i)),
                  pl.BlockSpec((tm, PLE_DIM), lambda i, e, *_: (i, 0)),
                  experts(wg), experts(wu), experts(wd),
                  full(pg), full(pp), full(g), full(b)],
        out_specs=pl.BlockSpec((tm, D_MODEL), lambda i, e, *_: (i, 0)),
        scratch_shapes=[pltpu.VMEM((tm, D_MODEL), jnp.float32),
                        pltpu.VMEM((tm, D_MODEL), jnp.bfloat16),
                        pltpu.VMEM((n_sub, N_EXPERTS * MOE_SLOTS, D_MODEL), jnp.bfloat16)],
    )
    return pl.pallas_call(
        functools.partial(_moe_kernel, alpha=alpha, n_sub=n_sub),
        grid_spec=grid_spec,
        out_shape=jax.ShapeDtypeStruct((T, D_MODEL), jnp.float32),
        compiler_params=_cparams(("parallel", "arbitrary")),
        name="moe_ple_ln",
    )(npass, h2, slot, slot.T, dest.T, gate, p2, wg, wu, wd, pg, pp, g, b)


def _rotate_half_cols(w):
    half = w.shape[1] // 2
    return jnp.concatenate([-w[:, half:], w[:, :half]], axis=1)


def _in_proj_weights(w_in):
    z = lambda n: jnp.zeros((D_MODEL, n), w_in.dtype)
    cq = w_in[:, :MLA_Q_RANK]
    ckv = w_in[:, MLA_Q_RANK:MLA_Q_RANK + MLA_KV_RANK]
    kr = w_in[:, MLA_Q_RANK + MLA_KV_RANK:MLA_IN]
    n_dq = DIFF_HEADS * 2 * DIFF_QK
    sb0 = MLA_IN + DIFF_IN
    n_sq = SB_HEADS * SB_D
    cols = [ckv,
            z(MLA_NOPE), kr, z(LANES - MLA_NOPE - MLA_ROPE),
            z(MLA_NOPE), _rotate_half_cols(kr), z(LANES - MLA_NOPE - MLA_ROPE),
            cq, z(UA_CQ_W - MLA_Q_RANK),
            w_in[:, MLA_IN:MLA_IN + n_dq] * (DIFF_QK ** -0.5 * LOG2E),
            w_in[:, MLA_IN + n_dq:sb0],
            w_in[:, sb0:sb0 + n_sq] * (SB_D ** -0.5),
            w_in[:, sb0 + n_sq:]]
    return _bf16(jnp.concatenate(cols, axis=1))


def _mla_up_weights(w_uq, w_ukv):
    dq = MLA_NOPE + MLA_ROPE
    zq = lambda n: jnp.zeros((MLA_Q_RANK, n), w_uq.dtype)
    plain, rot = [], []
    for h in range(MLA_HEADS):
        wh = w_uq[:, h * dq:(h + 1) * dq]
        plain += [wh, zq(LANES - dq)]
        rot += [zq(MLA_NOPE), _rotate_half_cols(wh[:, MLA_NOPE:]), zq(LANES - dq)]
    wq = jnp.concatenate(plain + rot, axis=1)
    wq = jnp.concatenate([wq, jnp.zeros((UA_CQ_W - MLA_Q_RANK, wq.shape[1]), wq.dtype)], axis=0)
    dkv = MLA_NOPE + MLA_V
    zk = lambda n: jnp.zeros((MLA_KV_RANK, n), w_ukv.dtype)
    kcols, vcols = [], []
    for h in range(MLA_HEADS):
        wh = w_ukv[:, h * dkv:(h + 1) * dkv]
        kcols += [wh[:, :MLA_NOPE], zk(LANES - MLA_NOPE)]
        vcols += [wh[:, MLA_NOPE:]]
    wkv = jnp.concatenate(kcols + vcols, axis=1)
    return _bf16(wq), _bf16(wkv)


def _attn_tiles(S):
    pick = lambda want: max(c for c in (128, 256, 512, 1024) if c <= want and S % c == 0)
    return pick(512), pick(512), pick(256)


def kernel(x, p, positions, w_in, mla_q_norm, mla_w_uq, mla_kv_norm, mla_w_ukv, diff_lambda_q1, diff_lambda_k1, diff_lambda_q2, diff_lambda_k2, diff_subln, rel_bias, w_o, ln1_g, ln1_b, router_w, router_b, w_gate, w_up, w_down, ple_proj, ple_gate, ln2_g, ln2_b):
    B, S, _ = x.shape
    depth = w_in.shape[0]
    T = B * S
    alpha = (2 * depth) ** 0.25
    t_mla, t_diff, t_sb = _attn_tiles(S)
    nq = S // t_diff

    pos_col = positions.reshape(T, 1)
    pos_tiles = positions.reshape(B * nq, 1, t_diff)
    tile_pos = positions.reshape(B * nq, t_diff)
    qmin = jnp.min(tile_pos, axis=1)
    kmax = jnp.max(tile_pos, axis=1)
    first = tile_pos[:, 0]
    consec = jnp.all(tile_pos[:, 1:] - tile_pos[:, :-1] == 1, axis=1).astype(jnp.int32)
    rb_flat = rel_bias.T.reshape(-1).astype(jnp.float32)

    half = MLA_ROPE // 2
    inv = ROPE_THETA ** (-jnp.arange(half, dtype=jnp.float32) / half)
    cos_t, sin_t = _rope_table(positions.reshape(1, T), jnp.concatenate([inv, inv]).reshape(MLA_ROPE, 1))
    pad_l = lambda v: jnp.full((T, MLA_NOPE), v, jnp.float32)
    pad_r = jnp.zeros((T, LANES - MLA_NOPE - MLA_ROPE), jnp.float32)
    cosf = jnp.concatenate([pad_l(1.0), cos_t.T, pad_r], axis=1)
    sinf = jnp.concatenate([pad_l(0.0), sin_t.T, pad_r], axis=1)

    rw_pad = jnp.pad(router_w.astype(jnp.float32), ((0, 0), (0, LANES - N_EXPERTS)))
    rb_col = router_b.reshape(N_EXPERTS, 1).astype(jnp.float32)
    row = lambda a: a.reshape(1, -1)

    wg_all, wu_all, wd_all = _bf16(w_gate), _bf16(w_up), _bf16(w_down)

    h = x.reshape(T, D_MODEL)
    for i in range(depth):
        lam_init = 0.8 - 0.6 * math.exp(-0.3 * i)
        wq, wkv = _mla_up_weights(mla_w_uq[i], mla_w_ukv[i])
        gq = jnp.pad(mla_q_norm[i], (0, UA_CQ_W - MLA_Q_RANK)).reshape(1, UA_CQ_W)
        ub, q_m, k_m, v_m = _in_proj(h, _in_proj_weights(w_in[i]), cosf, sinf, gq, row(mla_kv_norm[i]), wq, wkv)
        y_mla = _mla_attn(q_m, k_m, v_m, B, S, t=t_mla)
        y_diff = _diff_attn(ub, pos_col, pos_tiles, qmin, kmax, first, consec, rb_flat,
                            row(diff_lambda_q1[i]), row(diff_lambda_k1[i]),
                            row(diff_lambda_q2[i]), row(diff_lambda_k2[i]), row(diff_subln[i]),
                            B, S, lam_init, t=t_diff)
        y_sb = _sb_attn(ub, B, S, t=t_sb)
        wo = _bf16(w_o[i])
        n_m, n_d = MLA_HEADS * MLA_V, DIFF_HEADS * DIFF_V
        h = _out_proj(h, y_mla, y_diff, y_sb, wo[:n_m], wo[n_m:n_m + n_d], wo[n_m + n_d:],
                      row(ln1_g[i]), row(ln1_b[i]), alpha)
        gate, slot, dest = _router(h, rw_pad, rb_col)
        h = _moe(h, slot, dest, gate, p[i].reshape(T, PLE_DIM), wg_all, wu_all, wd_all, i,
                 _bf16(ple_gate[i]), _bf16(ple_proj[i]), row(ln2_g[i]), row(ln2_b[i]), alpha)
    return h.reshape(B, S, D_MODEL)
```

```python
import functools
import math

import jax
import jax.numpy as jnp
from jax import lax
from jax.experimental import pallas as pl
from jax.experimental.pallas import tpu as pltpu

D_MODEL = 1024
CHUNK = 64
PLE_DIM = 256
MLA_HEADS = 4
MLA_NOPE = 64
MLA_ROPE = 32
MLA_V = 64
MLA_Q_RANK = 192
MLA_KV_RANK = 128
ROPE_THETA = 10000.0
DIFF_HEADS = 4
DIFF_QK = 64
DIFF_V = 2 * DIFF_QK
SB_HEADS = 4
SB_D = 64
REL_BUCKETS = 32
REL_MAX_DIST = 128
N_EXPERTS = 16
N_GROUPS = 4
EXPERTS_PER_GROUP = N_EXPERTS // N_GROUPS
D_EXPERT = 512
MLA_IN = MLA_Q_RANK + MLA_KV_RANK + MLA_ROPE
DIFF_IN = 2 * DIFF_HEADS * 2 * DIFF_QK + DIFF_HEADS * DIFF_V
SB_IN = 3 * SB_HEADS * SB_D
EPS = 1e-5
NEG_INF = -1e30

LANES = 128
VMEM_LIMIT = 56 * 1024 * 1024
MOE_VMEM_LIMIT = 62 * 1024 * 1024

UA_CKV = 0
UA_KR = UA_CKV + MLA_KV_RANK
UA_KRS = UA_KR + LANES
UA_CQ = UA_KRS + LANES
UA_CQ_W = 2 * LANES
UA_W = UA_CQ + UA_CQ_W
UB_W = DIFF_IN + SB_IN
MLA_QK_W = MLA_HEADS * LANES
MLA_V_W = MLA_HEADS * MLA_V

T5_FAR = 91
SB_LOG_ZERO = -88.0
TOKEN_TILE = 1024
PROJ_COLS = 512
MOE_TILE = 1024
DIFF_HEADS_PER_STEP = 2
FAR_TILE_WIDTHS = (4, 2)
MOE_SUB = 512
MOE_SLOTS = 96
MOE_EXPERTS_PER_STEP = 4

_NT = (((1,), (1,)), ((), ()))
LOG2E = math.log2(math.e)


def _cparams(sem, vmem_limit=VMEM_LIMIT):
    return pltpu.CompilerParams(dimension_semantics=sem, vmem_limit_bytes=vmem_limit)


def _bf16(a):
    return a.astype(jnp.bfloat16)


def _dot(a, b):
    return jnp.dot(a, b, preferred_element_type=jnp.float32)


def _in_proj_kernel(x_ref, w_ref, cos_ref, sin_ref, gq_ref, gkv_ref, wq_ref, wkv_ref,
                    ub_ref, q_ref, k_ref, v_ref, ua_ref):
    x = _bf16(x_ref[...])
    step = PROJ_COLS
    for c in range(0, UA_W, step):
        e = min(c + step, UA_W)
        ua_ref[:, c:e] = _dot(x, w_ref[:, c:e])
    for c in range(0, UB_W, step):
        e = min(c + step, UB_W)
        ub_ref[:, c:e] = _bf16(_dot(x, w_ref[:, UA_W + c:UA_W + e]))
    _mla_prep_kernel(ua_ref, cos_ref, sin_ref, gq_ref, gkv_ref, wq_ref, wkv_ref, q_ref, k_ref, v_ref)


def _in_proj(h2, w_all, cosf, sinf, gq, gkv, wq, wkv, tm=TOKEN_TILE):
    T = h2.shape[0]
    full = lambda a: pl.BlockSpec(a.shape, lambda i: (0, 0))
    row = lambda w: pl.BlockSpec((tm, w), lambda i: (i, 0))
    return pl.pallas_call(
        _in_proj_kernel,
        grid=(T // tm,),
        in_specs=[row(D_MODEL), full(w_all), row(LANES), row(LANES), full(gq), full(gkv), full(wq), full(wkv)],
        out_specs=[row(UB_W), row(MLA_QK_W), row(MLA_QK_W), row(MLA_V_W)],
        out_shape=[jax.ShapeDtypeStruct((T, UB_W), jnp.bfloat16),
                   jax.ShapeDtypeStruct((T, MLA_QK_W), jnp.bfloat16),
                   jax.ShapeDtypeStruct((T, MLA_QK_W), jnp.bfloat16),
                   jax.ShapeDtypeStruct((T, MLA_V_W), jnp.bfloat16)],
        scratch_shapes=[pltpu.VMEM((tm, UA_W), jnp.float32)],
        compiler_params=_cparams(("parallel",)),
        name="in_proj_mla_prep",
    )(h2, w_all, cosf, sinf, gq, gkv, wq, wkv)


def _rope_table_kernel(pos_ref, inv_ref, cos_ref, sin_ref):
    ang = pos_ref[...].astype(jnp.float32) * inv_ref[...]
    cos_ref[...] = jnp.cos(ang)
    sin_ref[...] = jnp.sin(ang)


def _rope_table(pos_row, inv_col, tn=2048):
    T = pos_row.shape[1]
    n = inv_col.shape[0]
    tn = min(tn, T)
    return pl.pallas_call(
        _rope_table_kernel,
        grid=(T // tn,),
        in_specs=[pl.BlockSpec((1, tn), lambda i: (0, i)),
                  pl.BlockSpec((n, 1), lambda i: (0, 0))],
        out_specs=[pl.BlockSpec((n, tn), lambda i: (0, i))] * 2,
        out_shape=[jax.ShapeDtypeStruct((n, T), jnp.float32)] * 2,
        compiler_params=_cparams(("parallel",)),
        name="rope_table",
    )(pos_row, inv_col)


def _mla_prep_kernel(ua_ref, cos_ref, sin_ref, gq_ref, gkv_ref, wq_ref, wkv_ref,
                     q_ref, k_ref, v_ref):
    ckv = ua_ref[:, UA_CKV:UA_CKV + MLA_KV_RANK]
    kr = ua_ref[:, UA_KR:UA_KR + LANES]
    krs = ua_ref[:, UA_KRS:UA_KRS + LANES]
    cq = ua_ref[:, UA_CQ:UA_CQ + UA_CQ_W]
    cqn = cq * lax.rsqrt(jnp.sum(cq * cq, -1, keepdims=True) * (1.0 / MLA_Q_RANK) + EPS) * gq_ref[...]
    ckvn = ckv * lax.rsqrt(jnp.sum(ckv * ckv, -1, keepdims=True) * (1.0 / MLA_KV_RANK) + EPS) * gkv_ref[...]
    cosf = cos_ref[...]
    sinf = sin_ref[...]
    scale = (MLA_NOPE + MLA_ROPE) ** -0.5 * LOG2E
    cqb = _bf16(cqn)
    ckvb = _bf16(ckvn)
    k_rope = kr * cosf + krs * sinf
    for h in range(MLA_HEADS):
        head = slice(h * LANES, (h + 1) * LANES)
        rot = slice(MLA_QK_W + h * LANES, MLA_QK_W + (h + 1) * LANES)
        q_ref[:, head] = _bf16((_dot(cqb, wq_ref[:, head]) * cosf + _dot(cqb, wq_ref[:, rot]) * sinf) * scale)
        k_ref[:, head] = _bf16(_dot(ckvb, wkv_ref[:, head]) + k_rope)
    v_ref[...] = _bf16(_dot(ckvb, wkv_ref[:, MLA_QK_W:MLA_QK_W + MLA_V_W]))


def _chunk_mask(tq, tk):
    qc = lax.broadcasted_iota(jnp.int32, (tq, tk), 0) // CHUNK
    kc = lax.broadcasted_iota(jnp.int32, (tq, tk), 1) // CHUNK
    return kc <= qc


def _softmax_update(s, v, m_ref, l_ref, acc_ref, idx, shift=None):
    tk = s.shape[1]
    m_old = m_ref[idx]
    row_max = jnp.max(s, axis=1, keepdims=True)
    if shift is not None:
        row_max = row_max + shift
    m_new = jnp.maximum(m_old, row_max)
    alpha = jnp.exp2(m_old - m_new)
    m_sub = m_new if shift is None else m_new - shift
    p = jnp.exp2(s - jnp.concatenate([m_sub] * (tk // LANES), axis=1))
    psum = p[:, :LANES]
    for c in range(LANES, tk, LANES):
        psum = psum + p[:, c:c + LANES]
    l_ref[idx] = alpha * l_ref[idx] + psum
    acc_ref[idx] = alpha * acc_ref[idx] + _dot(_bf16(p), v)
    m_ref[idx] = m_new


def _softmax_init(m_ref, l_ref, acc_ref):
    m_ref[...] = jnp.full(m_ref.shape, NEG_INF, jnp.float32)
    l_ref[...] = jnp.zeros(l_ref.shape, jnp.float32)
    acc_ref[...] = jnp.zeros(acc_ref.shape, jnp.float32)


def _softmax_result(l_ref, acc_ref, idx):
    return acc_ref[idx] / jnp.sum(l_ref[idx], axis=1, keepdims=True)


def _mla_attn_kernel(q_ref, k_ref, v_ref, o_ref, m_ref, l_ref, acc_ref, *, t):
    qi = pl.program_id(1)
    _softmax_init(m_ref, l_ref, acc_ref)

    def tile(j, width=1):
        ks = pl.multiple_of(j * t, t)
        for hh in range(MLA_HEADS):
            q = q_ref[:, hh * LANES:(hh + 1) * LANES]
            k = k_ref[pl.ds(ks, width * t), hh * LANES:(hh + 1) * LANES]
            v = v_ref[pl.ds(ks, width * t), (hh // 2) * LANES:(hh // 2 + 1) * LANES]
            s = lax.dot_general(q, k, _NT, preferred_element_type=jnp.float32)
            _softmax_update(s, v, m_ref, l_ref, acc_ref, hh)

    def diagonal_tile():
        half = t // 2
        split = half % LANES == 0 and half % CHUNK == 0
        pieces = ((0, 0, half), (half, half, half)) if split else ((0, 0, t),)
        for hh in range(MLA_HEADS):
            cols = slice(hh * LANES, (hh + 1) * LANES)
            vcols = slice((hh // 2) * LANES, (hh // 2 + 1) * LANES)
            for r0, k0, nk in pieces:
                ks = pl.multiple_of(qi * t + k0, nk)
                s = lax.dot_general(q_ref[r0:, cols], k_ref[pl.ds(ks, nk), cols], _NT,
                                    preferred_element_type=jnp.float32)
                s = jnp.where(_chunk_mask(t - r0, nk), s, NEG_INF)
                _softmax_update(s, v_ref[pl.ds(ks, nk), vcols], m_ref, l_ref, acc_ref, (hh, slice(r0, t)))

    done = 0
    for width in FAR_TILE_WIDTHS + (1,):
        trips = (qi - done) // width

        def body(i, carry, width=width, done=done):
            tile(done + i * width, width=width)
            return carry

        lax.fori_loop(0, trips, body, 0)
        done = done + trips * width

    diagonal_tile()
    lane = lax.broadcasted_iota(jnp.int32, (t, LANES), 1)
    for pair in range(MLA_HEADS // 2):
        o_ref[:, pair * LANES:(pair + 1) * LANES] = _bf16(
            jnp.where(lane < MLA_V, _softmax_result(l_ref, acc_ref, 2 * pair),
                      _softmax_result(l_ref, acc_ref, 2 * pair + 1)))


def _mla_attn(q, k, v, B, S, t):
    nq = S // t
    wq, wv = MLA_QK_W, MLA_V_W
    return pl.pallas_call(
        functools.partial(_mla_attn_kernel, t=t),
        grid=(B, nq),
        in_specs=[pl.BlockSpec((t, wq), lambda b, i: (b * nq + i, 0)),
                  pl.BlockSpec((S, wq), lambda b, i: (b, 0)),
                  pl.BlockSpec((S, wv), lambda b, i: (b, 0))],
        out_specs=pl.BlockSpec((t, wv), lambda b, i: (b * nq + i, 0)),
        out_shape=jax.ShapeDtypeStruct((B * S, wv), jnp.bfloat16),
        scratch_shapes=[pltpu.VMEM((MLA_HEADS, t, LANES), jnp.float32)] * 3,
        compiler_params=_cparams(("parallel", "arbitrary")),
        name="mla_attn",
    )(q, k, v)


def _t5_bias(rel, rb_ref, h):
    nb = REL_BUCKETS // 2
    max_exact = nb // 2
    n = jnp.abs(rel)
    nf = jnp.maximum(n, 1).astype(jnp.float32)
    large = max_exact + (jnp.log(nf / max_exact) / math.log(REL_MAX_DIST / max_exact)
                         * (nb - max_exact)).astype(jnp.int32)
    large = jnp.minimum(large, nb - 1)
    low = jnp.where(n < max_exact, n, large)
    neg = jnp.zeros(rel.shape, jnp.float32)
    pos = jnp.zeros(rel.shape, jnp.float32)
    for j in range(nb):
        eq = low == j
        neg = jnp.where(eq, rb_ref[h * REL_BUCKETS + j], neg)
        pos = jnp.where(eq, rb_ref[h * REL_BUCKETS + nb + j], pos)
    return jnp.where(rel > 0, pos, neg)


def _diff_attn_kernel(qmin_ref, kmax_ref, first_ref, consec_ref,
                      rb_ref, q_ref, k_ref, v_ref, posq_ref, posk_ref,
                      lq1_ref, lk1_ref, lq2_ref, lk2_ref, sub_ref,
                      o_ref, m_ref, l_ref, acc_ref, bias_ref, cache_ref, *, t, nq, lam_init):
    b = pl.program_id(0)
    hp = pl.program_id(1)
    qi = pl.program_id(2)
    _softmax_init(m_ref, l_ref, acc_ref)

    @pl.when(qi == 0)
    def _():
        for n in range(4):
            cache_ref[n] = 0

    lane = lax.broadcasted_iota(jnp.int32, (t, LANES), 1)
    qs = []
    for hh in range(DIFF_HEADS_PER_STEP):
        q = q_ref[:, hh * LANES:(hh + 1) * LANES]
        zero = jnp.zeros_like(q)
        qs.append(jnp.concatenate([jnp.where(lane < DIFF_QK, q, zero), jnp.where(lane < DIFF_QK, zero, q)], axis=0))
    tq = b * nq + qi
    qmin = qmin_ref[tq]

    def toeplitz_bias(j, h):
        d0 = first_ref[b * nq + j] - first_ref[tq]
        x = lax.broadcasted_iota(jnp.int32, (1, 2 * t), 1)
        g = _t5_bias(d0 + jnp.where(x < t, x, x - 2 * t), rb_ref, h)
        g = pltpu.roll(jnp.broadcast_to(g, (t, 2 * t)), 0, 1, stride=1, stride_axis=0)
        return g[:, :t]

    def general_bias(j, h):
        rel = posk_ref[j] - posq_ref[...]
        return _t5_bias(rel, rb_ref, h)

    def masked_bias(bias, masked):
        bias = bias * LOG2E
        return jnp.where(_chunk_mask(t, t), bias, NEG_INF) if masked else bias

    half = t // 2
    split_diagonal = half % LANES == 0 and half % CHUNK == 0

    def tile(j, masked, bias_fn, width=1):
        ks = pl.multiple_of(j * t, t)
        for hh in range(DIFF_HEADS_PER_STEP):
            h = hp * DIFF_HEADS_PER_STEP + hh
            k = k_ref[pl.ds(ks, width * t), hh * LANES:(hh + 1) * LANES]
            v = v_ref[pl.ds(ks, width * t), hh * LANES:(hh + 1) * LANES]
            if masked and split_diagonal:
                bias = bias_fn(j, h, hh)
                s = lax.dot_general(qs[hh], k[:half], _NT, preferred_element_type=jnp.float32)
                s = (s.reshape(2, t, half) + bias[None, :, :half]).reshape(2 * t, half)
                _softmax_update(s, v[:half], m_ref, l_ref, acc_ref, hh)
                for r0 in (half, t + half):
                    s = lax.dot_general(qs[hh][r0:r0 + half], k[half:], _NT, preferred_element_type=jnp.float32)
                    _softmax_update(s + bias[half:, half:], v[half:], m_ref, l_ref, acc_ref,
                                    (hh, slice(r0, r0 + half)))
                continue
            s = lax.dot_general(qs[hh], k, _NT, preferred_element_type=jnp.float32)
            if bias_fn is None:
                shift = rb_ref[h * REL_BUCKETS + REL_BUCKETS // 2 - 1] * LOG2E
            else:
                shift = None
                s = (s.reshape(2, t, t) + bias_fn(j, h, hh)[None]).reshape(2 * t, t)
            _softmax_update(s, v, m_ref, l_ref, acc_ref, hh, shift)

    def near_tile(j, masked):
        consecutive = jnp.logical_and(consec_ref[tq] == 1, consec_ref[b * nq + j] == 1)
        slot = 1 if masked else 0

        @pl.when(consecutive)
        def _():
            d0 = first_ref[b * nq + j] - first_ref[tq]
            stale = jnp.logical_or(cache_ref[2 * slot + 1] != 1, cache_ref[2 * slot] != d0)

            @pl.when(stale)
            def _():
                for hh in range(DIFF_HEADS_PER_STEP):
                    h = hp * DIFF_HEADS_PER_STEP + hh
                    bias_ref[slot, hh] = masked_bias(toeplitz_bias(j, h), masked)
                cache_ref[2 * slot] = d0
                cache_ref[2 * slot + 1] = 1

            tile(j, masked, lambda j, h, hh: bias_ref[slot, hh])

        @pl.when(jnp.logical_not(consecutive))
        def _():
            tile(j, masked, lambda j, h, hh: masked_bias(general_bias(j, h), masked))

    def is_far(j):
        return kmax_ref[b * nq + jnp.minimum(j, qi)] - qmin <= -T5_FAR

    def wide_far_loop(start, width):
        def cond(j):
            ok = j + width - 1 < qi
            for d in range(width):
                ok = jnp.logical_and(ok, is_far(j + d))
            return ok

        def step(j):
            tile(j, False, None, width=width)
            return j + width

        return lax.while_loop(cond, step, start)

    def body(j, carry):
        far = is_far(j)

        @pl.when(far)
        def _():
            tile(j, False, None)

        @pl.when(jnp.logical_not(far))
        def _():
            near_tile(j, False)

        return carry

    done = 0
    for width in FAR_TILE_WIDTHS:
        done = wide_far_loop(done, width)
    lax.fori_loop(done, qi, body, 0)
    near_tile(qi, True)

    f32 = jnp.float32
    lam = (jnp.exp(jnp.sum(lq1_ref[...].astype(f32) * lk1_ref[...].astype(f32), keepdims=True))
           - jnp.exp(jnp.sum(lq2_ref[...].astype(f32) * lk2_ref[...].astype(f32), keepdims=True))
           + lam_init)
    for hh in range(DIFF_HEADS_PER_STEP):
        a = _softmax_result(l_ref, acc_ref, hh)
        o = a[:t] - lam * a[t:]
        o = o * lax.rsqrt(jnp.mean(o * o, -1, keepdims=True) + EPS) * sub_ref[...] * (1.0 - lam_init)
        o_ref[:, hh * LANES:(hh + 1) * LANES] = _bf16(o)


def _diff_attn(ub, pos_col, pos_tiles, qmin, kmax, first, consec, rb_flat, lq1, lk1, lq2, lk2, subln,
               B, S, lam_init, t):
    nq = S // t
    per = DIFF_HEADS_PER_STEP
    G = DIFF_HEADS // per
    wg = per * DIFF_V
    small = lambda a: pl.BlockSpec(a.shape, lambda b, h, i, *_: (0, 0))
    grid_spec = pltpu.PrefetchScalarGridSpec(
        num_scalar_prefetch=4,
        grid=(B, G, nq),
        in_specs=[pl.BlockSpec(memory_space=pltpu.SMEM),
                  pl.BlockSpec((t, wg), lambda b, h, i, *_: (b * nq + i, h)),
                  pl.BlockSpec((S, wg), lambda b, h, i, *_: (b, G + h)),
                  pl.BlockSpec((S, wg), lambda b, h, i, *_: (b, 2 * G + h)),
                  pl.BlockSpec((t, 1), lambda b, h, i, *_: (b * nq + i, 0)),
                  pl.BlockSpec((nq, 1, t), lambda b, h, i, *_: (b, 0, 0)),
                  small(lq1), small(lk1), small(lq2), small(lk2), small(subln)],
        out_specs=pl.BlockSpec((t, wg), lambda b, h, i, *_: (b * nq + i, h)),
        scratch_shapes=[pltpu.VMEM((per, 2 * t, LANES), jnp.float32)] * 3
                       + [pltpu.VMEM((2, per, t, t), jnp.float32), pltpu.SMEM((4,), jnp.int32)],
    )
    return pl.pallas_call(
        functools.partial(_diff_attn_kernel, t=t, nq=nq, lam_init=lam_init),
        grid_spec=grid_spec,
        out_shape=jax.ShapeDtypeStruct((B * S, DIFF_HEADS * DIFF_V), jnp.bfloat16),
        compiler_params=_cparams(("parallel", "parallel", "arbitrary")),
        name="diff_attn",
    )(qmin, kmax, first, consec, rb_flat, ub, ub, ub, pos_col, pos_tiles, lq1, lk1, lq2, lk2, subln)


def _sb_attn_kernel(q_ref, k_ref, v_ref, o_ref, c_ref, acc_ref, *, t):
    qi = pl.program_id(1)
    n_pairs = SB_HEADS // 2
    row = lax.broadcasted_iota(jnp.int32, (t, t), 0)
    col = lax.broadcasted_iota(jnp.int32, (t, t), 1)
    tri = jnp.where(row > col, 1.0, 0.0).astype(jnp.bfloat16)
    strict = (col < row)[None]
    lane = lax.broadcasted_iota(jnp.int32, (t, LANES), 1)
    qs = []
    for hp in range(n_pairs):
        q = q_ref[:, hp * LANES:(hp + 1) * LANES]
        zero = jnp.zeros_like(q)
        qs.append(jnp.concatenate([jnp.where(lane < SB_D, q, zero), jnp.where(lane < SB_D, zero, q)], axis=0))
    c_ref[...] = jnp.zeros(c_ref.shape, jnp.float32)
    acc_ref[...] = jnp.zeros(acc_ref.shape, jnp.float32)

    def tile(j, diag):
        ks = pl.multiple_of(j * t, t)
        for hp in range(n_pairs):
            k = k_ref[pl.ds(ks, t), hp * LANES:(hp + 1) * LANES]
            v = v_ref[pl.ds(ks, t), hp * LANES:(hp + 1) * LANES]
            z = lax.dot_general(qs[hp], k, _NT, preferred_element_type=jnp.float32)
            lf = -(jnp.maximum(z, 0.0) + jnp.log(1.0 + jnp.exp(-jnp.abs(z))))
            if diag:
                lf = jnp.where(strict, lf.reshape(2, t, t), 0.0).reshape(2 * t, t)
            hi = _bf16(lf)
            lo = _bf16(lf - hi.astype(jnp.float32))
            c = c_ref[hp]
            later = _dot(hi, tri) + _dot(lo, tri) + jnp.concatenate([c] * (t // LANES), axis=1)
            w = jnp.exp(lf + z + later)
            if diag:
                w = jnp.where(strict, w.reshape(2, t, t), 0.0).reshape(2 * t, t)
            acc_ref[hp] += _dot(_bf16(w), v)
            c_ref[hp] = c + jnp.sum(lf, axis=1, keepdims=True)

    tile(qi, True)

    def cond(carry):
        j, cmax = carry
        return jnp.logical_and(j >= 0, cmax > SB_LOG_ZERO)

    def body(carry):
        j, _ = carry
        tile(j, False)
        return j - 1, jnp.max(c_ref[...])

    lax.while_loop(cond, body, (qi - 1, jnp.max(c_ref[...])))
    for hp in range(n_pairs):
        acc = acc_ref[hp]
        o_ref[:, hp * LANES:(hp + 1) * LANES] = _bf16(jnp.where(lane < SB_D, acc[:t], acc[t:]))


def _sb_attn(ub, B, S, t):
    nq = S // t
    w = SB_HEADS * SB_D
    c0 = DIFF_IN // w
    return pl.pallas_call(
        functools.partial(_sb_attn_kernel, t=t),
        grid=(B, nq),
        in_specs=[pl.BlockSpec((t, w), lambda b, i: (b * nq + i, c0)),
                  pl.BlockSpec((S, w), lambda b, i: (b, c0 + 1)),
                  pl.BlockSpec((S, w), lambda b, i: (b, c0 + 2))],
        out_specs=pl.BlockSpec((t, w), lambda b, i: (b * nq + i, 0)),
        out_shape=jax.ShapeDtypeStruct((B * S, w), jnp.bfloat16),
        scratch_shapes=[pltpu.VMEM((SB_HEADS // 2, 2 * t, LANES), jnp.float32)] * 2,
        compiler_params=_cparams(("parallel", "arbitrary")),
        name="sb_attn",
    )(ub, ub, ub)


def _layer_norm(x, g, b):
    mu = jnp.mean(x, -1, keepdims=True)
    xc = x - mu
    var = jnp.mean(xc * xc, -1, keepdims=True)
    return xc * lax.rsqrt(var + EPS) * g + b


def _out_proj_kernel(h_ref, ym_ref, yd_ref, ys_ref, wm_ref, wd_ref, ws_ref, g_ref, b_ref, o_ref, *, alpha):
    mix = _dot(ym_ref[...], wm_ref[...]) + _dot(yd_ref[...], wd_ref[...]) + _dot(ys_ref[...], ws_ref[...])
    o_ref[...] = _layer_norm(alpha * h_ref[...] + mix, g_ref[...], b_ref[...])


def _out_proj(h2, ym, yd, ys, wm, wd, ws, g, b, alpha, tm=TOKEN_TILE):
    T = h2.shape[0]
    row = lambda a: pl.BlockSpec((tm, a.shape[1]), lambda i: (i, 0))
    full = lambda a: pl.BlockSpec(a.shape, lambda i: (0, 0))
    return pl.pallas_call(
        functools.partial(_out_proj_kernel, alpha=alpha),
        grid=(T // tm,),
        in_specs=[row(h2), row(ym), row(yd), row(ys), full(wm), full(wd), full(ws), full(g), full(b)],
        out_specs=pl.BlockSpec((tm, D_MODEL), lambda i: (i, 0)),
        out_shape=jax.ShapeDtypeStruct((T, D_MODEL), jnp.float32),
        compiler_params=_cparams(("parallel",)),
        name="out_proj_ln",
    )(h2, ym, yd, ys, wm, wd, ws, g, b)


def _first_max(vals):
    m = vals[0]
    for v in vals[1:]:
        m = jnp.maximum(m, v)
    taken = jnp.zeros(m.shape, jnp.bool_)
    hot = []
    for v in vals:
        is_first = jnp.logical_and(v == m, jnp.logical_not(taken))
        hot.append(is_first)
        taken = jnp.logical_or(taken, is_first)
    return m, hot


def _router_kernel(h_ref, rw_ref, rb_ref, gate_ref, slot_ref, dest_ref, chosen_ref, first_ref):
    h = h_ref[...]
    rw = rw_ref[...]
    h_hi, rw_hi = _bf16(h), _bf16(rw)
    h_lo, rw_lo = _bf16(h - h_hi.astype(jnp.float32)), _bf16(rw - rw_hi.astype(jnp.float32))
    logits = (_dot(h_hi, rw_hi) + (_dot(h_hi, rw_lo) + _dot(h_lo, rw_hi))).T[:N_EXPERTS]
    scores = jax.nn.sigmoid(logits)
    sel = scores + rb_ref[...]
    ninf = -jnp.inf
    group_score, first, second = [], [], []
    for g in range(N_GROUPS):
        vals = [sel[g * EXPERTS_PER_GROUP + k:g * EXPERTS_PER_GROUP + k + 1, :] for k in range(EXPERTS_PER_GROUP)]
        m1, hot1 = _first_max(vals)
        m2, hot2 = _first_max([jnp.where(hh, ninf, v) for hh, v in zip(hot1, vals)])
        group_score.append(m1 + m2)
        first.append(hot1)
        second.append(hot2)
    _, best = _first_max(group_score)
    w1 = jnp.zeros_like(group_score[0])
    w2 = jnp.zeros_like(group_score[0])
    for g in range(N_GROUPS):
        for k in range(EXPERTS_PER_GROUP):
            e = g * EXPERTS_PER_GROUP + k
            sc = scores[e:e + 1, :]
            w1 = jnp.where(jnp.logical_and(best[g], first[g][k]), sc, w1)
            w2 = jnp.where(jnp.logical_and(best[g], second[g][k]), sc, w2)
    tot = w1 + w2
    for g in range(N_GROUPS):
        for k in range(EXPERTS_PER_GROUP):
            e = g * EXPERTS_PER_GROUP + k
            is1 = jnp.logical_and(best[g], first[g][k])
            is2 = jnp.logical_and(best[g], second[g][k])
            gate_ref[e:e + 1, :] = jnp.where(is1, w1 / tot, 0.0) + jnp.where(is2, w2 / tot, 0.0)
            chosen_ref[e:e + 1, :] = jnp.where(jnp.logical_or(is1, is2), 1.0, 0.0)
            first_ref[e:e + 1, :] = jnp.where(is1, 1.0, 0.0)
    chosen = chosen_ref[...]
    tm = chosen.shape[1]
    before = (lax.broadcasted_iota(jnp.int32, (tm, tm), 0) < lax.broadcasted_iota(jnp.int32, (tm, tm), 1))
    rank = _dot(_bf16(chosen), jnp.where(before, 1.0, 0.0).astype(jnp.bfloat16)).astype(jnp.int32)
    slot_ref[...] = jnp.where(chosen > 0.0, rank, -1)
    expert = lax.broadcasted_iota(jnp.int32, chosen.shape, 0)
    parked = jnp.logical_and(chosen > 0.0, rank < MOE_SLOTS)
    code = jnp.where(parked, expert * MOE_SLOTS + rank + 1, 0)
    is_first = first_ref[...] > 0.0
    dest_ref[0:1, :] = jnp.sum(jnp.where(is_first, code, 0), axis=0, keepdims=True) - 1
    dest_ref[1:2, :] = jnp.sum(jnp.where(is_first, 0, code), axis=0, keepdims=True) - 1


def _router(h2, rw_pad, rb_col):
    T = h2.shape[0]
    tm = MOE_SUB
    tok = lambda rows: pl.BlockSpec((rows, tm), lambda i: (0, i))
    return pl.pallas_call(
        _router_kernel,
        grid=(T // tm,),
        in_specs=[pl.BlockSpec((tm, D_MODEL), lambda i: (i, 0)),
                  pl.BlockSpec(rw_pad.shape, lambda i: (0, 0)),
                  pl.BlockSpec(rb_col.shape, lambda i: (0, 0))],
        out_specs=[tok(N_EXPERTS), tok(N_EXPERTS), tok(2)],
        out_shape=[jax.ShapeDtypeStruct((N_EXPERTS, T), jnp.float32),
                   jax.ShapeDtypeStruct((N_EXPERTS, T), jnp.int32),
                   jax.ShapeDtypeStruct((2, T), jnp.int32)],
        scratch_shapes=[pltpu.VMEM((N_EXPERTS, tm), jnp.float32)] * 2,
        compiler_params=_cparams(("parallel",)),
        name="router",
    )(h2, rw_pad, rb_col)


def _moe_kernel(npass_ref, h_ref, slot_ref, slot_t_ref, dest_t_ref, gate_ref, p_ref, wg_ref, wu_ref, wd_ref,
                pg_ref, pp_ref, g_ref, b_ref, o_ref, xb_ref, y_ref, *, alpha, n_sub):
    i = pl.program_id(0)
    step = pl.program_id(1)
    R, SUB = MOE_SLOTS, MOE_SUB

    @pl.when(step == 0)
    def _():
        xb_ref[...] = _bf16(h_ref[...])
        o_ref[...] = jnp.zeros(o_ref.shape, jnp.float32)

    def expert_pass(e, k, c):
        slot_row = slot_ref[pl.ds(e, 1), :]
        gate_row = gate_ref[pl.ds(e, 1), :]
        xs, gs = [], []
        for j in range(n_sub):
            sl = slot_row[:, j * SUB:(j + 1) * SUB] - c * R
            hit = lax.broadcasted_iota(jnp.int32, (R, SUB), 0) == sl
            onehot = jnp.where(hit, 1.0, 0.0).astype(jnp.bfloat16)
            xs.append(_bf16(_dot(onehot, xb_ref[j * SUB:(j + 1) * SUB, :])))
            gs.append(jnp.sum(jnp.where(hit, gate_row[:, j * SUB:(j + 1) * SUB], 0.0), axis=1, keepdims=True))
        xe = jnp.concatenate(xs, axis=0)
        hid = jax.nn.silu(_dot(xe, wg_ref[k])) * _dot(xe, wu_ref[k])
        y = _dot(_bf16(hid), wd_ref[k])
        return [_bf16(y[j * R:(j + 1) * R] * gs[j]) for j in range(n_sub)]

    for k in range(MOE_EXPERTS_PER_STEP):
        e = step * MOE_EXPERTS_PER_STEP + k
        ys = expert_pass(e, k, 0)
        for j in range(n_sub):
            y_ref[j, pl.ds(pl.multiple_of(e * R, 16), R), :] = ys[j]

    for k in range(MOE_EXPERTS_PER_STEP):
        e = step * MOE_EXPERTS_PER_STEP + k

        def overflow_pass(c, carry, e=e, k=k):
            ys = expert_pass(e, k, c)
            lane16 = lax.broadcasted_iota(jnp.int32, (SUB, N_EXPERTS), 1)
            for j in range(n_sub):
                rows = slice(j * SUB, (j + 1) * SUB)
                slot_col = jnp.sum(jnp.where(lane16 == e, slot_t_ref[rows, :], 0), axis=1, keepdims=True)
                hit_t = lax.broadcasted_iota(jnp.int32, (SUB, R), 1) == slot_col - c * R
                o_ref[rows, :] += _dot(jnp.where(hit_t, 1.0, 0.0).astype(jnp.bfloat16), ys[j])
            return carry

        lax.fori_loop(1, npass_ref[i * N_EXPERTS + e], overflow_pass, 0)

    @pl.when(step == N_EXPERTS // MOE_EXPERTS_PER_STEP - 1)
    def _():
        lane = lax.broadcasted_iota(jnp.int32, (SUB, N_EXPERTS * R), 1)
        for j in range(n_sub):
            rows = slice(j * SUB, (j + 1) * SUB)
            dest = dest_t_ref[rows, :]
            hit = jnp.logical_or(lane == dest[:, 0:1], lane == dest[:, 1:2])
            scatter = jnp.where(hit, 1.0, 0.0).astype(jnp.bfloat16)
            ffn = _dot(scatter, y_ref[j]) + o_ref[rows, :]
            h = h_ref[rows, :]
            ple = jax.nn.sigmoid(_dot(xb_ref[rows, :], pg_ref[...])) * _dot(_bf16(p_ref[rows, :]), pp_ref[...])
            o_ref[rows, :] = _layer_norm(alpha * h + ffn + ple, g_ref[...], b_ref[...])


def _moe(h2, slot, dest, gate, p2, wg, wu, wd, layer, pg, pp, g, b, alpha, tm=MOE_TILE):
    T = h2.shape[0]
    n_sub = tm // MOE_SUB
    n_tiles = T // tm
    per = MOE_EXPERTS_PER_STEP
    count = jnp.sum((slot >= 0).reshape(N_EXPERTS, n_tiles, n_sub, MOE_SUB), axis=3)
    npass = jnp.maximum(1, (jnp.max(count, axis=2) + MOE_SLOTS - 1) // MOE_SLOTS).T.reshape(-1).astype(jnp.int32)
    full = lambda a: pl.BlockSpec(a.shape, lambda i, e, *_: (0, 0), pipeline_mode=pl.Buffered(1))
    experts = lambda a: pl.BlockSpec((None, per) + a.shape[2:], lambda i, e, *_: (layer, e, 0, 0))
    grid_spec = pltpu.PrefetchScalarGridSpec(
        num_scalar_prefetch=1,
        grid=(n_tiles, N_EXPERTS // per),
        in_specs=[pl.BlockSpec((tm, D_MODEL), lambda i, e, *_: (i, 0)),
                  pl.BlockSpec((N_EXPERTS, tm), lambda i, e, *_: (0, i)),
                  pl.BlockSpec((tm, N_EXPERTS), lambda i, e, *_: (i, 0)),
                  pl.BlockSpec((tm, 2), lambda i, e, *_: (i, 0)),
                  pl.BlockSpec((N_EXPERTS, tm), lambda i, e, *_: (0, i)),
                  pl.BlockSpec((tm, PLE_DIM), lambda i, e, *_: (i, 0)),
                  experts(wg), experts(wu), experts(wd),
                  full(pg), full(pp), full(g), full(b)],
        out_specs=pl.BlockSpec((tm, D_MODEL), lambda i, e, *_: (i, 0)),
        scratch_shapes=[pltpu.VMEM((tm, D_MODEL), jnp.bfloat16),
                        pltpu.VMEM((n_sub, N_EXPERTS * MOE_SLOTS, D_MODEL), jnp.bfloat16)],
    )
    return pl.pallas_call(
        functools.partial(_moe_kernel, alpha=alpha, n_sub=n_sub),
        grid_spec=grid_spec,
        out_shape=jax.ShapeDtypeStruct((T, D_MODEL), jnp.float32),
        compiler_params=_cparams(("parallel", "arbitrary"), MOE_VMEM_LIMIT),
        name="moe_ple_ln",
    )(npass, h2, slot, slot.T, dest.T, gate, p2, wg, wu, wd, pg, pp, g, b)


def _rotate_half_cols(w):
    half = w.shape[1] // 2
    return jnp.concatenate([-w[:, half:], w[:, :half]], axis=1)


def _in_proj_weights(w_in):
    z = lambda n: jnp.zeros((D_MODEL, n), w_in.dtype)
    cq = w_in[:, :MLA_Q_RANK]
    ckv = w_in[:, MLA_Q_RANK:MLA_Q_RANK + MLA_KV_RANK]
    kr = w_in[:, MLA_Q_RANK + MLA_KV_RANK:MLA_IN]
    n_dq = DIFF_HEADS * 2 * DIFF_QK
    sb0 = MLA_IN + DIFF_IN
    n_sq = SB_HEADS * SB_D
    cols = [ckv,
            z(MLA_NOPE), kr, z(LANES - MLA_NOPE - MLA_ROPE),
            z(MLA_NOPE), _rotate_half_cols(kr), z(LANES - MLA_NOPE - MLA_ROPE),
            cq, z(UA_CQ_W - MLA_Q_RANK),
            w_in[:, MLA_IN:MLA_IN + n_dq] * (DIFF_QK ** -0.5 * LOG2E),
            w_in[:, MLA_IN + n_dq:sb0],
            w_in[:, sb0:sb0 + n_sq] * (SB_D ** -0.5),
            w_in[:, sb0 + n_sq:]]
    return _bf16(jnp.concatenate(cols, axis=1))


def _mla_up_weights(w_uq, w_ukv):
    dq = MLA_NOPE + MLA_ROPE
    zq = lambda n: jnp.zeros((MLA_Q_RANK, n), w_uq.dtype)
    plain, rot = [], []
    for h in range(MLA_HEADS):
        wh = w_uq[:, h * dq:(h + 1) * dq]
        plain += [wh, zq(LANES - dq)]
        rot += [zq(MLA_NOPE), _rotate_half_cols(wh[:, MLA_NOPE:]), zq(LANES - dq)]
    wq = jnp.concatenate(plain + rot, axis=1)
    wq = jnp.concatenate([wq, jnp.zeros((UA_CQ_W - MLA_Q_RANK, wq.shape[1]), wq.dtype)], axis=0)
    dkv = MLA_NOPE + MLA_V
    zk = lambda n: jnp.zeros((MLA_KV_RANK, n), w_ukv.dtype)
    kcols, vcols = [], []
    for h in range(MLA_HEADS):
        wh = w_ukv[:, h * dkv:(h + 1) * dkv]
        kcols += [wh[:, :MLA_NOPE], zk(LANES - MLA_NOPE)]
        vcols += [wh[:, MLA_NOPE:]]
    wkv = jnp.concatenate(kcols + vcols, axis=1)
    return _bf16(wq), _bf16(wkv)


def _attn_tiles(S):
    pick = lambda want: max(c for c in (128, 256, 512, 1024) if c <= want and S % c == 0)
    return pick(512), pick(512), pick(256)


def kernel(x, p, positions, w_in, mla_q_norm, mla_w_uq, mla_kv_norm, mla_w_ukv, diff_lambda_q1, diff_lambda_k1, diff_lambda_q2, diff_lambda_k2, diff_subln, rel_bias, w_o, ln1_g, ln1_b, router_w, router_b, w_gate, w_up, w_down, ple_proj, ple_gate, ln2_g, ln2_b):
    B, S, _ = x.shape
    depth = w_in.shape[0]
    T = B * S
    alpha = (2 * depth) ** 0.25
    t_mla, t_diff, t_sb = _attn_tiles(S)
    nq = S // t_diff

    pos_col = positions.reshape(T, 1)
    pos_tiles = positions.reshape(B * nq, 1, t_diff)
    tile_pos = positions.reshape(B * nq, t_diff)
    qmin = jnp.min(tile_pos, axis=1)
    kmax = jnp.max(tile_pos, axis=1)
    first = tile_pos[:, 0]
    consec = jnp.all(tile_pos[:, 1:] - tile_pos[:, :-1] == 1, axis=1).astype(jnp.int32)
    rb_flat = rel_bias.T.reshape(-1).astype(jnp.float32)

    half = MLA_ROPE // 2
    inv = ROPE_THETA ** (-jnp.arange(half, dtype=jnp.float32) / half)
    cos_t, sin_t = _rope_table(positions.reshape(1, T), jnp.concatenate([inv, inv]).reshape(MLA_ROPE, 1))
    pad_l = lambda v: jnp.full((T, MLA_NOPE), v, jnp.float32)
    pad_r = jnp.zeros((T, LANES - MLA_NOPE - MLA_ROPE), jnp.float32)
    cosf = jnp.concatenate([pad_l(1.0), cos_t.T, pad_r], axis=1)
    sinf = jnp.concatenate([pad_l(0.0), sin_t.T, pad_r], axis=1)

    rw_pad = jnp.pad(router_w.astype(jnp.float32), ((0, 0), (0, LANES - N_EXPERTS)))
    rb_col = router_b.reshape(N_EXPERTS, 1).astype(jnp.float32)
    row = lambda a: a.reshape(1, -1)

    wg_all, wu_all, wd_all = _bf16(w_gate), _bf16(w_up), _bf16(w_down)

    h = x.reshape(T, D_MODEL)
    for i in range(depth):
        lam_init = 0.8 - 0.6 * math.exp(-0.3 * i)
        wq, wkv = _mla_up_weights(mla_w_uq[i], mla_w_ukv[i])
        gq = jnp.pad(mla_q_norm[i], (0, UA_CQ_W - MLA_Q_RANK)).reshape(1, UA_CQ_W)
        ub, q_m, k_m, v_m = _in_proj(h, _in_proj_weights(w_in[i]), cosf, sinf, gq, row(mla_kv_norm[i]), wq, wkv)
        y_mla = _mla_attn(q_m, k_m, v_m, B, S, t=t_mla)
        y_diff = _diff_attn(ub, pos_col, pos_tiles, qmin, kmax, first, consec, rb_flat,
                            row(diff_lambda_q1[i]), row(diff_lambda_k1[i]),
                            row(diff_lambda_q2[i]), row(diff_lambda_k2[i]), row(diff_subln[i]),
                            B, S, lam_init, t=t_diff)
        y_sb = _sb_attn(ub, B, S, t=t_sb)
        wo = _bf16(w_o[i])
        n_m, n_d = MLA_HEADS * MLA_V, DIFF_HEADS * DIFF_V
        h = _out_proj(h, y_mla, y_diff, y_sb, wo[:n_m], wo[n_m:n_m + n_d], wo[n_m + n_d:],
                      row(ln1_g[i]), row(ln1_b[i]), alpha)
        gate, slot, dest = _router(h, rw_pad, rb_col)
        h = _moe(h, slot, dest, gate, p[i].reshape(T, PLE_DIM), wg_all, wu_all, wd_all, i,
                 _bf16(ple_gate[i]), _bf16(ple_proj[i]), row(ln2_g[i]), row(ln2_b[i]), alpha)
    return h.reshape(B, S, D_MODEL)
```
